```python
import math
import jax, jax.numpy as jnp
from jax import lax
import numpy as np

D_MODEL = 2048
BATCH = 2
SEQ = 8192
DEPTH = 2

HEAD_DIM = 128
EPS = 1e-6
NEG = -1e30
N_EVEN = (DEPTH + 1) // 2
N_ODD = DEPTH // 2

A_HEADS = 8
A_PATTERNS = ((128, 1), (512, 4), (2048, 16))
B_HEADS = 8
Q_LORA = 512
KV_LORA = 256
QK_NOPE = 128
QK_ROPE = 64
V_DIM = 128
ROPE_THETA = 10000.0
C_HEADS = 16
MOBA_BLOCK = 256
MOBA_TOPK = 3
MOBA_Q_CHUNK = 16
Q_BLOCK = 128
N_EXPERTS = 16
N_GROUPS = 4
EXPERTS_PER_GROUP = N_EXPERTS // N_GROUPS
TOP_K = 2
EXPERT_FF = 512

A_QKV = 3 * A_HEADS * HEAD_DIM
EVEN_IN = A_QKV + Q_LORA + KV_LORA + QK_ROPE
EVEN_MIX = A_HEADS * HEAD_DIM + B_HEADS * V_DIM
ODD_MIX = C_HEADS * HEAD_DIM

kernel_name = 'hybrid_dilated_mla_moba_grouped_moe'


def rmsnorm(x, g):
    xf = x.astype(jnp.float32)
    y = xf * lax.rsqrt(jnp.mean(xf * xf, axis=-1, keepdims=True) + EPS)
    return (y * g.astype(jnp.float32)).astype(x.dtype)


def alibi_slopes(n):
    return 2.0 ** (-8.0 * jnp.arange(1, n + 1, dtype=jnp.float32) / n)


def rope(x, pos):
    r = x.shape[-1]
    inv = ROPE_THETA ** (-jnp.arange(0, r, 2, dtype=jnp.float32) / r)
    ang = pos.astype(jnp.float32)[:, None] * inv[None, :]
    cos = jnp.cos(ang)[None, :, None, :]
    sin = jnp.sin(ang)[None, :, None, :]
    x1, x2 = x[..., : r // 2], x[..., r // 2:]
    return jnp.concatenate([x1 * cos - x2 * sin, x1 * sin + x2 * cos], axis=-1)


def _dilated_partials(q, k, v, slopes, dil, band):
    B, S, H, Dh = q.shape
    L = S // dil
    nb = -(-L // band)
    Lp = nb * band

    def to_blocks(t):
        t = t.reshape(B, L, dil, H, Dh).transpose(0, 2, 1, 3, 4).reshape(B * dil, L, H, Dh)
        t = jnp.pad(t, ((0, 0), (0, Lp - L), (0, 0), (0, 0)))
        return t.reshape(B * dil, nb, band, H, Dh)

    qs, ks, vs = to_blocks(q), to_blocks(k), to_blocks(v)

    def with_prev(t):
        prev = jnp.pad(t, ((0, 0), (1, 0), (0, 0), (0, 0), (0, 0)))[:, :-1]
        return jnp.concatenate([prev, t], axis=2)

    kk, vv = with_prev(ks), with_prev(vs)
    qpos = jnp.arange(Lp).reshape(nb, band)
    kpos = qpos[:, :1] - band + jnp.arange(2 * band)[None, :]
    dist = qpos[:, :, None] - kpos[:, None, :]
    valid = (dist >= 0) & (dist <= band) & (kpos[:, None, :] >= 0)
    bias = -slopes[None, :, None, None] * (dist * dil).astype(jnp.float32)[:, None]
    s = jnp.einsum('xnqhd,xnkhd->xnhqk', qs, kk) * (HEAD_DIM ** -0.5)
    s = jnp.where(valid[:, None], s + bias, NEG)
    m = jnp.max(s, axis=-1)
    p = jnp.exp(s - m[..., None])
    l = jnp.sum(p, axis=-1)
    acc = jnp.einsum('xnhqk,xnkhd->xnqhd', p, vv)

    def back(t):
        t = t.reshape((B * dil, Lp) + t.shape[3:])[:, :L]
        t = t.reshape((B, dil, L) + t.shape[2:])
        return jnp.swapaxes(t, 1, 2).reshape((B, S) + t.shape[3:])

    return back(jnp.swapaxes(m, 2, 3)), back(jnp.swapaxes(l, 2, 3)), back(acc)


def dilated_attention(q, k, v, slopes):
    parts = [_dilated_partials(q, k, v, slopes, d, w // d) for (w, d) in A_PATTERNS]
    m_all = parts[0][0]
    for m_i, _, _ in parts[1:]:
        m_all = jnp.maximum(m_all, m_i)
    num = 0.0
    den = 0.0
    for m_i, l_i, acc_i in parts:
        w_i = jnp.exp(m_i - m_all)
        num = num + w_i[..., None] * acc_i
        den = den + w_i * l_i
    return num / den[..., None]


def causal_dense_attention(q, k, v):
    B, S, H, Dq = q.shape
    nqb = S // Q_BLOCK
    qb = q.reshape(B, nqb, Q_BLOCK, H, Dq).transpose(1, 0, 2, 3, 4)
    kpos = jnp.arange(S)
    scale = Dq ** -0.5

    def one(args):
        qi, i = args
        s = jnp.einsum('bqhd,bkhd->bhqk', qi, k) * scale
        qpos = i * Q_BLOCK + jnp.arange(Q_BLOCK)
        s = jnp.where(kpos[None, :] <= qpos[:, None], s, NEG)
        p = jax.nn.softmax(s, axis=-1)
        return jnp.einsum('bhqk,bkhd->bqhd', p, v)

    out = lax.map(one, (qb, jnp.arange(nqb)))
    return out.transpose(1, 0, 2, 3, 4).reshape(B, S, H, v.shape[-1])


def moba_attention(q, k, v, slopes):
    B, S, H, Dh = q.shape
    MB = MOBA_BLOCK
    nblk = -(-S // MB)
    Sp = nblk * MB
    pad = ((0, 0), (0, Sp - S), (0, 0), (0, 0))
    q, k, v = jnp.pad(q, pad), jnp.pad(k, pad), jnp.pad(v, pad)
    scale = Dh ** -0.5
    kb = k.reshape(B, nblk, MB, H, Dh)
    vb = v.reshape(B, nblk, MB, H, Dh)
    qblk = jnp.arange(Sp) // MB

    kmean = jnp.mean(kb, axis=2)
    gate = jnp.einsum('bshd,bjhd->bshj', q, kmean)
    past = jnp.arange(nblk)[None, :] < qblk[:, None]
    gate = jnp.where(past[None, :, None, :], gate, NEG)
    k_sel = min(MOBA_TOPK, nblk)
    _, sel = lax.top_k(gate, k_sel)

    qb = q.reshape(B, nblk, MB, H, Dh)
    i = jnp.arange(MB)
    dist_own = (i[:, None] - i[None, :])
    s_own = jnp.einsum('bnqhd,bnkhd->bnhqk', qb, kb) * scale
    s_own = jnp.where(dist_own >= 0, s_own - slopes[:, None, None] * dist_own.astype(jnp.float32), NEG)
    m_own = jnp.max(s_own, axis=-1)
    p_own = jnp.exp(s_own - m_own[..., None])
    l_own = jnp.sum(p_own, axis=-1)
    acc_own = jnp.einsum('bnhqk,bnkhd->bnqhd', p_own, vb).reshape(B, Sp, H, Dh)
    m_own = jnp.swapaxes(m_own, 2, 3).reshape(B, Sp, H)
    l_own = jnp.swapaxes(l_own, 2, 3).reshape(B, Sp, H)

    kh = kb.transpose(0, 3, 1, 2, 4)
    vh = vb.transpose(0, 3, 1, 2, 4)
    QC = MOBA_Q_CHUNK
    nc = Sp // QC
    qc = q.reshape(B, nc, QC, H, Dh).transpose(1, 0, 2, 3, 4)
    selc = sel.reshape(B, nc, QC, H, k_sel).transpose(1, 0, 2, 3, 4)
    bi = jnp.arange(B)[:, None, None, None]
    hi = jnp.arange(H)[None, None, :, None]

    def chunk(args):
        qi, si, c = args
        kg = kh[bi, hi, si]
        vg = vh[bi, hi, si]
        s = jnp.einsum('bqhd,bqhjkd->bqhjk', qi, kg) * scale
        qpos = c * QC + jnp.arange(QC)
        kpos = si[..., None] * MB + jnp.arange(MB)
        dist = (qpos[None, :, None, None, None] - kpos).astype(jnp.float32)
        slot_ok = jnp.arange(k_sel)[None, :] < (qpos // MB)[:, None]
        s = jnp.where(slot_ok[None, :, None, :, None], s - slopes[None, None, :, None, None] * dist, NEG)
        s = s.reshape(B, QC, H, k_sel * MB)
        m = jnp.max(s, axis=-1)
        p = jnp.exp(s - m[..., None])
        l = jnp.sum(p, axis=-1)
        acc = jnp.einsum('bqhk,bqhkd->bqhd', p, vg.reshape(B, QC, H, k_sel * MB, Dh))
        return m, l, acc

    m_sel, l_sel, acc_sel = lax.map(chunk, (qc, selc, jnp.arange(nc)))
    m_sel = m_sel.transpose(1, 0, 2, 3).reshape(B, Sp, H)
    l_sel = l_sel.transpose(1, 0, 2, 3).reshape(B, Sp, H)
    acc_sel = acc_sel.transpose(1, 0, 2, 3, 4).reshape(B, Sp, H, Dh)

    m_all = jnp.maximum(m_own, m_sel)
    w_own = jnp.exp(m_own - m_all)
    w_sel = jnp.exp(m_sel - m_all)
    out = (w_own[..., None] * acc_own + w_sel[..., None] * acc_sel) / (w_own * l_own + w_sel * l_sel)[..., None]
    return out[:, :S]


def even_mixer(h, w_in, g_qa, w_q_b, g_kva, w_kv_b, g_qn_a, g_kn_a, g_qn_b, g_kn_b, w_o):
    B, S, _ = h.shape
    proj = h @ w_in
    qkv_a = proj[..., :A_QKV].reshape(B, S, 3, A_HEADS, HEAD_DIM)
    q_lat = proj[..., A_QKV:A_QKV + Q_LORA]
    kv_lat = proj[..., A_QKV + Q_LORA:A_QKV + Q_LORA + KV_LORA]
    k_pe = proj[..., A_QKV + Q_LORA + KV_LORA:]

    qa = rmsnorm(qkv_a[:, :, 0], g_qn_a).astype(jnp.float32)
    ka = rmsnorm(qkv_a[:, :, 1], g_kn_a).astype(jnp.float32)
    va = qkv_a[:, :, 2].astype(jnp.float32)
    out_a = dilated_attention(qa, ka, va, alibi_slopes(A_HEADS))

    qb = (rmsnorm(q_lat, g_qa) @ w_q_b).reshape(B, S, B_HEADS, QK_NOPE + QK_ROPE)
    ckv = (rmsnorm(kv_lat, g_kva) @ w_kv_b).reshape(B, S, B_HEADS, QK_NOPE + V_DIM)
    k_nope, vb = ckv[..., :QK_NOPE], ckv[..., QK_NOPE:]
    kb = jnp.concatenate([k_nope, jnp.broadcast_to(k_pe[:, :, None, :], (B, S, B_HEADS, QK_ROPE))], axis=-1)
    qb = rmsnorm(qb, g_qn_b).astype(jnp.float32)
    kb = rmsnorm(kb, g_kn_b).astype(jnp.float32)
    pos = jnp.arange(S)
    qb = jnp.concatenate([qb[..., :QK_NOPE], rope(qb[..., QK_NOPE:], pos)], axis=-1)
    kb = jnp.concatenate([kb[..., :QK_NOPE], rope(kb[..., QK_NOPE:], pos)], axis=-1)
    out_b = causal_dense_attention(qb, kb, vb.astype(jnp.float32))

    mixed = jnp.concatenate([out_a.reshape(B, S, -1), out_b.reshape(B, S, -1)], axis=-1)
    return mixed.astype(h.dtype) @ w_o


def odd_mixer(h, w_qkv, g_qn, g_kn, w_o):
    B, S, _ = h.shape
    qkv = (h @ w_qkv).reshape(B, S, 3, C_HEADS, HEAD_DIM)
    q = rmsnorm(qkv[:, :, 0], g_qn).astype(jnp.float32)
    k = rmsnorm(qkv[:, :, 1], g_kn).astype(jnp.float32)
    v = qkv[:, :, 2].astype(jnp.float32)
    out = moba_attention(q, k, v, alibi_slopes(C_HEADS))
    return out.reshape(B, S, -1).astype(h.dtype) @ w_o


def grouped_moe(h, w_router, router_bias, w_gate, w_up, w_down):
    B, S, D = h.shape
    t = h.reshape(B * S, D)
    scores = jax.nn.sigmoid((t @ w_router).astype(jnp.float32))
    biased = scores + router_bias.astype(jnp.float32)
    grp = biased.reshape(-1, N_GROUPS, EXPERTS_PER_GROUP)
    grp_score = jnp.sum(lax.top_k(grp, TOP_K)[0], axis=-1)
    best = jnp.argmax(grp_score, axis=-1)
    in_group = (jnp.arange(N_EXPERTS) // EXPERTS_PER_GROUP)[None, :] == best[:, None]
    _, idx = lax.top_k(jnp.where(in_group, biased, NEG), TOP_K)
    sel = jnp.take_along_axis(scores, idx, axis=-1)
    gates = sel / jnp.sum(sel, axis=-1, keepdims=True)
    dense_gate = jnp.einsum('nk,nke->ne', gates, jax.nn.one_hot(idx, N_EXPERTS, dtype=jnp.float32))
    out = jnp.zeros((B * S, D), jnp.float32)
    for e in range(N_EXPERTS):
        y = (jax.nn.silu(t @ w_gate[e]) * (t @ w_up[e])) @ w_down[e]
        out = out + dense_gate[:, e:e + 1] * y.astype(jnp.float32)
    return out.astype(h.dtype).reshape(B, S, D)


def setup_inputs(seed: int = 0) -> dict:
    key = jax.random.key(seed)
    keys = iter(jax.random.split(key, 32))
    out_scale = (2.0 * DEPTH) ** -0.5

    def w(shape, fan_in, scale=1.0):
        return jax.random.normal(next(keys), shape, jnp.float32) * (scale * fan_in ** -0.5)

    def gain(shape):
        return 1.0 + 0.02 * jax.random.normal(next(keys), shape, jnp.float32)

    return {
        'x': jax.random.normal(next(keys), (BATCH, SEQ, D_MODEL), jnp.float32),
        'attn_norm': gain((DEPTH, D_MODEL)),
        'ev_w_in': w((N_EVEN, D_MODEL, EVEN_IN), D_MODEL),
        'ev_q_a_norm': gain((N_EVEN, Q_LORA)),
        'ev_w_q_b': w((N_EVEN, Q_LORA, B_HEADS * (QK_NOPE + QK_ROPE)), Q_LORA),
        'ev_kv_a_norm': gain((N_EVEN, KV_LORA)),
        'ev_w_kv_b': w((N_EVEN, KV_LORA, B_HEADS * (QK_NOPE + V_DIM)), KV_LORA),
        'ev_qn_a': gain((N_EVEN, HEAD_DIM)),
        'ev_kn_a': gain((N_EVEN, HEAD_DIM)),
        'ev_qn_b': gain((N_EVEN, QK_NOPE + QK_ROPE)),
        'ev_kn_b': gain((N_EVEN, QK_NOPE + QK_ROPE)),
        'ev_w_o': w((N_EVEN, EVEN_MIX, D_MODEL), EVEN_MIX, out_scale),
        'od_w_qkv': w((N_ODD, D_MODEL, 3 * ODD_MIX), D_MODEL),
        'od_qn': gain((N_ODD, HEAD_DIM)),
        'od_kn': gain((N_ODD, HEAD_DIM)),
        'od_w_o': w((N_ODD, ODD_MIX, D_MODEL), ODD_MIX, out_scale),
        'ffn_norm': gain((DEPTH, D_MODEL)),
        'w_router': w((D_MODEL, N_EXPERTS), D_MODEL),
        'router_bias': 0.01 * jax.random.normal(next(keys), (N_EXPERTS,), jnp.float32),
        'w_gate': w((DEPTH, N_EXPERTS, D_MODEL, EXPERT_FF), D_MODEL),
        'w_up': w((DEPTH, N_EXPERTS, D_MODEL, EXPERT_FF), D_MODEL),
        'w_down': w((DEPTH, N_EXPERTS, EXPERT_FF, D_MODEL), EXPERT_FF, out_scale),
    }


def reference(x, attn_norm, ev_w_in, ev_q_a_norm, ev_w_q_b, ev_kv_a_norm, ev_w_kv_b,
              ev_qn_a, ev_kn_a, ev_qn_b, ev_kn_b, ev_w_o, od_w_qkv, od_qn, od_kn, od_w_o,
              ffn_norm, w_router, router_bias, w_gate, w_up, w_down):
    h = x
    for layer in range(DEPTH):
        hn = rmsnorm(h, attn_norm[layer])
        if layer % 2 == 0:
            i = layer // 2
            h = h + even_mixer(hn, ev_w_in[i], ev_q_a_norm[i], ev_w_q_b[i], ev_kv_a_norm[i],
                               ev_w_kv_b[i], ev_qn_a[i], ev_kn_a[i], ev_qn_b[i], ev_kn_b[i], ev_w_o[i])
        else:
            i = layer // 2
            h = h + odd_mixer(hn, od_w_qkv[i], od_qn[i], od_kn[i], od_w_o[i])
        h = h + grouped_moe(rmsnorm(h, ffn_norm[layer]), w_router, router_bias,
                            w_gate[layer], w_up[layer], w_down[layer])
    return h
```

```python
import functools
import math

import jax
import jax.numpy as jnp
from jax import lax
from jax.experimental import pallas as pl
from jax.experimental.pallas import tpu as pltpu

EPS = 1e-6
NEG = -1e30
LANES = 128
HEAD_DIM = 128
ROPE_DIM = 64
ROPE_THETA = 10000.0
A_HEADS = 8
A_PATTERNS = ((128, 1), (512, 4), (2048, 16))
B_HEADS = 8
Q_LORA = 512
KV_LORA = 256
MLA_QK = 192
MLA_QK_PAD = 256
C_HEADS = 16
MOBA_BLOCK = 256
MOBA_TOPK = 3
N_EXPERTS = 16
N_GROUPS = 4
GROUP = N_EXPERTS // N_GROUPS
TOP_K = 2
EXPERT_FF = 512
A_QKV = 3 * A_HEADS * HEAD_DIM
EVEN_IN_PAD = 4096
KPE_COL = A_QKV + Q_LORA + KV_LORA
MIB = 1024 * 1024

_NT = (((1,), (1,)), ((), ()))


def _params(semantics, vmem_mib):
    return pltpu.CompilerParams(dimension_semantics=semantics,
                                vmem_limit_bytes=vmem_mib * MIB)


def _rms_scale(x, width):
    return lax.rsqrt(jnp.sum(x * x, axis=-1, keepdims=True) * (1.0 / width) + EPS)


def _norm_matmul_kernel(x_ref, g_ref, w_ref, o_ref, xn_ref, *, width):
    @pl.when(pl.program_id(1) == 0)
    def _():
        x = x_ref[...]
        xn_ref[...] = (x * _rms_scale(x, width) * g_ref[...]).astype(jnp.bfloat16)

    o_ref[...] = jnp.dot(xn_ref[...], w_ref[...],
                         preferred_element_type=jnp.float32).astype(o_ref.dtype)


def norm_matmul(x, g, w, *, col_block=0, tm=512, tn=512):
    n = x.shape[0]
    k, m = w.shape
    return pl.pallas_call(
        functools.partial(_norm_matmul_kernel, width=k),
        grid=(n // tm, m // tn),
        in_specs=[pl.BlockSpec((tm, k), lambda i, j: (i, col_block)),
                  pl.BlockSpec((1, k), lambda i, j: (0, 0)),
                  pl.BlockSpec((k, tn), lambda i, j: (0, j))],
        out_specs=pl.BlockSpec((tm, tn), lambda i, j: (i, j)),
        out_shape=jax.ShapeDtypeStruct((n, m), jnp.float32),
        scratch_shapes=[pltpu.VMEM((tm, k), jnp.bfloat16)],
        compiler_params=_params(("parallel", "arbitrary"), 40),
        name="norm_matmul",
    )(x, g.reshape(1, k), w)


def _matmul_res_kernel(*refs, n_a):
    a_refs, w_refs = refs[:n_a], refs[n_a:2 * n_a]
    r_ref, o_ref = refs[2 * n_a], refs[2 * n_a + 1]
    acc = r_ref[...]
    for a_ref, w_ref in zip(a_refs, w_refs):
        acc = acc + jnp.dot(a_ref[...], w_ref[...], preferred_element_type=jnp.float32)
    o_ref[...] = acc


def matmul_residual(a_list, w, res, *, tm=512, tn=1024):
    n, m = res.shape
    n_a = len(a_list)
    ka = a_list[0].shape[1]
    in_specs = [pl.BlockSpec((tm, ka), lambda i, j: (i, 0)) for _ in a_list]
    in_specs += [pl.BlockSpec((ka, tn), functools.partial(lambda i, j, p: (p, j), p=p))
                 for p in range(n_a)]
    in_specs += [pl.BlockSpec((tm, tn), lambda i, j: (i, j))]
    return pl.pallas_call(
        functools.partial(_matmul_res_kernel, n_a=n_a),
        grid=(n // tm, m // tn),
        in_specs=in_specs,
        out_specs=pl.BlockSpec((tm, tn), lambda i, j: (i, j)),
        out_shape=jax.ShapeDtypeStruct((n, m), jnp.float32),
        compiler_params=_params(("parallel", "parallel"), 40),
        name="matmul_residual",
    )(*a_list, *([w] * n_a), res)


def _prep_a_kernel(q_ref, k_ref, v_ref, gq_ref, gk_ref, qo_ref, ko_ref, vo_ref, *, scale):
    for h in range(A_HEADS):
        sl = slice(h * HEAD_DIM, (h + 1) * HEAD_DIM)
        q = q_ref[:, sl]
        qo_ref[:, sl] = (q * _rms_scale(q, HEAD_DIM) * gq_ref[...] * scale).astype(jnp.bfloat16)
        k = k_ref[:, sl]
        ko_ref[:, sl] = (k * _rms_scale(k, HEAD_DIM) * gk_ref[...]).astype(jnp.bfloat16)
    vo_ref[...] = v_ref[...].astype(jnp.bfloat16)


def prep_a(proj, gq, gk, *, tm=512):
    n = proj.shape[0]
    w = A_HEADS * HEAD_DIM
    blk = lambda c: pl.BlockSpec((tm, w), functools.partial(lambda i, c: (i, c), c=c))
    gspec = pl.BlockSpec((1, HEAD_DIM), lambda i: (0, 0))
    out = jax.ShapeDtypeStruct((n, w), jnp.bfloat16)
    return pl.pallas_call(
        functools.partial(_prep_a_kernel, scale=HEAD_DIM ** -0.5),
        grid=(n // tm,),
        in_specs=[blk(0), blk(1), blk(2), gspec, gspec],
        out_specs=[blk(0)] * 3,
        out_shape=[out] * 3,
        compiler_params=_params(("parallel",), 40),
        name="prep_a",
    )(proj, proj, proj, gq.reshape(1, -1), gk.reshape(1, -1))


def _dilated_kernel(*refs, dil, first, last):
    q_ref, kc_ref, kp_ref, vc_ref, vp_ref = refs[:5]
    if first:
        o_ref, lse_ref = refs[5:]
    elif last:
        oprev_ref, lprev_ref, o_ref = refs[5:]
    else:
        oprev_ref, lprev_ref, o_ref, lse_ref = refs[5:]
    band = HEAD_DIM
    n = pl.program_id(2)
    qi = lax.broadcasted_iota(jnp.int32, (band, band), 0)
    ki = lax.broadcasted_iota(jnp.int32, (band, band), 1)
    dist_c = (qi - ki).astype(jnp.float32)
    dist_p = dist_c + float(band)
    valid_c = qi >= ki
    valid_p = jnp.logical_and(ki >= qi, n > 0)
    for h in range(A_HEADS):
        sl = slice(h * HEAD_DIM, (h + 1) * HEAD_DIM)
        slope = 2.0 ** (-8.0 * (h + 1) / A_HEADS) * dil
        q = q_ref[0, :, sl]
        sc = lax.dot_general(q, kc_ref[0, :, sl], _NT, preferred_element_type=jnp.float32)
        sp = lax.dot_general(q, kp_ref[0, :, sl], _NT, preferred_element_type=jnp.float32)
        sc = jnp.where(valid_c, sc - slope * dist_c, NEG)
        sp = jnp.where(valid_p, sp - slope * dist_p, NEG)
        m = jnp.maximum(jnp.max(sc, axis=1, keepdims=True), jnp.max(sp, axis=1, keepdims=True))
        pc = jnp.exp(sc - m)
        pp = jnp.exp(sp - m)
        l = jnp.sum(pc, axis=1, keepdims=True) + jnp.sum(pp, axis=1, keepdims=True)
        acc = jnp.dot(pc.astype(jnp.bfloat16), vc_ref[0, :, sl], preferred_element_type=jnp.float32)
        acc = acc + jnp.dot(pp.astype(jnp.bfloat16), vp_ref[0, :, sl],
                            preferred_element_type=jnp.float32)
        o = acc / l
        lse = jnp.broadcast_to(m + jnp.log(l), (band, HEAD_DIM))
        if not first:
            lprev = lprev_ref[0, :, sl]
            mx = jnp.maximum(lprev, lse)
            w_prev = jnp.exp(lprev - mx)
            w_new = jnp.exp(lse - mx)
            den = w_prev + w_new
            o = (w_prev * oprev_ref[0, :, sl] + w_new * o) / den
            lse = mx + jnp.log(den)
        o_ref[0, :, sl] = o.astype(o_ref.dtype)
        if not last:
            lse_ref[0, :, sl] = lse


def dilated_pattern(q, k, v, state, *, batch, seq, dil, first, last):
    n_tok, w = q.shape
    band = HEAD_DIM
    sub = seq // dil
    view = lambda t: t.reshape(batch, sub, dil * w)
    cur = pl.BlockSpec((1, band, w), lambda b, r, n: (b, n, r))
    prev = pl.BlockSpec((1, band, w), lambda b, r, n: (b, jnp.maximum(n - 1, 0), r))
    ins = [view(q), view(k), view(k), view(v), view(v)]
    in_specs = [cur, cur, prev, cur, prev]
    if not first:
        ins += [view(state[0]), view(state[1])]
        in_specs += [cur, cur]
    f32 = jax.ShapeDtypeStruct((batch, sub, dil * w), jnp.float32)
    if last:
        out_shape = [jax.ShapeDtypeStruct((batch, sub, dil * w), jnp.bfloat16)]
        out_specs = [cur]
    else:
        out_shape = [f32, f32]
        out_specs = [cur, cur]
    outs = pl.pallas_call(
        functools.partial(_dilated_kernel, dil=dil, first=first, last=last),
        grid=(batch, dil, sub // band),
        in_specs=in_specs,
        out_specs=out_specs,
        out_shape=out_shape,
        compiler_params=_params(("parallel", "parallel", "arbitrary"), 40),
        name=f"dilated_d{dil}",
    )(*ins)
    return [o.reshape(n_tok, w) for o in outs]


def dilated_attention(q, k, v, *, batch, seq):
    state = None
    for idx, (window, dil) in enumerate(A_PATTERNS):
        assert window // dil == HEAD_DIM
        state = dilated_pattern(q, k, v, state, batch=batch, seq=seq, dil=dil,
                                first=idx == 0, last=idx == len(A_PATTERNS) - 1)
    return state[0]


def _rope(t, cos, sin_lo, sin_hi):
    return (t * cos + pltpu.roll(t, LANES - ROPE_DIM // 2, 1) * sin_lo
            + pltpu.roll(t, ROPE_DIM // 2, 1) * sin_hi)


def _prep_b_kernel(qb_ref, ckv_ref, kpe_ref, gq_ref, gkn_ref, gkp_ref, cos_ref, slo_ref, shi_ref,
                   qo_ref, ko_ref, vo_ref, *, scale):
    cos, slo, shi = cos_ref[...], slo_ref[...], shi_ref[...]
    kpe = kpe_ref[...]
    kpe_ss = jnp.sum(kpe * kpe, axis=-1, keepdims=True)
    kpe_rot = _rope(kpe * gkp_ref[...], cos, slo, shi)
    for h in range(B_HEADS):
        lo = slice(h * MLA_QK_PAD, h * MLA_QK_PAD + HEAD_DIM)
        hi = slice(h * MLA_QK_PAD + HEAD_DIM, (h + 1) * MLA_QK_PAD)
        q_n, q_p = qb_ref[:, lo], qb_ref[:, hi]
        ss = jnp.sum(q_n * q_n, axis=-1, keepdims=True) + jnp.sum(q_p * q_p, axis=-1, keepdims=True)
        rq = lax.rsqrt(ss * (1.0 / MLA_QK) + EPS) * scale
        qo_ref[h, :, :HEAD_DIM] = (q_n * rq * gq_ref[:, :HEAD_DIM]).astype(jnp.bfloat16)
        qo_ref[h, :, HEAD_DIM:] = _rope(q_p * rq * gq_ref[:, HEAD_DIM:], cos, slo, shi
                                        ).astype(jnp.bfloat16)
        k_n = ckv_ref[:, lo]
        ss = jnp.sum(k_n * k_n, axis=-1, keepdims=True) + kpe_ss
        rk = lax.rsqrt(ss * (1.0 / MLA_QK) + EPS)
        ko_ref[h, :, :HEAD_DIM] = (k_n * rk * gkn_ref[...]).astype(jnp.bfloat16)
        ko_ref[h, :, HEAD_DIM:] = (kpe_rot * rk).astype(jnp.bfloat16)
        vo_ref[h, :, :] = ckv_ref[:, hi].astype(jnp.bfloat16)


def prep_b(qb, ckv, proj, gq_pad, gk_nope, gk_pe_pad, rope_tabs, *, seq, tm=256):
    n = qb.shape[0]
    wide = B_HEADS * MLA_QK_PAD
    nblk = seq // tm
    row = lambda c: pl.BlockSpec((tm, wide), lambda i: (i, 0))
    tab = pl.BlockSpec((tm, LANES), lambda i: (i % nblk, 0))
    return pl.pallas_call(
        functools.partial(_prep_b_kernel, scale=MLA_QK ** -0.5),
        grid=(n // tm,),
        in_specs=[row(0), row(0),
                  pl.BlockSpec((tm, LANES), lambda i: (i, KPE_COL // LANES)),
                  pl.BlockSpec((1, MLA_QK_PAD), lambda i: (0, 0)),
                  pl.BlockSpec((1, LANES), lambda i: (0, 0)),
                  pl.BlockSpec((1, LANES), lambda i: (0, 0)),
                  tab, tab, tab],
        out_specs=[pl.BlockSpec((B_HEADS, tm, MLA_QK_PAD), lambda i: (0, i, 0)),
                   pl.BlockSpec((B_HEADS, tm, MLA_QK_PAD), lambda i: (0, i, 0)),
                   pl.BlockSpec((B_HEADS, tm, HEAD_DIM), lambda i: (0, i, 0))],
        out_shape=[jax.ShapeDtypeStruct((B_HEADS, n, MLA_QK_PAD), jnp.bfloat16),
                   jax.ShapeDtypeStruct((B_HEADS, n, MLA_QK_PAD), jnp.bfloat16),
                   jax.ShapeDtypeStruct((B_HEADS, n, HEAD_DIM), jnp.bfloat16)],
        compiler_params=_params(("parallel",), 40),
        name="prep_b",
    )(qb, ckv, proj, gq_pad, gk_nope, gk_pe_pad, *rope_tabs)


def rope_tables(seq):
    half = ROPE_DIM // 2
    inv = ROPE_THETA ** (-jnp.arange(0, ROPE_DIM, 2, dtype=jnp.float32) / ROPE_DIM)
    ang = jnp.arange(seq, dtype=jnp.float32)[:, None] * inv[None, :]
    cos, sin = jnp.cos(ang), jnp.sin(ang)
    z = jnp.zeros((seq, half), jnp.float32)
    pad = jnp.zeros((seq, LANES - ROPE_DIM), jnp.float32)
    return (jnp.concatenate([cos, cos, pad], axis=1),
            jnp.concatenate([-sin, z, pad], axis=1),
            jnp.concatenate([z, sin, pad], axis=1))


def _flash_kernel(*refs, tile, moba):
    if moba:
        q_ref, k_ref, v_ref, sel_ref, slope_ref, o_ref, m_ref, l_ref, acc_ref = refs
    else:
        q_ref, k_ref, v_ref, o_ref, m_ref, l_ref, acc_ref = refs
    h = pl.program_id(1)
    i = pl.program_id(2)
    q = q_ref[0]
    m_ref[...] = jnp.full_like(m_ref, NEG)
    l_ref[...] = jnp.zeros_like(l_ref)
    acc_ref[...] = jnp.zeros_like(acc_ref)
    if moba:
        sel = sel_ref[0]
        head_lane = lax.broadcasted_iota(jnp.int32, sel.shape, 1)
        sel_col = jnp.sum(jnp.where(head_lane == h, sel, 0), axis=1, keepdims=True)
        slope_row = slope_ref[0]
        col = lax.broadcasted_iota(jnp.int32, (1, tile), 1).astype(jnp.float32)

    def step(j, s):
        v = v_ref[0, pl.ds(pl.multiple_of(j * tile, tile), tile), :]
        m_prev = m_ref[...]
        m_new = jnp.maximum(m_prev, jnp.max(s, axis=1, keepdims=True))
        alpha = jnp.exp(m_prev - m_new)
        p = jnp.exp(s - m_new)
        l_ref[...] = alpha * l_ref[...] + jnp.sum(p, axis=1, keepdims=True)
        acc_ref[...] = alpha * acc_ref[...] + jnp.dot(p.astype(jnp.bfloat16), v,
                                                       preferred_element_type=jnp.float32)
        m_ref[...] = m_new

    def scores(j):
        k = k_ref[0, pl.ds(pl.multiple_of(j * tile, tile), tile), :]
        s = lax.dot_general(q, k, _NT, preferred_element_type=jnp.float32)
        if moba:
            s = s + slope_row * (col + ((j - i) * tile).astype(jnp.float32))
        return s

    def past(j, carry):
        s = scores(j)
        if moba:
            s = jnp.where((jnp.right_shift(sel_col, j) & 1) > 0, s, NEG)
        step(j, s)
        return carry

    lax.fori_loop(0, i, past, 0)
    qi = lax.broadcasted_iota(jnp.int32, (tile, tile), 0)
    ki = lax.broadcasted_iota(jnp.int32, (tile, tile), 1)
    step(i, jnp.where(qi >= ki, scores(i), NEG))
    o_ref[...] = (acc_ref[...] / l_ref[...]).astype(o_ref.dtype)


def flash_attention(q, k, v, *, batch, seq, tile, sel=None, slopes=None):
    heads, n, dq = q.shape
    dv = v.shape[2]
    nq = seq // tile
    moba = sel is not None
    kv_view = lambda t: t.reshape(heads * batch, seq, t.shape[2])
    in_specs = [pl.BlockSpec((1, tile, dq), lambda b, h, i: (h, b * nq + i, 0)),
                pl.BlockSpec((1, seq, dq), lambda b, h, i: (h * batch + b, 0, 0)),
                pl.BlockSpec((1, seq, dv), lambda b, h, i: (h * batch + b, 0, 0))]
    ins = [q, kv_view(k), kv_view(v)]
    if moba:
        in_specs += [pl.BlockSpec((1, tile, heads), lambda b, h, i: (b * nq + i, 0, 0)),
                     pl.BlockSpec((1, 1, tile), lambda b, h, i: (h, 0, 0))]
        ins += [sel.reshape(n // tile, tile, heads), slopes]
    return pl.pallas_call(
        functools.partial(_flash_kernel, tile=tile, moba=moba),
        grid=(batch, heads, nq),
        in_specs=in_specs,
        out_specs=pl.BlockSpec((tile, dv), lambda b, h, i: (b * nq + i, h)),
        out_shape=jax.ShapeDtypeStruct((n, heads * dv), jnp.bfloat16),
        scratch_shapes=[pltpu.VMEM((tile, 1), jnp.float32),
                        pltpu.VMEM((tile, 1), jnp.float32),
                        pltpu.VMEM((tile, dv), jnp.float32)],
        compiler_params=_params(("parallel", "parallel", "arbitrary"), 40),
        name="flash_moba" if moba else "flash_mla",
    )(*ins)


def _prep_c_kernel(q_ref, k_ref, v_ref, gq_ref, gk_ref, qo_ref, ko_ref, vo_ref, sel_ref,
                   kmean_ref, *, blocks_per_seq, scale):
    i = pl.program_id(0)
    bq = i % blocks_per_seq

    @pl.when(bq == 0)
    def _():
        kmean_ref[...] = jnp.zeros_like(kmean_ref)

    rows = q_ref.shape[0]
    blk = lax.broadcasted_iota(jnp.int32, (rows, blocks_per_seq), 1)
    head_lane = lax.broadcasted_iota(jnp.int32, (rows, C_HEADS), 1)
    sel = jnp.zeros((rows, C_HEADS), jnp.int32)
    for h in range(C_HEADS):
        sl = slice(h * HEAD_DIM, (h + 1) * HEAD_DIM)
        q = q_ref[:, sl]
        qn = q * _rms_scale(q, HEAD_DIM) * gq_ref[...]
        k = k_ref[:, sl]
        kn = k * _rms_scale(k, HEAD_DIM) * gk_ref[...]
        qo_ref[h] = (qn * scale).astype(jnp.bfloat16)
        ko_ref[h] = kn.astype(jnp.bfloat16)
        vo_ref[h] = v_ref[:, sl].astype(jnp.bfloat16)
        gate = lax.dot_general(qn, kmean_ref[h], _NT, precision=lax.Precision.HIGHEST,
                               preferred_element_type=jnp.float32)
        gate = jnp.where(blk < bq, gate, NEG)
        kmean_ref[h, pl.ds(bq, 1), :] = jnp.mean(kn, axis=0, keepdims=True)
        bits = jnp.zeros((rows, 1), jnp.int32)
        for r in range(MOBA_TOPK):
            best = jnp.max(gate, axis=1, keepdims=True)
            idx = jnp.min(jnp.where(gate == best, blk, blocks_per_seq), axis=1, keepdims=True)
            bits = bits | jnp.where(r < bq, jnp.left_shift(1, idx), 0)
            gate = jnp.where(blk == idx, 2.0 * NEG, gate)
        sel = jnp.where(head_lane == h, bits, sel)
    sel_ref[...] = sel


def prep_c(qkv, gq, gk, *, seq):
    n = qkv.shape[0]
    tm = MOBA_BLOCK
    w = C_HEADS * HEAD_DIM
    blocks_per_seq = seq // tm
    blk = lambda c: pl.BlockSpec((tm, w), functools.partial(lambda i, c: (i, c), c=c))
    gspec = pl.BlockSpec((1, HEAD_DIM), lambda i: (0, 0))
    hm = pl.BlockSpec((C_HEADS, tm, HEAD_DIM), lambda i: (0, i, 0))
    hm_shape = jax.ShapeDtypeStruct((C_HEADS, n, HEAD_DIM), jnp.bfloat16)
    return pl.pallas_call(
        functools.partial(_prep_c_kernel, blocks_per_seq=blocks_per_seq, scale=HEAD_DIM ** -0.5),
        grid=(n // tm,),
        in_specs=[blk(0), blk(1), blk(2), gspec, gspec],
        out_specs=[hm, hm, hm, pl.BlockSpec((tm, C_HEADS), lambda i: (i, 0))],
        out_shape=[hm_shape, hm_shape, hm_shape,
                   jax.ShapeDtypeStruct((n, C_HEADS), jnp.int32)],
        scratch_shapes=[pltpu.VMEM((C_HEADS, blocks_per_seq, HEAD_DIM), jnp.float32)],
        compiler_params=_params(("arbitrary",), 40),
        name="prep_c",
    )(qkv, qkv, qkv, gq.reshape(1, -1), gk.reshape(1, -1))


def _router_kernel(h_ref, g_ref, wr_ref, b_ref, t_ref, gate_ref):
    x = h_ref[...]
    t = x * _rms_scale(x, x.shape[1]) * g_ref[...]
    t_ref[...] = t.astype(jnp.bfloat16)
    logits = lax.dot_general(wr_ref[...], t, _NT, precision=lax.Precision.HIGHEST,
                             preferred_element_type=jnp.float32)
    scores = 1.0 / (1.0 + jnp.exp(-logits))
    biased = scores + b_ref[...]
    s = [scores[e:e + 1, :] for e in range(N_EXPERTS)]
    b = [biased[e:e + 1, :] for e in range(N_EXPERTS)]
    best_val, best_grp = None, None
    for g in range(N_GROUPS):
        mem = b[g * GROUP:(g + 1) * GROUP]
        top2 = None
        for x1 in range(GROUP):
            for x2 in range(x1 + 1, GROUP):
                pair = mem[x1] + mem[x2]
                top2 = pair if top2 is None else jnp.maximum(top2, pair)
        if g == 0:
            best_val, best_grp = top2, jnp.zeros_like(top2, dtype=jnp.int32)
        else:
            better = top2 > best_val
            best_grp = jnp.where(better, g, best_grp)
            best_val = jnp.where(better, top2, best_val)
    picked = []
    for e in range(N_EXPERTS):
        g = e // GROUP
        ahead = jnp.zeros_like(best_grp)
        for o in range(g * GROUP, (g + 1) * GROUP):
            if o == e:
                continue
            wins = (b[o] > b[e]) | ((b[o] == b[e]) & (o < e))
            ahead = ahead + wins.astype(jnp.int32)
        picked.append(jnp.where((best_grp == g) & (ahead < TOP_K), s[e], 0.0))
    total = picked[0]
    for e in range(1, N_EXPERTS):
        total = total + picked[e]
    for e in range(N_EXPERTS):
        gate_ref[e:e + 1, :] = picked[e] / total


def router(h, g, w_router_t, bias, *, tm=512):
    n, d = h.shape
    return pl.pallas_call(
        _router_kernel,
        grid=(n // tm,),
        in_specs=[pl.BlockSpec((tm, d), lambda i: (i, 0)),
                  pl.BlockSpec((1, d), lambda i: (0, 0)),
                  pl.BlockSpec((N_EXPERTS, d), lambda i: (0, 0)),
                  pl.BlockSpec((N_EXPERTS, 1), lambda i: (0, 0))],
        out_specs=[pl.BlockSpec((tm, d), lambda i: (i, 0)),
                   pl.BlockSpec((N_EXPERTS, tm), lambda i: (0, i))],
        out_shape=[jax.ShapeDtypeStruct((n, d), jnp.bfloat16),
                   jax.ShapeDtypeStruct((N_EXPERTS, n), jnp.float32)],
        compiler_params=_params(("parallel",), 40),
        name="router",
    )(h, g.reshape(1, d), w_router_t, bias.reshape(N_EXPERTS, 1))


def _moe_kernel(t_ref, gate_ref, wg_ref, wu_ref, wd_ref, h_ref, o_ref):
    e = pl.program_id(1)

    @pl.when(e == 0)
    def _():
        o_ref[...] = h_ref[...]

    t = t_ref[...]
    a = jnp.dot(t, wg_ref[0], preferred_element_type=jnp.float32)
    u = jnp.dot(t, wu_ref[0], preferred_element_type=jnp.float32)
    hid = (a / (1.0 + jnp.exp(-a))) * u
    y = jnp.dot(hid.astype(jnp.bfloat16), wd_ref[0], preferred_element_type=jnp.float32)
    gates = gate_ref[...]
    lane = lax.broadcasted_iota(jnp.int32, gates.shape, 1)
    g_col = jnp.sum(jnp.where(lane == e, gates, 0.0), axis=1, keepdims=True)
    o_ref[...] += g_col * y


def moe_dense(t, gates, wg, wu, wd, h, *, tm=512):
    n, d = h.shape
    ff = wg.shape[2]
    return pl.pallas_call(
        _moe_kernel,
        grid=(n // tm, N_EXPERTS),
        in_specs=[pl.BlockSpec((tm, d), lambda i, e: (i, 0)),
                  pl.BlockSpec((tm, N_EXPERTS), lambda i, e: (i, 0)),
                  pl.BlockSpec((1, d, ff), lambda i, e: (e, 0, 0)),
                  pl.BlockSpec((1, d, ff), lambda i, e: (e, 0, 0)),
                  pl.BlockSpec((1, ff, d), lambda i, e: (e, 0, 0)),
                  pl.BlockSpec((tm, d), lambda i, e: (i, 0))],
        out_specs=pl.BlockSpec((tm, d), lambda i, e: (i, 0)),
        out_shape=jax.ShapeDtypeStruct((n, d), jnp.float32),
        compiler_params=_params(("parallel", "arbitrary"), 48),
        name="moe_dense",
    )(t, gates, wg, wu, wd, h)


def moe_layer(h, ffn_g, w_router_t, router_bias, wg, wu, wd):
    t, gate_t = router(h, ffn_g, w_router_t, router_bias)
    return moe_dense(t, gate_t.T, wg, wu, wd, h)


def _pad_cols(w, width):
    return jnp.pad(w, ((0, 0), (0, width - w.shape[1])))


def _pad_heads(w, heads, real, padded):
    k = w.shape[0]
    w = w.reshape(k, heads, real)
    return jnp.pad(w, ((0, 0), (0, 0), (0, padded - real))).reshape(k, heads * padded)


def kernel(x, attn_norm, ev_w_in, ev_q_a_norm, ev_w_q_b, ev_kv_a_norm, ev_w_kv_b, ev_qn_a, ev_kn_a, ev_qn_b, ev_kn_b, ev_w_o, od_w_qkv, od_qn, od_kn, od_w_o, ffn_norm, w_router, router_bias, w_gate, w_up, w_down):
    batch, seq, d = x.shape
    n = batch * seq
    bf = jnp.bfloat16
    h = x.reshape(n, d)
    w_router_t = w_router.T
    depth = attn_norm.shape[0]
    for layer in range(depth):
        i = layer // 2
        if layer % 2 == 0:
            w_in = _pad_cols(ev_w_in[i], EVEN_IN_PAD).astype(bf)
            proj = norm_matmul(h, attn_norm[layer], w_in)
            qa, ka, va = prep_a(proj, ev_qn_a[i], ev_kn_a[i])
            out_a = dilated_attention(qa, ka, va, batch=batch, seq=seq)
            w_qb = _pad_heads(ev_w_q_b[i], B_HEADS, MLA_QK, MLA_QK_PAD).astype(bf)
            qb = norm_matmul(proj, ev_q_a_norm[i], w_qb, col_block=A_QKV // Q_LORA)
            ckv = norm_matmul(proj, ev_kv_a_norm[i], ev_w_kv_b[i].astype(bf),
                              col_block=(A_QKV + Q_LORA) // KV_LORA)
            gq_pad = jnp.pad(ev_qn_b[i], (0, MLA_QK_PAD - MLA_QK)).reshape(1, -1)
            gk_nope = ev_kn_b[i][:HEAD_DIM].reshape(1, -1)
            gk_pe = jnp.pad(ev_kn_b[i][HEAD_DIM:], (0, LANES - ROPE_DIM)).reshape(1, -1)
            q_b, k_b, v_b = prep_b(qb, ckv, proj, gq_pad, gk_nope, gk_pe, rope_tables(seq), seq=seq)
            out_b = flash_attention(q_b, k_b, v_b, batch=batch, seq=seq, tile=256)
            h = matmul_residual([out_a, out_b], ev_w_o[i].astype(bf), h)
        else:
            qkv = norm_matmul(h, attn_norm[layer], od_w_qkv[i].astype(bf))
            q_c, k_c, v_c, sel = prep_c(qkv, od_qn[i], od_kn[i], seq=seq)
            slopes = 2.0 ** (-8.0 * jnp.arange(1, C_HEADS + 1, dtype=jnp.float32) / C_HEADS)
            slopes = jnp.broadcast_to(slopes[:, None, None], (C_HEADS, 1, MOBA_BLOCK))
            out_c = flash_attention(q_c, k_c, v_c, batch=batch, seq=seq, tile=MOBA_BLOCK,
                                    sel=sel, slopes=slopes)
            h = matmul_residual([out_c], od_w_o[i].astype(bf), h)
        h = moe_layer(h, ffn_norm[layer], w_router_t, router_bias,
                      w_gate[layer].astype(bf), w_up[layer].astype(bf), w_down[layer].astype(bf))
    return h.reshape(batch, seq, d)
```

```python
import functools
import math

import jax
import jax.numpy as jnp
from jax import lax
from jax.experimental import pallas as pl
from jax.experimental.pallas import tpu as pltpu

EPS = 1e-6
NEG = -1e30
LANES = 128
HEAD_DIM = 128
ROPE_DIM = 64
ROPE_THETA = 10000.0
A_HEADS = 8
A_PATTERNS = ((128, 1), (512, 4), (2048, 16))
B_HEADS = 8
Q_LORA = 512
KV_LORA = 256
MLA_QK = 192
MLA_QK_PAD = 256
C_HEADS = 16
MOBA_BLOCK = 256
MOBA_TOPK = 3
N_EXPERTS = 16
N_GROUPS = 4
GROUP = N_EXPERTS // N_GROUPS
TOP_K = 2
EXPERT_FF = 512
A_QKV = 3 * A_HEADS * HEAD_DIM
EVEN_IN_PAD = 4096
KPE_COL = A_QKV + Q_LORA + KV_LORA
MIB = 1024 * 1024
FLASH_TQ = 1024
FLASH_CK = 256

_NT = (((1,), (1,)), ((), ()))


def _params(semantics, vmem_mib):
    return pltpu.CompilerParams(dimension_semantics=semantics,
                                vmem_limit_bytes=vmem_mib * MIB)


def _rms_scale(x, width):
    return lax.rsqrt(jnp.sum(x * x, axis=-1, keepdims=True) * (1.0 / width) + EPS)


def _norm_matmul_kernel(x_ref, g_ref, w_ref, o_ref, xn_ref, *, width):
    @pl.when(pl.program_id(1) == 0)
    def _():
        x = x_ref[...]
        xn_ref[...] = (x * _rms_scale(x, width) * g_ref[...]).astype(jnp.bfloat16)

    o_ref[...] = jnp.dot(xn_ref[...], w_ref[...],
                         preferred_element_type=jnp.float32).astype(o_ref.dtype)


def norm_matmul(x, g, w, *, col_block=0, tm=512, tn=512):
    n = x.shape[0]
    k, m = w.shape
    return pl.pallas_call(
        functools.partial(_norm_matmul_kernel, width=k),
        grid=(n // tm, m // tn),
        in_specs=[pl.BlockSpec((tm, k), lambda i, j: (i, col_block)),
                  pl.BlockSpec((1, k), lambda i, j: (0, 0)),
                  pl.BlockSpec((k, tn), lambda i, j: (0, j))],
        out_specs=pl.BlockSpec((tm, tn), lambda i, j: (i, j)),
        out_shape=jax.ShapeDtypeStruct((n, m), jnp.float32),
        scratch_shapes=[pltpu.VMEM((tm, k), jnp.bfloat16)],
        compiler_params=_params(("parallel", "arbitrary"), 40),
        name="norm_matmul",
    )(x, g.reshape(1, k), w)


def _matmul_res_kernel(*refs, n_a):
    a_refs, w_refs = refs[:n_a], refs[n_a:2 * n_a]
    r_ref, o_ref = refs[2 * n_a], refs[2 * n_a + 1]
    acc = r_ref[...]
    for a_ref, w_ref in zip(a_refs, w_refs):
        acc = acc + jnp.dot(a_ref[...], w_ref[...], preferred_element_type=jnp.float32)
    o_ref[...] = acc


def matmul_residual(a_list, w, res, *, tm=512, tn=1024):
    n, m = res.shape
    n_a = len(a_list)
    ka = a_list[0].shape[1]
    in_specs = [pl.BlockSpec((tm, ka), lambda i, j: (i, 0)) for _ in a_list]
    in_specs += [pl.BlockSpec((ka, tn), functools.partial(lambda i, j, p: (p, j), p=p))
                 for p in range(n_a)]
    in_specs += [pl.BlockSpec((tm, tn), lambda i, j: (i, j))]
    return pl.pallas_call(
        functools.partial(_matmul_res_kernel, n_a=n_a),
        grid=(n // tm, m // tn),
        in_specs=in_specs,
        out_specs=pl.BlockSpec((tm, tn), lambda i, j: (i, j)),
        out_shape=jax.ShapeDtypeStruct((n, m), jnp.float32),
        compiler_params=_params(("parallel", "parallel"), 40),
        name="matmul_residual",
    )(*a_list, *([w] * n_a), res)


def _prep_a_kernel(q_ref, k_ref, v_ref, gq_ref, gk_ref, qo_ref, ko_ref, vo_ref, *, scale):
    for h in range(A_HEADS):
        sl = slice(h * HEAD_DIM, (h + 1) * HEAD_DIM)
        q = q_ref[:, sl]
        qo_ref[:, sl] = (q * _rms_scale(q, HEAD_DIM) * gq_ref[...] * scale).astype(jnp.bfloat16)
        k = k_ref[:, sl]
        ko_ref[:, sl] = (k * _rms_scale(k, HEAD_DIM) * gk_ref[...]).astype(jnp.bfloat16)
    vo_ref[...] = v_ref[...].astype(jnp.bfloat16)


def prep_a(proj, gq, gk, *, tm=512):
    n = proj.shape[0]
    w = A_HEADS * HEAD_DIM
    blk = lambda c: pl.BlockSpec((tm, w), functools.partial(lambda i, c: (i, c), c=c))
    gspec = pl.BlockSpec((1, HEAD_DIM), lambda i: (0, 0))
    out = jax.ShapeDtypeStruct((n, w), jnp.bfloat16)
    return pl.pallas_call(
        functools.partial(_prep_a_kernel, scale=HEAD_DIM ** -0.5),
        grid=(n // tm,),
        in_specs=[blk(0), blk(1), blk(2), gspec, gspec],
        out_specs=[blk(0)] * 3,
        out_shape=[out] * 3,
        compiler_params=_params(("parallel",), 40),
        name="prep_a",
    )(proj, proj, proj, gq.reshape(1, -1), gk.reshape(1, -1))


def _dilated_kernel(*refs, dil, first, last):
    q_ref, kc_ref, kp_ref, vc_ref, vp_ref = refs[:5]
    if first:
        o_ref, lse_ref = refs[5:]
    elif last:
        oprev_ref, lprev_ref, o_ref = refs[5:]
    else:
        oprev_ref, lprev_ref, o_ref, lse_ref = refs[5:]
    band = HEAD_DIM
    n = pl.program_id(2)
    qi = lax.broadcasted_iota(jnp.int32, (band, band), 0)
    ki = lax.broadcasted_iota(jnp.int32, (band, band), 1)
    dist_c = (qi - ki).astype(jnp.float32)
    dist_p = dist_c + float(band)
    valid_c = qi >= ki
    valid_p = jnp.logical_and(ki >= qi, n > 0)
    for h in range(A_HEADS):
        sl = slice(h * HEAD_DIM, (h + 1) * HEAD_DIM)
        slope = 2.0 ** (-8.0 * (h + 1) / A_HEADS) * dil
        q = q_ref[0, :, sl]
        sc = lax.dot_general(q, kc_ref[0, :, sl], _NT, preferred_element_type=jnp.float32)
        sp = lax.dot_general(q, kp_ref[0, :, sl], _NT, preferred_element_type=jnp.float32)
        sc = jnp.where(valid_c, sc - slope * dist_c, NEG)
        sp = jnp.where(valid_p, sp - slope * dist_p, NEG)
        m = jnp.maximum(jnp.max(sc, axis=1, keepdims=True), jnp.max(sp, axis=1, keepdims=True))
        pc = jnp.exp(sc - m)
        pp = jnp.exp(sp - m)
        l = jnp.sum(pc, axis=1, keepdims=True) + jnp.sum(pp, axis=1, keepdims=True)
        acc = jnp.dot(pc.astype(jnp.bfloat16), vc_ref[0, :, sl], preferred_element_type=jnp.float32)
        acc = acc + jnp.dot(pp.astype(jnp.bfloat16), vp_ref[0, :, sl],
                            preferred_element_type=jnp.float32)
        o = acc / l
        lse = jnp.broadcast_to(m + jnp.log(l), (band, HEAD_DIM))
        if not first:
            lprev = lprev_ref[0, :, sl]
            mx = jnp.maximum(lprev, lse)
            w_prev = jnp.exp(lprev - mx)
            w_new = jnp.exp(lse - mx)
            den = w_prev + w_new
            o = (w_prev * oprev_ref[0, :, sl] + w_new * o) / den
            lse = mx + jnp.log(den)
        o_ref[0, :, sl] = o.astype(o_ref.dtype)
        if not last:
            lse_ref[0, :, sl] = lse


def dilated_pattern(q, k, v, state, *, batch, seq, dil, first, last):
    n_tok, w = q.shape
    band = HEAD_DIM
    sub = seq // dil
    view = lambda t: t.reshape(batch, sub, dil * w)
    cur = pl.BlockSpec((1, band, w), lambda b, r, n: (b, n, r))
    prev = pl.BlockSpec((1, band, w), lambda b, r, n: (b, jnp.maximum(n - 1, 0), r))
    ins = [view(q), view(k), view(k), view(v), view(v)]
    in_specs = [cur, cur, prev, cur, prev]
    if not first:
        ins += [view(state[0]), view(state[1])]
        in_specs += [cur, cur]
    f32 = jax.ShapeDtypeStruct((batch, sub, dil * w), jnp.float32)
    if last:
        out_shape = [jax.ShapeDtypeStruct((batch, sub, dil * w), jnp.bfloat16)]
        out_specs = [cur]
    else:
        out_shape = [f32, f32]
        out_specs = [cur, cur]
    outs = pl.pallas_call(
        functools.partial(_dilated_kernel, dil=dil, first=first, last=last),
        grid=(batch, dil, sub // band),
        in_specs=in_specs,
        out_specs=out_specs,
        out_shape=out_shape,
        compiler_params=_params(("parallel", "parallel", "arbitrary"), 40),
        name=f"dilated_d{dil}",
    )(*ins)
    return [o.reshape(n_tok, w) for o in outs]


def dilated_attention(q, k, v, *, batch, seq):
    state = None
    for idx, (window, dil) in enumerate(A_PATTERNS):
        assert window // dil == HEAD_DIM
        state = dilated_pattern(q, k, v, state, batch=batch, seq=seq, dil=dil,
                                first=idx == 0, last=idx == len(A_PATTERNS) - 1)
    return state[0]


def _rope(t, cos, sin_lo, sin_hi):
    return (t * cos + pltpu.roll(t, LANES - ROPE_DIM // 2, 1) * sin_lo
            + pltpu.roll(t, ROPE_DIM // 2, 1) * sin_hi)


def _prep_b_kernel(qb_ref, ckv_ref, kpe_ref, gq_ref, gkn_ref, gkp_ref, cos_ref, slo_ref, shi_ref,
                   qo_ref, ko_ref, vo_ref, *, scale):
    cos, slo, shi = cos_ref[...], slo_ref[...], shi_ref[...]
    kpe = kpe_ref[...]
    kpe_ss = jnp.sum(kpe * kpe, axis=-1, keepdims=True)
    kpe_rot = _rope(kpe * gkp_ref[...], cos, slo, shi)
    for h in range(B_HEADS):
        lo = slice(h * MLA_QK_PAD, h * MLA_QK_PAD + HEAD_DIM)
        hi = slice(h * MLA_QK_PAD + HEAD_DIM, (h + 1) * MLA_QK_PAD)
        q_n, q_p = qb_ref[:, lo], qb_ref[:, hi]
        ss = jnp.sum(q_n * q_n, axis=-1, keepdims=True) + jnp.sum(q_p * q_p, axis=-1, keepdims=True)
        rq = lax.rsqrt(ss * (1.0 / MLA_QK) + EPS) * scale
        qo_ref[h, :, :HEAD_DIM] = (q_n * rq * gq_ref[:, :HEAD_DIM]).astype(jnp.bfloat16)
        qo_ref[h, :, HEAD_DIM:] = _rope(q_p * rq * gq_ref[:, HEAD_DIM:], cos, slo, shi
                                        ).astype(jnp.bfloat16)
        k_n = ckv_ref[:, lo]
        ss = jnp.sum(k_n * k_n, axis=-1, keepdims=True) + kpe_ss
        rk = lax.rsqrt(ss * (1.0 / MLA_QK) + EPS)
        ko_ref[h, :, :HEAD_DIM] = (k_n * rk * gkn_ref[...]).astype(jnp.bfloat16)
        ko_ref[h, :, HEAD_DIM:] = (kpe_rot * rk).astype(jnp.bfloat16)
        vo_ref[h] = ckv_ref[:, hi].T.astype(jnp.bfloat16)


def prep_b(qb, ckv, proj, gq_pad, gk_nope, gk_pe_pad, rope_tabs, *, seq, tm=256):
    n = qb.shape[0]
    wide = B_HEADS * MLA_QK_PAD
    nblk = seq // tm
    row = lambda c: pl.BlockSpec((tm, wide), lambda i: (i, 0))
    tab = pl.BlockSpec((tm, LANES), lambda i: (i % nblk, 0))
    return pl.pallas_call(
        functools.partial(_prep_b_kernel, scale=MLA_QK ** -0.5),
        grid=(n // tm,),
        in_specs=[row(0), row(0),
                  pl.BlockSpec((tm, LANES), lambda i: (i, KPE_COL // LANES)),
                  pl.BlockSpec((1, MLA_QK_PAD), lambda i: (0, 0)),
                  pl.BlockSpec((1, LANES), lambda i: (0, 0)),
                  pl.BlockSpec((1, LANES), lambda i: (0, 0)),
                  tab, tab, tab],
        out_specs=[pl.BlockSpec((B_HEADS, tm, MLA_QK_PAD), lambda i: (0, i, 0)),
                   pl.BlockSpec((B_HEADS, tm, MLA_QK_PAD), lambda i: (0, i, 0)),
                   pl.BlockSpec((B_HEADS, HEAD_DIM, tm), lambda i: (0, 0, i))],
        out_shape=[jax.ShapeDtypeStruct((B_HEADS, n, MLA_QK_PAD), jnp.bfloat16),
                   jax.ShapeDtypeStruct((B_HEADS, n, MLA_QK_PAD), jnp.bfloat16),
                   jax.ShapeDtypeStruct((B_HEADS, HEAD_DIM, n), jnp.bfloat16)],
        compiler_params=_params(("parallel",), 40),
        name="prep_b",
    )(qb, ckv, proj, gq_pad, gk_nope, gk_pe_pad, *rope_tabs)


def rope_tables(seq):
    half = ROPE_DIM // 2
    inv = ROPE_THETA ** (-jnp.arange(0, ROPE_DIM, 2, dtype=jnp.float32) / ROPE_DIM)
    ang = jnp.arange(seq, dtype=jnp.float32)[:, None] * inv[None, :]
    cos, sin = jnp.cos(ang), jnp.sin(ang)
    z = jnp.zeros((seq, half), jnp.float32)
    pad = jnp.zeros((seq, LANES - ROPE_DIM), jnp.float32)
    return (jnp.concatenate([cos, cos, pad], axis=1),
            jnp.concatenate([-sin, z, pad], axis=1),
            jnp.concatenate([z, sin, pad], axis=1))


def _flash_kernel(*refs, tq, ck, moba):
    if moba:
        q_ref, k_ref, vt_ref, sel_ref, slope_ref, o_ref, m_ref, l_ref, acc_ref = refs
    else:
        q_ref, k_ref, vt_ref, o_ref, m_ref, l_ref, acc_ref = refs
    i = pl.program_id(2)
    nb = tq // ck
    n_past = i * nb
    m_ref[...] = jnp.full_like(m_ref, NEG)
    l_ref[...] = jnp.zeros_like(l_ref)
    acc_ref[...] = jnp.zeros_like(acc_ref)
    if moba:
        sel_row = sel_ref[0]
        key_off = lax.broadcasted_iota(jnp.int32, (ck, 1), 0).astype(jnp.float32)
        slope = slope_ref[0][:, :1]

    def scores(j, r0):
        start = pl.multiple_of(j * ck, ck)
        s = lax.dot_general(k_ref[0, pl.ds(start, ck), :], q_ref[0, r0:, :], _NT,
                            preferred_element_type=jnp.float32)
        if moba:
            s = s + slope * (key_off + ((j - n_past) * ck).astype(jnp.float32))
        return s

    def fold(j, r0, s):
        start = pl.multiple_of(j * ck, ck)
        m_prev = m_ref[:, r0:]
        m_new = jnp.maximum(m_prev, jnp.max(s, axis=0, keepdims=True))
        alpha = jnp.exp(m_prev - m_new)
        p = jnp.exp(s - m_new)
        l_ref[:, r0:] = alpha * l_ref[:, r0:] + jnp.sum(p, axis=0, keepdims=True)
        acc_ref[:, r0:] = alpha * acc_ref[:, r0:] + jnp.dot(
            vt_ref[0, :, pl.ds(start, ck)], p.astype(jnp.bfloat16),
            preferred_element_type=jnp.float32)
        m_ref[:, r0:] = m_new

    def selected(j, r0):
        return (jnp.right_shift(sel_row[:, r0:], j) & 1) > 0

    def past(j, s):
        s_next = scores(j + 1, 0)
        if moba:
            s = jnp.where(selected(j, 0), s, NEG)
        fold(j, 0, s)
        return s_next

    s = lax.fori_loop(0, n_past, past, scores(0, 0))
    for jb in range(nb):
        j = n_past + jb
        r0 = jb * ck
        if jb > 0:
            s = scores(j, r0)
        ki = lax.broadcasted_iota(jnp.int32, (ck, tq - r0), 0)
        qi = lax.broadcasted_iota(jnp.int32, (ck, tq - r0), 1)
        ok = qi >= ki
        if moba:
            ok = ok & ((qi < ck) | selected(j, r0))
        fold(j, r0, jnp.where(ok, s, NEG))
    o_ref[...] = (acc_ref[...] / l_ref[...]).T.astype(o_ref.dtype)


def flash_attention(q, k, vt, *, batch, seq, tq, ck, sel=None, slopes=None):
    heads, n, dq = q.shape
    dv = vt.shape[1]
    nq = seq // tq
    moba = sel is not None
    in_specs = [pl.BlockSpec((1, tq, dq), lambda b, h, i: (h, b * nq + i, 0)),
                pl.BlockSpec((1, seq, dq), lambda b, h, i: (h * batch + b, 0, 0)),
                pl.BlockSpec((1, dv, seq), lambda b, h, i: (h, 0, b))]
    ins = [q, k.reshape(heads * batch, seq, dq), vt]
    if moba:
        in_specs += [pl.BlockSpec((1, 1, tq), lambda b, h, i: (h, 0, b * nq + i)),
                     pl.BlockSpec((1, 1, LANES), lambda b, h, i: (h, 0, 0))]
        ins += [sel, slopes]
    return pl.pallas_call(
        functools.partial(_flash_kernel, tq=tq, ck=ck, moba=moba),
        grid=(batch, heads, nq),
        in_specs=in_specs,
        out_specs=pl.BlockSpec((tq, dv), lambda b, h, i: (b * nq + i, h)),
        out_shape=jax.ShapeDtypeStruct((n, heads * dv), jnp.bfloat16),
        scratch_shapes=[pltpu.VMEM((1, tq), jnp.float32),
                        pltpu.VMEM((1, tq), jnp.float32),
                        pltpu.VMEM((dv, tq), jnp.float32)],
        compiler_params=_params(("parallel", "parallel", "arbitrary"), 48),
        name="flash_moba" if moba else "flash_mla",
    )(*ins)


def _prep_c_kernel(q_ref, k_ref, v_ref, gq_ref, gk_ref, qo_ref, ko_ref, vo_ref, sel_ref,
                   kmean_ref, *, blocks_per_seq, scale):
    i = pl.program_id(0)
    bq = i % blocks_per_seq

    @pl.when(bq == 0)
    def _():
        kmean_ref[...] = jnp.zeros_like(kmean_ref)

    rows = q_ref.shape[0]
    blk = lax.broadcasted_iota(jnp.int32, (blocks_per_seq, rows), 0)
    for h in range(C_HEADS):
        sl = slice(h * HEAD_DIM, (h + 1) * HEAD_DIM)
        q = q_ref[:, sl]
        qn = q * _rms_scale(q, HEAD_DIM) * gq_ref[...]
        k = k_ref[:, sl]
        kn = k * _rms_scale(k, HEAD_DIM) * gk_ref[...]
        qo_ref[h] = (qn * scale).astype(jnp.bfloat16)
        ko_ref[h] = kn.astype(jnp.bfloat16)
        vo_ref[h] = v_ref[:, sl].T.astype(jnp.bfloat16)
        gate = lax.dot_general(kmean_ref[h], qn, _NT, precision=lax.Precision.HIGHEST,
                               preferred_element_type=jnp.float32)
        gate = jnp.where(blk < bq, gate, NEG)
        kmean_ref[h, pl.ds(bq, 1), :] = jnp.mean(kn, axis=0, keepdims=True)
        bits = jnp.zeros((1, rows), jnp.int32)
        for r in range(MOBA_TOPK):
            best = jnp.max(gate, axis=0, keepdims=True)
            idx = jnp.min(jnp.where(gate == best, blk, blocks_per_seq), axis=0, keepdims=True)
            bits = bits | jnp.where(r < bq, jnp.left_shift(1, idx), 0)
            gate = jnp.where(blk == idx, 2.0 * NEG, gate)
        sel_ref[h] = bits


def prep_c(qkv, gq, gk, *, seq):
    n = qkv.shape[0]
    tm = MOBA_BLOCK
    w = C_HEADS * HEAD_DIM
    blocks_per_seq = seq // tm
    blk = lambda c: pl.BlockSpec((tm, w), functools.partial(lambda i, c: (i, c), c=c))
    gspec = pl.BlockSpec((1, HEAD_DIM), lambda i: (0, 0))
    hm = pl.BlockSpec((C_HEADS, tm, HEAD_DIM), lambda i: (0, i, 0))
    hm_shape = jax.ShapeDtypeStruct((C_HEADS, n, HEAD_DIM), jnp.bfloat16)
    return pl.pallas_call(
        functools.partial(_prep_c_kernel, blocks_per_seq=blocks_per_seq, scale=HEAD_DIM ** -0.5),
        grid=(n // tm,),
        in_specs=[blk(0), blk(1), blk(2), gspec, gspec],
        out_specs=[hm, hm, pl.BlockSpec((C_HEADS, HEAD_DIM, tm), lambda i: (0, 0, i)),
                   pl.BlockSpec((C_HEADS, 1, tm), lambda i: (0, 0, i))],
        out_shape=[hm_shape, hm_shape,
                   jax.ShapeDtypeStruct((C_HEADS, HEAD_DIM, n), jnp.bfloat16),
                   jax.ShapeDtypeStruct((C_HEADS, 1, n), jnp.int32)],
        scratch_shapes=[pltpu.VMEM((C_HEADS, blocks_per_seq, HEAD_DIM), jnp.float32)],
        compiler_params=_params(("arbitrary",), 40),
        name="prep_c",
    )(qkv, qkv, qkv, gq.reshape(1, -1), gk.reshape(1, -1))


def _router_kernel(h_ref, g_ref, wr_ref, b_ref, t_ref, gate_ref):
    x = h_ref[...]
    t = x * _rms_scale(x, x.shape[1]) * g_ref[...]
    t_ref[...] = t.astype(jnp.bfloat16)
    logits = lax.dot_general(wr_ref[...], t, _NT, precision=lax.Precision.HIGHEST,
                             preferred_element_type=jnp.float32)
    scores = 1.0 / (1.0 + jnp.exp(-logits))
    biased = scores + b_ref[...]
    s = [scores[e:e + 1, :] for e in range(N_EXPERTS)]
    b = [biased[e:e + 1, :] for e in range(N_EXPERTS)]
    best_val, best_grp = None, None
    for g in range(N_GROUPS):
        mem = b[g * GROUP:(g + 1) * GROUP]
        top2 = None
        for x1 in range(GROUP):
            for x2 in range(x1 + 1, GROUP):
                pair = mem[x1] + mem[x2]
                top2 = pair if top2 is None else jnp.maximum(top2, pair)
        if g == 0:
            best_val, best_grp = top2, jnp.zeros_like(top2, dtype=jnp.int32)
        else:
            better = top2 > best_val
            best_grp = jnp.where(better, g, best_grp)
            best_val = jnp.where(better, top2, best_val)
    picked = []
    for e in range(N_EXPERTS):
        g = e // GROUP
        ahead = jnp.zeros_like(best_grp)
        for o in range(g * GROUP, (g + 1) * GROUP):
            if o == e:
                continue
            wins = (b[o] > b[e]) | ((b[o] == b[e]) & (o < e))
            ahead = ahead + wins.astype(jnp.int32)
        picked.append(jnp.where((best_grp == g) & (ahead < TOP_K), s[e], 0.0))
    total = picked[0]
    for e in range(1, N_EXPERTS):
        total = total + picked[e]
    for e in range(N_EXPERTS):
        gate_ref[e:e + 1, :] = picked[e] / total


def router(h, g, w_router_t, bias, *, tm=512):
    n, d = h.shape
    return pl.pallas_call(
        _router_kernel,
        grid=(n // tm,),
        in_specs=[pl.BlockSpec((tm, d), lambda i: (i, 0)),
                  pl.BlockSpec((1, d), lambda i: (0, 0)),
                  pl.BlockSpec((N_EXPERTS, d), lambda i: (0, 0)),
                  pl.BlockSpec((N_EXPERTS, 1), lambda i: (0, 0))],
        out_specs=[pl.BlockSpec((tm, d), lambda i: (i, 0)),
                   pl.BlockSpec((N_EXPERTS, tm), lambda i: (0, i))],
        out_shape=[jax.ShapeDtypeStruct((n, d), jnp.bfloat16),
                   jax.ShapeDtypeStruct((N_EXPERTS, n), jnp.float32)],
        compiler_params=_params(("parallel",), 40),
        name="router",
    )(h, g.reshape(1, d), w_router_t, bias.reshape(N_EXPERTS, 1))


def _moe_kernel(t_ref, gate_ref, wg_ref, wu_ref, wd_ref, h_ref, o_ref):
    e = pl.program_id(1)

    @pl.when(e == 0)
    def _():
        o_ref[...] = h_ref[...]

    t = t_ref[...]
    a = jnp.dot(t, wg_ref[0], preferred_element_type=jnp.float32)
    u = jnp.dot(t, wu_ref[0], preferred_element_type=jnp.float32)
    hid = (a / (1.0 + jnp.exp(-a))) * u
    y = jnp.dot(hid.astype(jnp.bfloat16), wd_ref[0], preferred_element_type=jnp.float32)
    gates = gate_ref[...]
    lane = lax.broadcasted_iota(jnp.int32, gates.shape, 1)
    g_col = jnp.sum(jnp.where(lane == e, gates, 0.0), axis=1, keepdims=True)
    o_ref[...] += g_col * y


def moe_dense(t, gates, wg, wu, wd, h, *, tm=512):
    n, d = h.shape
    ff = wg.shape[2]
    return pl.pallas_call(
        _moe_kernel,
        grid=(n // tm, N_EXPERTS),
        in_specs=[pl.BlockSpec((tm, d), lambda i, e: (i, 0)),
                  pl.BlockSpec((tm, N_EXPERTS), lambda i, e: (i, 0)),
                  pl.BlockSpec((1, d, ff), lambda i, e: (e, 0, 0)),
                  pl.BlockSpec((1, d, ff), lambda i, e: (e, 0, 0)),
                  pl.BlockSpec((1, ff, d), lambda i, e: (e, 0, 0)),
                  pl.BlockSpec((tm, d), lambda i, e: (i, 0))],
        out_specs=pl.BlockSpec((tm, d), lambda i, e: (i, 0)),
        out_shape=jax.ShapeDtypeStruct((n, d), jnp.float32),
        compiler_params=_params(("parallel", "arbitrary"), 48),
        name="moe_dense",
    )(t, gates, wg, wu, wd, h)


def moe_layer(h, ffn_g, w_router_t, router_bias, wg, wu, wd):
    t, gate_t = router(h, ffn_g, w_router_t, router_bias)
    return moe_dense(t, gate_t.T, wg, wu, wd, h)


def _pad_cols(w, width):
    return jnp.pad(w, ((0, 0), (0, width - w.shape[1])))


def _pad_heads(w, heads, real, padded):
    k = w.shape[0]
    w = w.reshape(k, heads, real)
    return jnp.pad(w, ((0, 0), (0, 0), (0, padded - real))).reshape(k, heads * padded)


def kernel(x, attn_norm, ev_w_in, ev_q_a_norm, ev_w_q_b, ev_kv_a_norm, ev_w_kv_b, ev_qn_a, ev_kn_a, ev_qn_b, ev_kn_b, ev_w_o, od_w_qkv, od_qn, od_kn, od_w_o, ffn_norm, w_router, router_bias, w_gate, w_up, w_down):
    batch, seq, d = x.shape
    n = batch * seq
    bf = jnp.bfloat16
    h = x.reshape(n, d)
    w_router_t = w_router.T
    depth = attn_norm.shape[0]
    for layer in range(depth):
        i = layer // 2
        if layer % 2 == 0:
            w_in = _pad_cols(ev_w_in[i], EVEN_IN_PAD).astype(bf)
            proj = norm_matmul(h, attn_norm[layer], w_in)
            qa, ka, va = prep_a(proj, ev_qn_a[i], ev_kn_a[i])
            out_a = dilated_attention(qa, ka, va, batch=batch, seq=seq)
            w_qb = _pad_heads(ev_w_q_b[i], B_HEADS, MLA_QK, MLA_QK_PAD).astype(bf)
            qb = norm_matmul(proj, ev_q_a_norm[i], w_qb, col_block=A_QKV // Q_LORA)
            ckv = norm_matmul(proj, ev_kv_a_norm[i], ev_w_kv_b[i].astype(bf),
                              col_block=(A_QKV + Q_LORA) // KV_LORA)
            gq_pad = jnp.pad(ev_qn_b[i], (0, MLA_QK_PAD - MLA_QK)).reshape(1, -1)
            gk_nope = ev_kn_b[i][:HEAD_DIM].reshape(1, -1)
            gk_pe = jnp.pad(ev_kn_b[i][HEAD_DIM:], (0, LANES - ROPE_DIM)).reshape(1, -1)
            q_b, k_b, v_b = prep_b(qb, ckv, proj, gq_pad, gk_nope, gk_pe, rope_tables(seq), seq=seq)
            out_b = flash_attention(q_b, k_b, v_b, batch=batch, seq=seq,
                                    tq=min(FLASH_TQ, seq), ck=FLASH_CK)
            h = matmul_residual([out_a, out_b], ev_w_o[i].astype(bf), h)
        else:
            qkv = norm_matmul(h, attn_norm[layer], od_w_qkv[i].astype(bf))
            q_c, k_c, v_c, sel = prep_c(qkv, od_qn[i], od_kn[i], seq=seq)
            slopes = 2.0 ** (-8.0 * jnp.arange(1, C_HEADS + 1, dtype=jnp.float32) / C_HEADS)
            slopes = jnp.broadcast_to(slopes[:, None, None], (C_HEADS, 1, LANES))
            out_c = flash_attention(q_c, k_c, v_c, batch=batch, seq=seq, tq=min(FLASH_TQ, seq),
                                    ck=MOBA_BLOCK, sel=sel, slopes=slopes)
            h = matmul_residual([out_c], od_w_o[i].astype(bf), h)
        h = moe_layer(h, ffn_norm[layer], w_router_t, router_bias,
                      w_gate[layer].astype(bf), w_up[layer].astype(bf), w_down[layer].astype(bf))
    return h.reshape(batch, seq, d)
```

```python
import functools
import math

import jax
import jax.numpy as jnp
from jax import lax
from jax.experimental import pallas as pl
from jax.experimental.pallas import tpu as pltpu

EPS = 1e-6
NEG = -1e30
LANES = 128
HEAD_DIM = 128
ROPE_DIM = 64
ROPE_THETA = 10000.0
A_HEADS = 8
A_PATTERNS = ((128, 1), (512, 4), (2048, 16))
B_HEADS = 8
Q_LORA = 512
KV_LORA = 256
MLA_QK = 192
MLA_QK_PAD = 256
C_HEADS = 16
MOBA_BLOCK = 256
MOBA_TOPK = 3
N_EXPERTS = 16
N_GROUPS = 4
GROUP = N_EXPERTS // N_GROUPS
TOP_K = 2
EXPERT_FF = 512
A_QKV = 3 * A_HEADS * HEAD_DIM
EVEN_IN_PAD = 4096
KPE_COL = A_QKV + Q_LORA + KV_LORA
MIB = 1024 * 1024
FLASH_TQ = 1024
FLASH_CK = 256
MOE_TM = 512
MOE_FF_SPLIT = 2

_NT = (((1,), (1,)), ((), ()))


def _params(semantics, vmem_mib):
    return pltpu.CompilerParams(dimension_semantics=semantics,
                                vmem_limit_bytes=vmem_mib * MIB)


def _rms_scale(x, width):
    return lax.rsqrt(jnp.sum(x * x, axis=-1, keepdims=True) * (1.0 / width) + EPS)


def _norm_matmul_kernel(x_ref, g_ref, w_ref, o_ref, xn_ref, *, width):
    @pl.when(pl.program_id(1) == 0)
    def _():
        x = x_ref[...]
        xn_ref[...] = (x * _rms_scale(x, width) * g_ref[...]).astype(jnp.bfloat16)

    o_ref[...] = jnp.dot(xn_ref[...], w_ref[...],
                         preferred_element_type=jnp.float32).astype(o_ref.dtype)


def norm_matmul(x, g, w, *, col_block=0, tm=1024, tn=512):
    n = x.shape[0]
    k, m = w.shape
    return pl.pallas_call(
        functools.partial(_norm_matmul_kernel, width=k),
        grid=(n // tm, m // tn),
        in_specs=[pl.BlockSpec((tm, k), lambda i, j: (i, col_block)),
                  pl.BlockSpec((1, k), lambda i, j: (0, 0)),
                  pl.BlockSpec((k, tn), lambda i, j: (0, j))],
        out_specs=pl.BlockSpec((tm, tn), lambda i, j: (i, j)),
        out_shape=jax.ShapeDtypeStruct((n, m), jnp.float32),
        scratch_shapes=[pltpu.VMEM((tm, k), jnp.bfloat16)],
        compiler_params=_params(("parallel", "arbitrary"), 40),
        name="norm_matmul",
    )(x, g.reshape(1, k), w)


def _matmul_res_kernel(*refs, n_a):
    a_refs, w_refs = refs[:n_a], refs[n_a:2 * n_a]
    r_ref, o_ref = refs[2 * n_a], refs[2 * n_a + 1]
    acc = r_ref[...]
    for a_ref, w_ref in zip(a_refs, w_refs):
        acc = acc + jnp.dot(a_ref[...], w_ref[...], preferred_element_type=jnp.float32)
    o_ref[...] = acc


def matmul_residual(a_list, w, res, *, tm=512, tn=1024):
    n, m = res.shape
    n_a = len(a_list)
    ka = a_list[0].shape[1]
    in_specs = [pl.BlockSpec((tm, ka), lambda i, j: (i, 0)) for _ in a_list]
    in_specs += [pl.BlockSpec((ka, tn), functools.partial(lambda i, j, p: (p, j), p=p))
                 for p in range(n_a)]
    in_specs += [pl.BlockSpec((tm, tn), lambda i, j: (i, j))]
    return pl.pallas_call(
        functools.partial(_matmul_res_kernel, n_a=n_a),
        grid=(n // tm, m // tn),
        in_specs=in_specs,
        out_specs=pl.BlockSpec((tm, tn), lambda i, j: (i, j)),
        out_shape=jax.ShapeDtypeStruct((n, m), jnp.float32),
        compiler_params=_params(("parallel", "parallel"), 40),
        name="matmul_residual",
    )(*a_list, *([w] * n_a), res)


def _prep_a_kernel(q_ref, k_ref, v_ref, gq_ref, gk_ref, qo_ref, ko_ref, vo_ref, *, scale):
    for h in range(A_HEADS):
        sl = slice(h * HEAD_DIM, (h + 1) * HEAD_DIM)
        q = q_ref[:, sl]
        qo_ref[:, sl] = (q * _rms_scale(q, HEAD_DIM) * gq_ref[...] * scale).astype(jnp.bfloat16)
        k = k_ref[:, sl]
        ko_ref[:, sl] = (k * _rms_scale(k, HEAD_DIM) * gk_ref[...]).astype(jnp.bfloat16)
    vo_ref[...] = v_ref[...].astype(jnp.bfloat16)


def prep_a(proj, gq, gk, *, tm=512):
    n = proj.shape[0]
    w = A_HEADS * HEAD_DIM
    blk = lambda c: pl.BlockSpec((tm, w), functools.partial(lambda i, c: (i, c), c=c))
    gspec = pl.BlockSpec((1, HEAD_DIM), lambda i: (0, 0))
    out = jax.ShapeDtypeStruct((n, w), jnp.bfloat16)
    return pl.pallas_call(
        functools.partial(_prep_a_kernel, scale=HEAD_DIM ** -0.5),
        grid=(n // tm,),
        in_specs=[blk(0), blk(1), blk(2), gspec, gspec],
        out_specs=[blk(0)] * 3,
        out_shape=[out] * 3,
        compiler_params=_params(("parallel",), 40),
        name="prep_a",
    )(proj, proj, proj, gq.reshape(1, -1), gk.reshape(1, -1))


def _dilated_kernel(*refs, dil, first, last):
    q_ref, kc_ref, kp_ref, vc_ref, vp_ref = refs[:5]
    if first:
        o_ref, lse_ref = refs[5:]
    elif last:
        oprev_ref, lprev_ref, o_ref = refs[5:]
    else:
        oprev_ref, lprev_ref, o_ref, lse_ref = refs[5:]
    band = HEAD_DIM
    n = pl.program_id(2)
    qi = lax.broadcasted_iota(jnp.int32, (band, band), 0)
    ki = lax.broadcasted_iota(jnp.int32, (band, band), 1)
    dist_c = (qi - ki).astype(jnp.float32)
    dist_p = dist_c + float(band)
    valid_c = qi >= ki
    valid_p = jnp.logical_and(ki >= qi, n > 0)
    for h in range(A_HEADS):
        sl = slice(h * HEAD_DIM, (h + 1) * HEAD_DIM)
        slope = 2.0 ** (-8.0 * (h + 1) / A_HEADS) * dil
        q = q_ref[0, :, sl]
        sc = lax.dot_general(q, kc_ref[0, :, sl], _NT, preferred_element_type=jnp.float32)
        sp = lax.dot_general(q, kp_ref[0, :, sl], _NT, preferred_element_type=jnp.float32)
        sc = jnp.where(valid_c, sc - slope * dist_c, NEG)
        sp = jnp.where(valid_p, sp - slope * dist_p, NEG)
        m = jnp.maximum(jnp.max(sc, axis=1, keepdims=True), jnp.max(sp, axis=1, keepdims=True))
        pc = jnp.exp(sc - m)
        pp = jnp.exp(sp - m)
        l = jnp.sum(pc, axis=1, keepdims=True) + jnp.sum(pp, axis=1, keepdims=True)
        acc = jnp.dot(pc.astype(jnp.bfloat16), vc_ref[0, :, sl], preferred_element_type=jnp.float32)
        acc = acc + jnp.dot(pp.astype(jnp.bfloat16), vp_ref[0, :, sl],
                            preferred_element_type=jnp.float32)
        o = acc / l
        lse = jnp.broadcast_to(m + jnp.log(l), (band, HEAD_DIM))
        if not first:
            lprev = lprev_ref[0, :, sl]
            mx = jnp.maximum(lprev, lse)
            w_prev = jnp.exp(lprev - mx)
            w_new = jnp.exp(lse - mx)
            den = w_prev + w_new
            o = (w_prev * oprev_ref[0, :, sl] + w_new * o) / den
            lse = mx + jnp.log(den)
        o_ref[0, :, sl] = o.astype(o_ref.dtype)
        if not last:
            lse_ref[0, :, sl] = lse


def dilated_pattern(q, k, v, state, *, batch, seq, dil, first, last):
    n_tok, w = q.shape
    band = HEAD_DIM
    sub = seq // dil
    view = lambda t: t.reshape(batch, sub, dil * w)
    cur = pl.BlockSpec((1, band, w), lambda b, r, n: (b, n, r))
    prev = pl.BlockSpec((1, band, w), lambda b, r, n: (b, jnp.maximum(n - 1, 0), r))
    ins = [view(q), view(k), view(k), view(v), view(v)]
    in_specs = [cur, cur, prev, cur, prev]
    if not first:
        ins += [view(state[0]), view(state[1])]
        in_specs += [cur, cur]
    f32 = jax.ShapeDtypeStruct((batch, sub, dil * w), jnp.float32)
    if last:
        out_shape = [jax.ShapeDtypeStruct((batch, sub, dil * w), jnp.bfloat16)]
        out_specs = [cur]
    else:
        out_shape = [f32, f32]
        out_specs = [cur, cur]
    outs = pl.pallas_call(
        functools.partial(_dilated_kernel, dil=dil, first=first, last=last),
        grid=(batch, dil, sub // band),
        in_specs=in_specs,
        out_specs=out_specs,
        out_shape=out_shape,
        compiler_params=_params(("parallel", "parallel", "arbitrary"), 40),
        name=f"dilated_d{dil}",
    )(*ins)
    return [o.reshape(n_tok, w) for o in outs]


def dilated_attention(q, k, v, *, batch, seq):
    state = None
    for idx, (window, dil) in enumerate(A_PATTERNS):
        assert window // dil == HEAD_DIM
        state = dilated_pattern(q, k, v, state, batch=batch, seq=seq, dil=dil,
                                first=idx == 0, last=idx == len(A_PATTERNS) - 1)
    return state[0]


def _rope(t, cos, sin_lo, sin_hi):
    return (t * cos + pltpu.roll(t, LANES - ROPE_DIM // 2, 1) * sin_lo
            + pltpu.roll(t, ROPE_DIM // 2, 1) * sin_hi)


def _prep_b_kernel(qb_ref, ckv_ref, kpe_ref, gq_ref, gkn_ref, gkp_ref, cos_ref, slo_ref, shi_ref,
                   qo_ref, ko_ref, vo_ref, *, scale):
    cos, slo, shi = cos_ref[...], slo_ref[...], shi_ref[...]
    kpe = kpe_ref[...]
    kpe_ss = jnp.sum(kpe * kpe, axis=-1, keepdims=True)
    kpe_rot = _rope(kpe * gkp_ref[...], cos, slo, shi)
    for h in range(B_HEADS):
        lo = slice(h * MLA_QK_PAD, h * MLA_QK_PAD + HEAD_DIM)
        hi = slice(h * MLA_QK_PAD + HEAD_DIM, (h + 1) * MLA_QK_PAD)
        q_n, q_p = qb_ref[:, lo], qb_ref[:, hi]
        ss = jnp.sum(q_n * q_n, axis=-1, keepdims=True) + jnp.sum(q_p * q_p, axis=-1, keepdims=True)
        rq = lax.rsqrt(ss * (1.0 / MLA_QK) + EPS) * scale
        qo_ref[h, :, :HEAD_DIM] = (q_n * rq * gq_ref[:, :HEAD_DIM]).astype(jnp.bfloat16)
        qo_ref[h, :, HEAD_DIM:] = _rope(q_p * rq * gq_ref[:, HEAD_DIM:], cos, slo, shi
                                        ).astype(jnp.bfloat16)
        k_n = ckv_ref[:, lo]
        ss = jnp.sum(k_n * k_n, axis=-1, keepdims=True) + kpe_ss
        rk = lax.rsqrt(ss * (1.0 / MLA_QK) + EPS)
        ko_ref[h, :, :HEAD_DIM] = (k_n * rk * gkn_ref[...]).astype(jnp.bfloat16)
        ko_ref[h, :, HEAD_DIM:] = (kpe_rot * rk).astype(jnp.bfloat16)
        vo_ref[h] = ckv_ref[:, hi].T.astype(jnp.bfloat16)


def prep_b(qb, ckv, proj, gq_pad, gk_nope, gk_pe_pad, rope_tabs, *, seq, tm=256):
    n = qb.shape[0]
    wide = B_HEADS * MLA_QK_PAD
    nblk = seq // tm
    row = lambda c: pl.BlockSpec((tm, wide), lambda i: (i, 0))
    tab = pl.BlockSpec((tm, LANES), lambda i: (i % nblk, 0))
    return pl.pallas_call(
        functools.partial(_prep_b_kernel, scale=MLA_QK ** -0.5),
        grid=(n // tm,),
        in_specs=[row(0), row(0),
                  pl.BlockSpec((tm, LANES), lambda i: (i, KPE_COL // LANES)),
                  pl.BlockSpec((1, MLA_QK_PAD), lambda i: (0, 0)),
                  pl.BlockSpec((1, LANES), lambda i: (0, 0)),
                  pl.BlockSpec((1, LANES), lambda i: (0, 0)),
                  tab, tab, tab],
        out_specs=[pl.BlockSpec((B_HEADS, tm, MLA_QK_PAD), lambda i: (0, i, 0)),
                   pl.BlockSpec((B_HEADS, tm, MLA_QK_PAD), lambda i: (0, i, 0)),
                   pl.BlockSpec((B_HEADS, HEAD_DIM, tm), lambda i: (0, 0, i))],
        out_shape=[jax.ShapeDtypeStruct((B_HEADS, n, MLA_QK_PAD), jnp.bfloat16),
                   jax.ShapeDtypeStruct((B_HEADS, n, MLA_QK_PAD), jnp.bfloat16),
                   jax.ShapeDtypeStruct((B_HEADS, HEAD_DIM, n), jnp.bfloat16)],
        compiler_params=_params(("parallel",), 40),
        name="prep_b",
    )(qb, ckv, proj, gq_pad, gk_nope, gk_pe_pad, *rope_tabs)


def rope_tables(seq):
    half = ROPE_DIM // 2
    inv = ROPE_THETA ** (-jnp.arange(0, ROPE_DIM, 2, dtype=jnp.float32) / ROPE_DIM)
    ang = jnp.arange(seq, dtype=jnp.float32)[:, None] * inv[None, :]
    cos, sin = jnp.cos(ang), jnp.sin(ang)
    z = jnp.zeros((seq, half), jnp.float32)
    pad = jnp.zeros((seq, LANES - ROPE_DIM), jnp.float32)
    return (jnp.concatenate([cos, cos, pad], axis=1),
            jnp.concatenate([-sin, z, pad], axis=1),
            jnp.concatenate([z, sin, pad], axis=1))


def _flash_kernel(*refs, tq, ck, moba):
    if moba:
        q_ref, k_ref, vt_ref, sel_ref, slope_ref, o_ref, m_ref, l_ref, acc_ref = refs
    else:
        q_ref, k_ref, vt_ref, o_ref, m_ref, l_ref, acc_ref = refs
    i = pl.program_id(2)
    nb = tq // ck
    n_past = i * nb
    m_ref[...] = jnp.full_like(m_ref, NEG)
    l_ref[...] = jnp.zeros_like(l_ref)
    acc_ref[...] = jnp.zeros_like(acc_ref)
    if moba:
        sel_row = sel_ref[0]
        key_off = lax.broadcasted_iota(jnp.int32, (ck, 1), 0).astype(jnp.float32)
        slope = slope_ref[0][:, :1]

    def scores(j, r0):
        start = pl.multiple_of(j * ck, ck)
        s = lax.dot_general(k_ref[0, pl.ds(start, ck), :], q_ref[0, r0:, :], _NT,
                            preferred_element_type=jnp.float32)
        if moba:
            s = s + slope * (key_off + ((j - n_past) * ck).astype(jnp.float32))
        return s

    def fold(j, r0, s):
        start = pl.multiple_of(j * ck, ck)
        m_prev = m_ref[:, r0:]
        m_new = jnp.maximum(m_prev, jnp.max(s, axis=0, keepdims=True))
        alpha = jnp.exp(m_prev - m_new)
        p = jnp.exp(s - m_new)
        l_ref[:, r0:] = alpha * l_ref[:, r0:] + jnp.sum(p, axis=0, keepdims=True)
        acc_ref[:, r0:] = alpha * acc_ref[:, r0:] + jnp.dot(
            vt_ref[0, :, pl.ds(start, ck)], p.astype(jnp.bfloat16),
            preferred_element_type=jnp.float32)
        m_ref[:, r0:] = m_new

    def selected(j, r0):
        return (jnp.right_shift(sel_row[:, r0:], j) & 1) > 0

    def past(j, s):
        s_next = scores(j + 1, 0)
        if moba:
            s = jnp.where(selected(j, 0), s, NEG)
        fold(j, 0, s)
        return s_next

    s = lax.fori_loop(0, n_past, past, scores(0, 0))
    for jb in range(nb):
        j = n_past + jb
        r0 = jb * ck
        if jb > 0:
            s = scores(j, r0)
        ki = lax.broadcasted_iota(jnp.int32, (ck, tq - r0), 0)
        qi = lax.broadcasted_iota(jnp.int32, (ck, tq - r0), 1)
        ok = qi >= ki
        if moba:
            ok = ok & ((qi < ck) | selected(j, r0))
        fold(j, r0, jnp.where(ok, s, NEG))
    o_ref[...] = (acc_ref[...] / l_ref[...]).T.astype(o_ref.dtype)


def flash_attention(q, k, vt, *, batch, seq, tq, ck, sel=None, slopes=None):
    heads, n, dq = q.shape
    dv = vt.shape[1]
    nq = seq // tq
    moba = sel is not None
    in_specs = [pl.BlockSpec((1, tq, dq), lambda b, h, i: (h, b * nq + i, 0)),
                pl.BlockSpec((1, seq, dq), lambda b, h, i: (h * batch + b, 0, 0)),
                pl.BlockSpec((1, dv, seq), lambda b, h, i: (h, 0, b))]
    ins = [q, k.reshape(heads * batch, seq, dq), vt]
    if moba:
        in_specs += [pl.BlockSpec((1, 1, tq), lambda b, h, i: (h, 0, b * nq + i)),
                     pl.BlockSpec((1, 1, LANES), lambda b, h, i: (h, 0, 0))]
        ins += [sel, slopes]
    return pl.pallas_call(
        functools.partial(_flash_kernel, tq=tq, ck=ck, moba=moba),
        grid=(batch, heads, nq),
        in_specs=in_specs,
        out_specs=pl.BlockSpec((tq, dv), lambda b, h, i: (b * nq + i, h)),
        out_shape=jax.ShapeDtypeStruct((n, heads * dv), jnp.bfloat16),
        scratch_shapes=[pltpu.VMEM((1, tq), jnp.float32),
                        pltpu.VMEM((1, tq), jnp.float32),
                        pltpu.VMEM((dv, tq), jnp.float32)],
        compiler_params=_params(("parallel", "parallel", "arbitrary"), 48),
        name="flash_moba" if moba else "flash_mla",
    )(*ins)


def _prep_c_kernel(q_ref, k_ref, v_ref, gq_ref, gk_ref, qo_ref, ko_ref, vo_ref, sel_ref,
                   kmean_ref, *, blocks_per_seq, scale):
    i = pl.program_id(0)
    bq = i % blocks_per_seq

    @pl.when(bq == 0)
    def _():
        kmean_ref[...] = jnp.zeros_like(kmean_ref)

    rows = q_ref.shape[0]
    blk = lax.broadcasted_iota(jnp.int32, (blocks_per_seq, rows), 0)
    for h in range(C_HEADS):
        sl = slice(h * HEAD_DIM, (h + 1) * HEAD_DIM)
        q = q_ref[:, sl]
        qn = q * _rms_scale(q, HEAD_DIM) * gq_ref[...]
        k = k_ref[:, sl]
        kn = k * _rms_scale(k, HEAD_DIM) * gk_ref[...]
        qo_ref[h] = (qn * scale).astype(jnp.bfloat16)
        ko_ref[h] = kn.astype(jnp.bfloat16)
        vo_ref[h] = v_ref[:, sl].T.astype(jnp.bfloat16)
        gate = lax.dot_general(kmean_ref[h], qn, _NT, precision=lax.Precision.HIGHEST,
                               preferred_element_type=jnp.float32)
        gate = jnp.where(blk < bq, gate, NEG)
        kmean_ref[h, pl.ds(bq, 1), :] = jnp.mean(kn, axis=0, keepdims=True)
        bits = jnp.zeros((1, rows), jnp.int32)
        for r in range(MOBA_TOPK):
            best = jnp.max(gate, axis=0, keepdims=True)
            idx = jnp.min(jnp.where(gate == best, blk, blocks_per_seq), axis=0, keepdims=True)
            bits = bits | jnp.where(r < bq, jnp.left_shift(1, idx), 0)
            gate = jnp.where(blk == idx, 2.0 * NEG, gate)
        sel_ref[h] = bits


def prep_c(qkv, gq, gk, *, seq):
    n = qkv.shape[0]
    tm = MOBA_BLOCK
    w = C_HEADS * HEAD_DIM
    blocks_per_seq = seq // tm
    blk = lambda c: pl.BlockSpec((tm, w), functools.partial(lambda i, c: (i, c), c=c))
    gspec = pl.BlockSpec((1, HEAD_DIM), lambda i: (0, 0))
    hm = pl.BlockSpec((C_HEADS, tm, HEAD_DIM), lambda i: (0, i, 0))
    hm_shape = jax.ShapeDtypeStruct((C_HEADS, n, HEAD_DIM), jnp.bfloat16)
    return pl.pallas_call(
        functools.partial(_prep_c_kernel, blocks_per_seq=blocks_per_seq, scale=HEAD_DIM ** -0.5),
        grid=(n // tm,),
        in_specs=[blk(0), blk(1), blk(2), gspec, gspec],
        out_specs=[hm, hm, pl.BlockSpec((C_HEADS, HEAD_DIM, tm), lambda i: (0, 0, i)),
                   pl.BlockSpec((C_HEADS, 1, tm), lambda i: (0, 0, i))],
        out_shape=[hm_shape, hm_shape,
                   jax.ShapeDtypeStruct((C_HEADS, HEAD_DIM, n), jnp.bfloat16),
                   jax.ShapeDtypeStruct((C_HEADS, 1, n), jnp.int32)],
        scratch_shapes=[pltpu.VMEM((C_HEADS, blocks_per_seq, HEAD_DIM), jnp.float32)],
        compiler_params=_params(("arbitrary",), 40),
        name="prep_c",
    )(qkv, qkv, qkv, gq.reshape(1, -1), gk.reshape(1, -1))


def _router_kernel(h_ref, g_ref, wr_ref, b_ref, gate_ref, grp_ref):
    x = h_ref[...]
    t = x * _rms_scale(x, x.shape[1]) * g_ref[...]
    logits = lax.dot_general(wr_ref[...], t, _NT, precision=lax.Precision.HIGHEST,
                             preferred_element_type=jnp.float32)
    scores = 1.0 / (1.0 + jnp.exp(-logits))
    biased = scores + b_ref[...]
    s = [scores[e:e + 1, :] for e in range(N_EXPERTS)]
    b = [biased[e:e + 1, :] for e in range(N_EXPERTS)]
    best_val, best_grp = None, None
    for g in range(N_GROUPS):
        mem = b[g * GROUP:(g + 1) * GROUP]
        top2 = None
        for x1 in range(GROUP):
            for x2 in range(x1 + 1, GROUP):
                pair = mem[x1] + mem[x2]
                top2 = pair if top2 is None else jnp.maximum(top2, pair)
        if g == 0:
            best_val, best_grp = top2, jnp.zeros_like(top2, dtype=jnp.int32)
        else:
            better = top2 > best_val
            best_grp = jnp.where(better, g, best_grp)
            best_val = jnp.where(better, top2, best_val)
    picked = []
    for e in range(N_EXPERTS):
        g = e // GROUP
        ahead = jnp.zeros_like(best_grp)
        for o in range(g * GROUP, (g + 1) * GROUP):
            if o == e:
                continue
            wins = (b[o] > b[e]) | ((b[o] == b[e]) & (o < e))
            ahead = ahead + wins.astype(jnp.int32)
        picked.append(jnp.where((best_grp == g) & (ahead < TOP_K), s[e], 0.0))
    total = picked[0]
    for e in range(1, N_EXPERTS):
        total = total + picked[e]
    for e in range(N_EXPERTS):
        gate_ref[e:e + 1, :] = picked[e] / total
    grp_ref[...] = best_grp


def router(h, g, w_router_t, bias, *, tm=512):
    n, d = h.shape
    return pl.pallas_call(
        _router_kernel,
        grid=(n // tm,),
        in_specs=[pl.BlockSpec((tm, d), lambda i: (i, 0)),
                  pl.BlockSpec((1, d), lambda i: (0, 0)),
                  pl.BlockSpec((N_EXPERTS, d), lambda i: (0, 0)),
                  pl.BlockSpec((N_EXPERTS, 1), lambda i: (0, 0))],
        out_specs=[pl.BlockSpec((N_EXPERTS, tm), lambda i: (0, i)),
                   pl.BlockSpec((1, tm), lambda i: (0, i))],
        out_shape=[jax.ShapeDtypeStruct((N_EXPERTS, n), jnp.float32),
                   jax.ShapeDtypeStruct((1, n), jnp.int32)],
        compiler_params=_params(("parallel",), 40),
        name="router",
    )(h, g.reshape(1, d), w_router_t, bias.reshape(N_EXPERTS, 1))


def _row_copies(tok_ref, base, rows, hbm_ref, buf_ref, sem, to_hbm):
    def copy(r):
        hbm_row = hbm_ref.at[pl.ds(tok_ref[base + r], 1), :]
        buf_row = buf_ref.at[pl.ds(r, 1), :]
        src, dst = (buf_row, hbm_row) if to_hbm else (hbm_row, buf_row)
        return pltpu.make_async_copy(src, dst, sem)

    def start(r, carry):
        copy(r).start()
        return carry

    def wait(r, carry):
        copy(r).wait()
        return carry

    lax.fori_loop(0, rows, start, 0, unroll=8)
    lax.fori_loop(0, rows, wait, 0, unroll=8)


def _moe_kernel(tile_ref, grp_ref, flag_ref, tok_ref, h_hbm, g_ref, gate_ref, wg_ref, wu_ref, wd_ref,
                out_hbm, hbuf, tbuf, sem, *, tm, halves):
    j = pl.program_id(0)
    sub = pl.program_id(1)
    flag = flag_ref[j]
    base = tile_ref[j] * tm

    @pl.when((sub == 0) & ((flag & 2) != 0))
    def _():
        _row_copies(tok_ref, base, tm, h_hbm, hbuf, sem, to_hbm=False)
        x = hbuf[...]
        tbuf[...] = (x * _rms_scale(x, x.shape[1]) * g_ref[...]).astype(jnp.bfloat16)

    @pl.when((flag & 1) != 0)
    def _():
        e = grp_ref[j] * GROUP + sub // halves
        t = tbuf[...]
        a = jnp.dot(t, wg_ref[0, 0].astype(jnp.bfloat16), preferred_element_type=jnp.float32)
        u = jnp.dot(t, wu_ref[0, 0].astype(jnp.bfloat16), preferred_element_type=jnp.float32)
        hid = (a / (1.0 + jnp.exp(-a))) * u
        y = jnp.dot(hid.astype(jnp.bfloat16), wd_ref[0, 0].astype(jnp.bfloat16),
                    preferred_element_type=jnp.float32)
        gates = gate_ref[...]
        lane = lax.broadcasted_iota(jnp.int32, gates.shape, 1)
        g_col = jnp.sum(jnp.where(lane == e, gates, 0.0), axis=1, keepdims=True)
        hbuf[...] += g_col * y

    @pl.when((sub == pl.num_programs(1) - 1) & ((flag & 4) != 0))
    def _():
        _row_copies(tok_ref, base, tm, out_hbm, hbuf, sem, to_hbm=True)


def _moe_worklist(grp, tm):
    n = grp.shape[0]
    n_tiles = n // tm
    order = jnp.argsort(grp, stable=True).astype(jnp.int32)
    counts = jnp.sum(grp[None, :] == jnp.arange(N_GROUPS, dtype=jnp.int32)[:, None], axis=1)
    ends = jnp.cumsum(counts)
    first_slot = jnp.arange(n_tiles, dtype=jnp.int32) * tm
    g_lo = jnp.sum(ends[None, :] <= first_slot[:, None], axis=1)
    g_hi = jnp.sum(ends[None, :] <= (first_slot + tm - 1)[:, None], axis=1)
    cnt = g_hi - g_lo + 1
    first_item = jnp.cumsum(cnt) - cnt
    total = first_item[-1] + cnt[-1]
    n_items = n_tiles + N_GROUPS - 1
    item = jnp.arange(n_items, dtype=jnp.int32)
    tile = jnp.minimum(jnp.sum((first_item + cnt)[None, :] <= item[:, None], axis=1), n_tiles - 1)
    valid = item < total
    grp_of = jnp.where(valid, g_lo[tile] + item - first_item[tile], g_hi[n_tiles - 1])
    is_first = valid & (item == first_item[tile])
    is_last = valid & (item == first_item[tile] + cnt[tile] - 1)
    flag = valid.astype(jnp.int32) + 2 * is_first.astype(jnp.int32) + 4 * is_last.astype(jnp.int32)
    return order, tile.astype(jnp.int32), grp_of.astype(jnp.int32), flag


def moe_grouped(h, ffn_g, gate_t, grp, wg, wu, wd, *, layer, tm=MOE_TM, halves=MOE_FF_SPLIT):
    n, d = h.shape
    ff = wg.shape[3] // halves
    order, tile, grp_of, flag = _moe_worklist(grp.reshape(n), tm)
    gates_sorted = jnp.take(gate_t.T, order, axis=0)
    expert = lambda j, sub, tile_ref, grp_ref, flag_ref, tok_ref: grp_ref[j] * GROUP + sub // halves
    grid_spec = pltpu.PrefetchScalarGridSpec(
        num_scalar_prefetch=4,
        grid=(tile.shape[0], GROUP * halves),
        in_specs=[pl.BlockSpec(memory_space=pl.ANY),
                  pl.BlockSpec((1, d), lambda j, sub, *_: (0, 0)),
                  pl.BlockSpec((tm, N_EXPERTS), lambda j, sub, tile_ref, *_: (tile_ref[j], 0)),
                  pl.BlockSpec((1, 1, d, ff),
                               lambda j, sub, *p: (layer, expert(j, sub, *p), 0, sub % halves)),
                  pl.BlockSpec((1, 1, d, ff),
                               lambda j, sub, *p: (layer, expert(j, sub, *p), 0, sub % halves)),
                  pl.BlockSpec((1, 1, ff, d),
                               lambda j, sub, *p: (layer, expert(j, sub, *p), sub % halves, 0))],
        out_specs=pl.BlockSpec(memory_space=pl.ANY),
        scratch_shapes=[pltpu.VMEM((tm, d), jnp.float32),
                        pltpu.VMEM((tm, d), jnp.bfloat16),
                        pltpu.SemaphoreType.DMA(())])
    return pl.pallas_call(
        functools.partial(_moe_kernel, tm=tm, halves=halves),
        grid_spec=grid_spec,
        out_shape=jax.ShapeDtypeStruct((n, d), jnp.float32),
        compiler_params=_params(("arbitrary", "arbitrary"), 48),
        name="moe_grouped",
    )(tile, grp_of, flag, order, h, ffn_g.reshape(1, d), gates_sorted, wg, wu, wd)


def moe_layer(h, ffn_g, w_router_t, router_bias, wg, wu, wd, *, layer):
    gate_t, grp = router(h, ffn_g, w_router_t, router_bias)
    return moe_grouped(h, ffn_g, gate_t, grp, wg, wu, wd, layer=layer)


def _pad_cols(w, width):
    return jnp.pad(w, ((0, 0), (0, width - w.shape[1])))


def _pad_heads(w, heads, real, padded):
    k = w.shape[0]
    w = w.reshape(k, heads, real)
    return jnp.pad(w, ((0, 0), (0, 0), (0, padded - real))).reshape(k, heads * padded)


def kernel(x, attn_norm, ev_w_in, ev_q_a_norm, ev_w_q_b, ev_kv_a_norm, ev_w_kv_b, ev_qn_a, ev_kn_a, ev_qn_b, ev_kn_b, ev_w_o, od_w_qkv, od_qn, od_kn, od_w_o, ffn_norm, w_router, router_bias, w_gate, w_up, w_down):
    batch, seq, d = x.shape
    n = batch * seq
    bf = jnp.bfloat16
    h = x.reshape(n, d)
    w_router_t = w_router.T
    depth = attn_norm.shape[0]
    for layer in range(depth):
        i = layer // 2
        if layer % 2 == 0:
            w_in = _pad_cols(ev_w_in[i], EVEN_IN_PAD).astype(bf)
            proj = norm_matmul(h, attn_norm[layer], w_in)
            qa, ka, va = prep_a(proj, ev_qn_a[i], ev_kn_a[i])
            out_a = dilated_attention(qa, ka, va, batch=batch, seq=seq)
            w_qb = _pad_heads(ev_w_q_b[i], B_HEADS, MLA_QK, MLA_QK_PAD).astype(bf)
            qb = norm_matmul(proj, ev_q_a_norm[i], w_qb, col_block=A_QKV // Q_LORA)
            ckv = norm_matmul(proj, ev_kv_a_norm[i], ev_w_kv_b[i].astype(bf),
                              col_block=(A_QKV + Q_LORA) // KV_LORA)
            gq_pad = jnp.pad(ev_qn_b[i], (0, MLA_QK_PAD - MLA_QK)).reshape(1, -1)
            gk_nope = ev_kn_b[i][:HEAD_DIM].reshape(1, -1)
            gk_pe = jnp.pad(ev_kn_b[i][HEAD_DIM:], (0, LANES - ROPE_DIM)).reshape(1, -1)
            q_b, k_b, v_b = prep_b(qb, ckv, proj, gq_pad, gk_nope, gk_pe, rope_tables(seq), seq=seq)
            out_b = flash_attention(q_b, k_b, v_b, batch=batch, seq=seq,
                                    tq=min(FLASH_TQ, seq), ck=FLASH_CK)
            h = matmul_residual([out_a, out_b], ev_w_o[i].astype(bf), h)
        else:
            qkv = norm_matmul(h, attn_norm[layer], od_w_qkv[i].astype(bf))
            q_c, k_c, v_c, sel = prep_c(qkv, od_qn[i], od_kn[i], seq=seq)
            slopes = 2.0 ** (-8.0 * jnp.arange(1, C_HEADS + 1, dtype=jnp.float32) / C_HEADS)
            slopes = jnp.broadcast_to(slopes[:, None, None], (C_HEADS, 1, LANES))
            out_c = flash_attention(q_c, k_c, v_c, batch=batch, seq=seq, tq=min(FLASH_TQ, seq),
                                    ck=MOBA_BLOCK, sel=sel, slopes=slopes)
            h = matmul_residual([out_c], od_w_o[i].astype(bf), h)
        h = moe_layer(h, ffn_norm[layer], w_router_t, router_bias, w_gate, w_up, w_down, layer=layer)
    return h.reshape(batch, seq, d)
```

```python
import functools
import math

import jax
import jax.numpy as jnp
from jax import lax
from jax.experimental import pallas as pl
from jax.experimental.pallas import tpu as pltpu

EPS = 1e-6
NEG = -1e30
LANES = 128
HEAD_DIM = 128
ROPE_DIM = 64
ROPE_THETA = 10000.0
A_HEADS = 8
A_PATTERNS = ((128, 1), (512, 4), (2048, 16))
B_HEADS = 8
Q_LORA = 512
KV_LORA = 256
MLA_QK = 192
MLA_QK_PAD = 256
C_HEADS = 16
MOBA_BLOCK = 256
MOBA_TOPK = 3
N_EXPERTS = 16
N_GROUPS = 4
GROUP = N_EXPERTS // N_GROUPS
TOP_K = 2
EXPERT_FF = 512
A_QKV = 3 * A_HEADS * HEAD_DIM
EVEN_IN_PAD = 4096
KPE_COL = A_QKV + Q_LORA + KV_LORA
MIB = 1024 * 1024
FLASH_TQ = 1024
FLASH_CK = 256
ONES_ROWS = 16
ALIBI_COLS = 6
LOG2E = 1.4426950408889634
MOE_TM = 1024
MOE_FF_SPLIT = 2

_NT = (((1,), (1,)), ((), ()))


def _params(semantics, vmem_mib):
    return pltpu.CompilerParams(dimension_semantics=semantics,
                                vmem_limit_bytes=vmem_mib * MIB)


def _rms_scale(x, width):
    return lax.rsqrt(jnp.sum(x * x, axis=-1, keepdims=True) * (1.0 / width) + EPS)


def _norm_matmul_kernel(x_ref, g_ref, w_ref, o_ref, xn_ref, *, width):
    @pl.when(pl.program_id(1) == 0)
    def _():
        x = x_ref[...]
        xn_ref[...] = (x * _rms_scale(x, width) * g_ref[...]).astype(jnp.bfloat16)

    o_ref[...] = jnp.dot(xn_ref[...], w_ref[...],
                         preferred_element_type=jnp.float32).astype(o_ref.dtype)


def norm_matmul(x, g, w, *, col_block=0, tm=1024, tn=512):
    n = x.shape[0]
    k, m = w.shape
    return pl.pallas_call(
        functools.partial(_norm_matmul_kernel, width=k),
        grid=(n // tm, m // tn),
        in_specs=[pl.BlockSpec((tm, k), lambda i, j: (i, col_block)),
                  pl.BlockSpec((1, k), lambda i, j: (0, 0)),
                  pl.BlockSpec((k, tn), lambda i, j: (0, j))],
        out_specs=pl.BlockSpec((tm, tn), lambda i, j: (i, j)),
        out_shape=jax.ShapeDtypeStruct((n, m), jnp.float32),
        scratch_shapes=[pltpu.VMEM((tm, k), jnp.bfloat16)],
        compiler_params=_params(("parallel", "arbitrary"), 40),
        name="norm_matmul",
    )(x, g.reshape(1, k), w)


def _matmul_res_kernel(*refs, n_a):
    a_refs, w_refs = refs[:n_a], refs[n_a:2 * n_a]
    r_ref, o_ref = refs[2 * n_a], refs[2 * n_a + 1]
    acc = r_ref[...]
    for a_ref, w_ref in zip(a_refs, w_refs):
        acc = acc + jnp.dot(a_ref[...], w_ref[...], preferred_element_type=jnp.float32)
    o_ref[...] = acc


def matmul_residual(a_list, w, res, *, tm=512, tn=1024):
    n, m = res.shape
    n_a = len(a_list)
    ka = a_list[0].shape[1]
    in_specs = [pl.BlockSpec((tm, ka), lambda i, j: (i, 0)) for _ in a_list]
    in_specs += [pl.BlockSpec((ka, tn), functools.partial(lambda i, j, p: (p, j), p=p))
                 for p in range(n_a)]
    in_specs += [pl.BlockSpec((tm, tn), lambda i, j: (i, j))]
    return pl.pallas_call(
        functools.partial(_matmul_res_kernel, n_a=n_a),
        grid=(n // tm, m // tn),
        in_specs=in_specs,
        out_specs=pl.BlockSpec((tm, tn), lambda i, j: (i, j)),
        out_shape=jax.ShapeDtypeStruct((n, m), jnp.float32),
        compiler_params=_params(("parallel", "parallel"), 40),
        name="matmul_residual",
    )(*a_list, *([w] * n_a), res)


def _prep_a_kernel(q_ref, k_ref, v_ref, gq_ref, gk_ref, qo_ref, ko_ref, vo_ref, *, scale):
    for h in range(A_HEADS):
        sl = slice(h * HEAD_DIM, (h + 1) * HEAD_DIM)
        q = q_ref[:, sl]
        qo_ref[:, sl] = (q * _rms_scale(q, HEAD_DIM) * gq_ref[...] * scale).astype(jnp.bfloat16)
        k = k_ref[:, sl]
        ko_ref[:, sl] = (k * _rms_scale(k, HEAD_DIM) * gk_ref[...]).astype(jnp.bfloat16)
    vo_ref[...] = v_ref[...].astype(jnp.bfloat16)


def prep_a(proj, gq, gk, *, tm=512):
    n = proj.shape[0]
    w = A_HEADS * HEAD_DIM
    blk = lambda c: pl.BlockSpec((tm, w), functools.partial(lambda i, c: (i, c), c=c))
    gspec = pl.BlockSpec((1, HEAD_DIM), lambda i: (0, 0))
    out = jax.ShapeDtypeStruct((n, w), jnp.bfloat16)
    return pl.pallas_call(
        functools.partial(_prep_a_kernel, scale=HEAD_DIM ** -0.5),
        grid=(n // tm,),
        in_specs=[blk(0), blk(1), blk(2), gspec, gspec],
        out_specs=[blk(0)] * 3,
        out_shape=[out] * 3,
        compiler_params=_params(("parallel",), 40),
        name="prep_a",
    )(proj, proj, proj, gq.reshape(1, -1), gk.reshape(1, -1))


def _dilated_kernel(*refs, dil, first, last):
    q_ref, kc_ref, kp_ref, vc_ref, vp_ref = refs[:5]
    if first:
        o_ref, lse_ref = refs[5:]
    elif last:
        oprev_ref, lprev_ref, o_ref = refs[5:]
    else:
        oprev_ref, lprev_ref, o_ref, lse_ref = refs[5:]
    band = HEAD_DIM
    n = pl.program_id(2)
    qi = lax.broadcasted_iota(jnp.int32, (band, band), 0)
    ki = lax.broadcasted_iota(jnp.int32, (band, band), 1)
    dist_c = (qi - ki).astype(jnp.float32)
    dist_p = dist_c + float(band)
    valid_c = qi >= ki
    valid_p = jnp.logical_and(ki >= qi, n > 0)
    for h in range(A_HEADS):
        sl = slice(h * HEAD_DIM, (h + 1) * HEAD_DIM)
        slope = 2.0 ** (-8.0 * (h + 1) / A_HEADS) * dil
        q = q_ref[0, :, sl]
        sc = lax.dot_general(q, kc_ref[0, :, sl], _NT, preferred_element_type=jnp.float32)
        sp = lax.dot_general(q, kp_ref[0, :, sl], _NT, preferred_element_type=jnp.float32)
        sc = jnp.where(valid_c, sc - slope * dist_c, NEG)
        sp = jnp.where(valid_p, sp - slope * dist_p, NEG)
        m = jnp.maximum(jnp.max(sc, axis=1, keepdims=True), jnp.max(sp, axis=1, keepdims=True))
        pc = jnp.exp(sc - m)
        pp = jnp.exp(sp - m)
        l = jnp.sum(pc, axis=1, keepdims=True) + jnp.sum(pp, axis=1, keepdims=True)
        acc = jnp.dot(pc.astype(jnp.bfloat16), vc_ref[0, :, sl], preferred_element_type=jnp.float32)
        acc = acc + jnp.dot(pp.astype(jnp.bfloat16), vp_ref[0, :, sl],
                            preferred_element_type=jnp.float32)
        o = acc / l
        lse = jnp.broadcast_to(m + jnp.log(l), (band, HEAD_DIM))
        if not first:
            lprev = lprev_ref[0, :, sl]
            mx = jnp.maximum(lprev, lse)
            w_prev = jnp.exp(lprev - mx)
            w_new = jnp.exp(lse - mx)
            den = w_prev + w_new
            o = (w_prev * oprev_ref[0, :, sl] + w_new * o) / den
            lse = mx + jnp.log(den)
        o_ref[0, :, sl] = o.astype(o_ref.dtype)
        if not last:
            lse_ref[0, :, sl] = lse


def dilated_pattern(q, k, v, state, *, batch, seq, dil, first, last):
    n_tok, w = q.shape
    band = HEAD_DIM
    sub = seq // dil
    view = lambda t: t.reshape(batch, sub, dil * w)
    cur = pl.BlockSpec((1, band, w), lambda b, r, n: (b, n, r))
    prev = pl.BlockSpec((1, band, w), lambda b, r, n: (b, jnp.maximum(n - 1, 0), r))
    ins = [view(q), view(k), view(k), view(v), view(v)]
    in_specs = [cur, cur, prev, cur, prev]
    if not first:
        ins += [view(state[0]), view(state[1])]
        in_specs += [cur, cur]
    f32 = jax.ShapeDtypeStruct((batch, sub, dil * w), jnp.float32)
    if last:
        out_shape = [jax.ShapeDtypeStruct((batch, sub, dil * w), jnp.bfloat16)]
        out_specs = [cur]
    else:
        out_shape = [f32, f32]
        out_specs = [cur, cur]
    outs = pl.pallas_call(
        functools.partial(_dilated_kernel, dil=dil, first=first, last=last),
        grid=(batch, dil, sub // band),
        in_specs=in_specs,
        out_specs=out_specs,
        out_shape=out_shape,
        compiler_params=_params(("parallel", "parallel", "arbitrary"), 40),
        name=f"dilated_d{dil}",
    )(*ins)
    return [o.reshape(n_tok, w) for o in outs]


def dilated_attention(q, k, v, *, batch, seq):
    state = None
    for idx, (window, dil) in enumerate(A_PATTERNS):
        assert window // dil == HEAD_DIM
        state = dilated_pattern(q, k, v, state, batch=batch, seq=seq, dil=dil,
                                first=idx == 0, last=idx == len(A_PATTERNS) - 1)
    return state[0]


def _rope(t, cos, sin_lo, sin_hi):
    return (t * cos + pltpu.roll(t, LANES - ROPE_DIM // 2, 1) * sin_lo
            + pltpu.roll(t, ROPE_DIM // 2, 1) * sin_hi)


def _prep_b_kernel(qb_ref, ckv_ref, kpe_ref, gq_ref, gkn_ref, gkp_ref, cos_ref, slo_ref, shi_ref,
                   qo_ref, ko_ref, vo_ref, *, scale):
    cos, slo, shi = cos_ref[...], slo_ref[...], shi_ref[...]
    kpe = kpe_ref[...]
    kpe_ss = jnp.sum(kpe * kpe, axis=-1, keepdims=True)
    kpe_rot = _rope(kpe * gkp_ref[...], cos, slo, shi)
    for h in range(B_HEADS):
        lo = slice(h * MLA_QK_PAD, h * MLA_QK_PAD + HEAD_DIM)
        hi = slice(h * MLA_QK_PAD + HEAD_DIM, (h + 1) * MLA_QK_PAD)
        q_n, q_p = qb_ref[:, lo], qb_ref[:, hi]
        ss = jnp.sum(q_n * q_n, axis=-1, keepdims=True) + jnp.sum(q_p * q_p, axis=-1, keepdims=True)
        rq = lax.rsqrt(ss * (1.0 / MLA_QK) + EPS) * scale
        qo_ref[h, :, :HEAD_DIM] = (q_n * rq * gq_ref[:, :HEAD_DIM]).astype(jnp.bfloat16)
        qo_ref[h, :, HEAD_DIM:] = _rope(q_p * rq * gq_ref[:, HEAD_DIM:], cos, slo, shi
                                        ).astype(jnp.bfloat16)
        k_n = ckv_ref[:, lo]
        ss = jnp.sum(k_n * k_n, axis=-1, keepdims=True) + kpe_ss
        rk = lax.rsqrt(ss * (1.0 / MLA_QK) + EPS)
        ko_ref[h, :, :HEAD_DIM] = (k_n * rk * gkn_ref[...]).astype(jnp.bfloat16)
        ko_ref[h, :, HEAD_DIM:] = (kpe_rot * rk).astype(jnp.bfloat16)
        vo_ref[h, :HEAD_DIM, :] = ckv_ref[:, hi].T.astype(jnp.bfloat16)
        vo_ref[h, HEAD_DIM:, :] = jnp.ones((ONES_ROWS, ckv_ref.shape[0]), jnp.bfloat16)


def prep_b(qb, ckv, proj, gq_pad, gk_nope, gk_pe_pad, rope_tabs, *, seq, tm=256):
    n = qb.shape[0]
    wide = B_HEADS * MLA_QK_PAD
    nblk = seq // tm
    row = lambda c: pl.BlockSpec((tm, wide), lambda i: (i, 0))
    tab = pl.BlockSpec((tm, LANES), lambda i: (i % nblk, 0))
    return pl.pallas_call(
        functools.partial(_prep_b_kernel, scale=MLA_QK ** -0.5 * LOG2E),
        grid=(n // tm,),
        in_specs=[row(0), row(0),
                  pl.BlockSpec((tm, LANES), lambda i: (i, KPE_COL // LANES)),
                  pl.BlockSpec((1, MLA_QK_PAD), lambda i: (0, 0)),
                  pl.BlockSpec((1, LANES), lambda i: (0, 0)),
                  pl.BlockSpec((1, LANES), lambda i: (0, 0)),
                  tab, tab, tab],
        out_specs=[pl.BlockSpec((B_HEADS, tm, MLA_QK_PAD), lambda i: (0, i, 0)),
                   pl.BlockSpec((B_HEADS, tm, MLA_QK_PAD), lambda i: (0, i, 0)),
                   pl.BlockSpec((B_HEADS, HEAD_DIM + ONES_ROWS, tm), lambda i: (0, 0, i))],
        out_shape=[jax.ShapeDtypeStruct((B_HEADS, n, MLA_QK_PAD), jnp.bfloat16),
                   jax.ShapeDtypeStruct((B_HEADS, n, MLA_QK_PAD), jnp.bfloat16),
                   jax.ShapeDtypeStruct((B_HEADS, HEAD_DIM + ONES_ROWS, n), jnp.bfloat16)],
        compiler_params=_params(("parallel",), 40),
        name="prep_b",
    )(qb, ckv, proj, gq_pad, gk_nope, gk_pe_pad, *rope_tabs)


def rope_tables(seq):
    half = ROPE_DIM // 2
    inv = ROPE_THETA ** (-jnp.arange(0, ROPE_DIM, 2, dtype=jnp.float32) / ROPE_DIM)
    ang = jnp.arange(seq, dtype=jnp.float32)[:, None] * inv[None, :]
    cos, sin = jnp.cos(ang), jnp.sin(ang)
    z = jnp.zeros((seq, half), jnp.float32)
    pad = jnp.zeros((seq, LANES - ROPE_DIM), jnp.float32)
    return (jnp.concatenate([cos, cos, pad], axis=1),
            jnp.concatenate([-sin, z, pad], axis=1),
            jnp.concatenate([z, sin, pad], axis=1))


def _flash_kernel(q_ref, k_ref, vt_ref, o_ref, m_ref, acc_ref, *, tq, ck, dv):
    i = pl.program_id(2)
    nb = tq // ck
    n_past = i * nb
    m_ref[...] = jnp.full_like(m_ref, NEG)
    acc_ref[...] = jnp.zeros_like(acc_ref)

    def scores(j, r0):
        start = pl.multiple_of(j * ck, ck)
        return lax.dot_general(k_ref[0, pl.ds(start, ck), :], q_ref[0, r0:, :], _NT,
                               preferred_element_type=jnp.float32)

    def fold(j, r0, s):
        start = pl.multiple_of(j * ck, ck)
        m_prev = m_ref[:, r0:]
        m_new = jnp.maximum(m_prev, jnp.max(s, axis=0, keepdims=True))
        alpha = jnp.exp2(m_prev - m_new)
        p = jnp.exp2(s - m_new)
        acc_ref[:, r0:] = alpha * acc_ref[:, r0:] + jnp.dot(
            vt_ref[0, :, pl.ds(start, ck)], p.astype(jnp.bfloat16),
            preferred_element_type=jnp.float32)
        m_ref[:, r0:] = m_new

    def past(j, s):
        s_next = scores(j + 1, 0)
        fold(j, 0, s)
        return s_next

    s = lax.fori_loop(0, n_past, past, scores(0, 0))
    for jb in range(nb):
        j = n_past + jb
        r0 = jb * ck
        if jb > 0:
            s = scores(j, r0)
        ki = lax.broadcasted_iota(jnp.int32, (ck, tq - r0), 0)
        qi = lax.broadcasted_iota(jnp.int32, (ck, tq - r0), 1)
        fold(j, r0, jnp.where(qi >= ki, s, NEG))
    acc = acc_ref[...]
    o_ref[...] = (acc[:dv] / acc[dv:dv + 1]).T.astype(o_ref.dtype)


def flash_attention(q, k, vt, *, batch, seq, tq, ck):
    heads, n, dq = q.shape
    dvp = vt.shape[1]
    dv = dvp - ONES_ROWS
    nq = seq // tq
    return pl.pallas_call(
        functools.partial(_flash_kernel, tq=tq, ck=ck, dv=dv),
        grid=(batch, heads, nq),
        in_specs=[pl.BlockSpec((1, tq, dq), lambda b, h, i: (h, b * nq + i, 0)),
                  pl.BlockSpec((1, seq, dq), lambda b, h, i: (h * batch + b, 0, 0)),
                  pl.BlockSpec((1, dvp, seq), lambda b, h, i: (h, 0, b))],
        out_specs=pl.BlockSpec((tq, dv), lambda b, h, i: (b * nq + i, h)),
        out_shape=jax.ShapeDtypeStruct((n, heads * dv), jnp.bfloat16),
        scratch_shapes=[pltpu.VMEM((1, tq), jnp.float32),
                        pltpu.VMEM((dvp, tq), jnp.float32)],
        compiler_params=_params(("parallel", "parallel", "arbitrary"), 48),
        name=f"flash_h{heads}",
    )(q, k.reshape(heads * batch, seq, dq), vt)


def _prep_c_kernel(q_ref, k_ref, v_ref, gq_ref, gk_ref, alibi_ref, qo_ref, ko_ref, vo_ref,
                   kmean_ref, *, blocks_per_seq, scale):
    i = pl.program_id(0)
    bq = i % blocks_per_seq

    @pl.when(bq == 0)
    def _():
        kmean_ref[...] = jnp.zeros_like(kmean_ref)

    rows = q_ref.shape[0]
    nblk = blocks_per_seq
    blk = lax.broadcasted_iota(jnp.int32, (nblk, rows), 0)
    lane = lax.broadcasted_iota(jnp.int32, (rows, LANES), 1)
    kpos = bq * rows + lax.broadcasted_iota(jnp.int32, (rows, LANES), 0)
    pos_part = jnp.where((lane - nblk) % 2 == 0, jnp.right_shift(kpos, 7), kpos & (LANES - 1))
    k_extra = jnp.where(lane < nblk, (lane == bq).astype(jnp.float32),
                        jnp.where(lane < nblk + ALIBI_COLS, pos_part.astype(jnp.float32), 0.0))
    k_extra = k_extra.astype(jnp.bfloat16)
    ones = jnp.ones((ONES_ROWS, rows), jnp.bfloat16)
    for h in range(C_HEADS):
        sl = slice(h * HEAD_DIM, (h + 1) * HEAD_DIM)
        q = q_ref[:, sl]
        qn = q * _rms_scale(q, HEAD_DIM) * gq_ref[...]
        k = k_ref[:, sl]
        kn = k * _rms_scale(k, HEAD_DIM) * gk_ref[...]
        qo_ref[h, :, :HEAD_DIM] = (qn * scale).astype(jnp.bfloat16)
        ko_ref[h, :, :HEAD_DIM] = kn.astype(jnp.bfloat16)
        ko_ref[h, :, HEAD_DIM:] = k_extra
        vo_ref[h, :HEAD_DIM, :] = v_ref[:, sl].T.astype(jnp.bfloat16)
        vo_ref[h, HEAD_DIM:, :] = ones
        gate = lax.dot_general(kmean_ref[h], qn, _NT, precision=lax.Precision.HIGHEST,
                               preferred_element_type=jnp.float32)
        gate = jnp.where(blk < bq, gate, NEG)
        kmean_ref[h, pl.ds(bq, 1), :] = jnp.mean(kn, axis=0, keepdims=True)
        allowed = blk == bq
        for r in range(MOBA_TOPK):
            best = jnp.max(gate, axis=0, keepdims=True)
            idx = jnp.min(jnp.where(gate == best, blk, nblk), axis=0, keepdims=True)
            allowed = allowed | ((blk == idx) & (r < bq))
            gate = jnp.where(blk == idx, 2.0 * NEG, gate)
        block_bias = jnp.where(allowed, 0.0, NEG)
        block_bias = jnp.concatenate(
            [block_bias, jnp.zeros((LANES - nblk, rows), jnp.float32)], axis=0)
        qo_ref[h, :, HEAD_DIM:] = (block_bias.T + alibi_ref[h]).astype(jnp.bfloat16)


def prep_c(qkv, gq, gk, alibi_q, *, seq):
    n = qkv.shape[0]
    tm = MOBA_BLOCK
    w = C_HEADS * HEAD_DIM
    blocks_per_seq = seq // tm
    assert blocks_per_seq + ALIBI_COLS <= LANES
    blk = lambda c: pl.BlockSpec((tm, w), functools.partial(lambda i, c: (i, c), c=c))
    gspec = pl.BlockSpec((1, HEAD_DIM), lambda i: (0, 0))
    hm = pl.BlockSpec((C_HEADS, tm, 2 * HEAD_DIM), lambda i: (0, i, 0))
    hm_shape = jax.ShapeDtypeStruct((C_HEADS, n, 2 * HEAD_DIM), jnp.bfloat16)
    dvp = HEAD_DIM + ONES_ROWS
    return pl.pallas_call(
        functools.partial(_prep_c_kernel, blocks_per_seq=blocks_per_seq,
                          scale=HEAD_DIM ** -0.5 * LOG2E),
        grid=(n // tm,),
        in_specs=[blk(0), blk(1), blk(2), gspec, gspec,
                  pl.BlockSpec((C_HEADS, 1, LANES), lambda i: (0, 0, 0))],
        out_specs=[hm, hm, pl.BlockSpec((C_HEADS, dvp, tm), lambda i: (0, 0, i))],
        out_shape=[hm_shape, hm_shape, jax.ShapeDtypeStruct((C_HEADS, dvp, n), jnp.bfloat16)],
        scratch_shapes=[pltpu.VMEM((C_HEADS, blocks_per_seq, HEAD_DIM), jnp.float32)],
        compiler_params=_params(("arbitrary",), 40),
        name="prep_c",
    )(qkv, qkv, qkv, gq.reshape(1, -1), gk.reshape(1, -1), alibi_q)


def alibi_query_columns(blocks_per_seq):
    sigma = 2.0 ** (-8.0 * jnp.arange(1, C_HEADS + 1, dtype=jnp.float32) / C_HEADS) * LOG2E
    pieces = []
    rest = sigma
    for _ in range(ALIBI_COLS // 2):
        piece = rest.astype(jnp.bfloat16).astype(jnp.float32)
        pieces += [piece * float(LANES), piece]
        rest = rest - piece
    cols = jnp.stack(pieces, axis=1)
    cols = jnp.pad(cols, ((0, 0), (blocks_per_seq, LANES - blocks_per_seq - ALIBI_COLS)))
    return cols.reshape(C_HEADS, 1, LANES)


def _router_kernel(h_ref, g_ref, wr_ref, b_ref, gate_ref, grp_ref):
    x = h_ref[...]
    t = x * _rms_scale(x, x.shape[1]) * g_ref[...]
    logits = lax.dot_general(wr_ref[...], t, _NT, precision=lax.Precision.HIGHEST,
                             preferred_element_type=jnp.float32)
    scores = 1.0 / (1.0 + jnp.exp(-logits))
    biased = scores + b_ref[...]
    s = [scores[e:e + 1, :] for e in range(N_EXPERTS)]
    b = [biased[e:e + 1, :] for e in range(N_EXPERTS)]
    best_val, best_grp = None, None
    for g in range(N_GROUPS):
        mem = b[g * GROUP:(g + 1) * GROUP]
        top2 = None
        for x1 in range(GROUP):
            for x2 in range(x1 + 1, GROUP):
                pair = mem[x1] + mem[x2]
                top2 = pair if top2 is None else jnp.maximum(top2, pair)
        if g == 0:
            best_val, best_grp = top2, jnp.zeros_like(top2, dtype=jnp.int32)
        else:
            better = top2 > best_val
            best_grp = jnp.where(better, g, best_grp)
            best_val = jnp.where(better, top2, best_val)
    picked = []
    for e in range(N_EXPERTS):
        g = e // GROUP
        ahead = jnp.zeros_like(best_grp)
        for o in range(g * GROUP, (g + 1) * GROUP):
            if o == e:
                continue
            wins = (b[o] > b[e]) | ((b[o] == b[e]) & (o < e))
            ahead = ahead + wins.astype(jnp.int32)
        picked.append(jnp.where((best_grp == g) & (ahead < TOP_K), s[e], 0.0))
    total = picked[0]
    for e in range(1, N_EXPERTS):
        total = total + picked[e]
    for e in range(N_EXPERTS):
        gate_ref[e:e + 1, :] = picked[e] / total
    grp_ref[...] = best_grp


def router(h, g, w_router_t, bias, *, tm=512):
    n, d = h.shape
    return pl.pallas_call(
        _router_kernel,
        grid=(n // tm,),
        in_specs=[pl.BlockSpec((tm, d), lambda i: (i, 0)),
                  pl.BlockSpec((1, d), lambda i: (0, 0)),
                  pl.BlockSpec((N_EXPERTS, d), lambda i: (0, 0)),
                  pl.BlockSpec((N_EXPERTS, 1), lambda i: (0, 0))],
        out_specs=[pl.BlockSpec((N_EXPERTS, tm), lambda i: (0, i)),
                   pl.BlockSpec((1, tm), lambda i: (0, i))],
        out_shape=[jax.ShapeDtypeStruct((N_EXPERTS, n), jnp.float32),
                   jax.ShapeDtypeStruct((1, n), jnp.int32)],
        compiler_params=_params(("parallel",), 40),
        name="router",
    )(h, g.reshape(1, d), w_router_t, bias.reshape(N_EXPERTS, 1))


def _row_copies(tok_ref, base, rows, hbm_ref, buf_ref, sem, to_hbm):
    def copy(r):
        hbm_row = hbm_ref.at[pl.ds(tok_ref[base + r], 1), :]
        buf_row = buf_ref.at[pl.ds(r, 1), :]
        src, dst = (buf_row, hbm_row) if to_hbm else (hbm_row, buf_row)
        return pltpu.make_async_copy(src, dst, sem)

    def start(r, carry):
        copy(r).start()
        return carry

    def wait(r, carry):
        copy(r).wait()
        return carry

    lax.fori_loop(0, rows, start, 0, unroll=8)
    lax.fori_loop(0, rows, wait, 0, unroll=8)


def _moe_kernel(tile_ref, grp_ref, flag_ref, tok_ref, h_hbm, g_ref, gate_ref, wg_ref, wu_ref, wd_ref,
                out_hbm, hbuf, tbuf, sem, *, tm, halves):
    j = pl.program_id(0)
    sub = pl.program_id(1)
    flag = flag_ref[j]
    base = tile_ref[j] * tm

    @pl.when((sub == 0) & ((flag & 2) != 0))
    def _():
        _row_copies(tok_ref, base, tm, h_hbm, hbuf, sem, to_hbm=False)
        x = hbuf[...]
        tbuf[...] = (x * _rms_scale(x, x.shape[1]) * g_ref[...]).astype(jnp.bfloat16)

    @pl.when((flag & 1) != 0)
    def _():
        e = grp_ref[j] * GROUP + sub // halves
        t = tbuf[...]
        a = jnp.dot(t, wg_ref[0, 0].astype(jnp.bfloat16), preferred_element_type=jnp.float32)
        u = jnp.dot(t, wu_ref[0, 0].astype(jnp.bfloat16), preferred_element_type=jnp.float32)
        hid = (a / (1.0 + jnp.exp(-a))) * u
        y = jnp.dot(hid.astype(jnp.bfloat16), wd_ref[0, 0].astype(jnp.bfloat16),
                    preferred_element_type=jnp.float32)
        gates = gate_ref[...]
        lane = lax.broadcasted_iota(jnp.int32, gates.shape, 1)
        g_col = jnp.sum(jnp.where(lane == e, gates, 0.0), axis=1, keepdims=True)
        hbuf[...] += g_col * y

    @pl.when((sub == pl.num_programs(1) - 1) & ((flag & 4) != 0))
    def _():
        _row_copies(tok_ref, base, tm, out_hbm, hbuf, sem, to_hbm=True)


def _moe_worklist(grp, tm):
    n = grp.shape[0]
    n_tiles = n // tm
    order = jnp.argsort(grp, stable=True).astype(jnp.int32)
    counts = jnp.sum(grp[None, :] == jnp.arange(N_GROUPS, dtype=jnp.int32)[:, None], axis=1)
    ends = jnp.cumsum(counts)
    first_slot = jnp.arange(n_tiles, dtype=jnp.int32) * tm
    g_lo = jnp.sum(ends[None, :] <= first_slot[:, None], axis=1)
    g_hi = jnp.sum(ends[None, :] <= (first_slot + tm - 1)[:, None], axis=1)
    cnt = g_hi - g_lo + 1
    first_item = jnp.cumsum(cnt) - cnt
    total = first_item[-1] + cnt[-1]
    n_items = n_tiles + N_GROUPS - 1
    item = jnp.arange(n_items, dtype=jnp.int32)
    tile = jnp.minimum(jnp.sum((first_item + cnt)[None, :] <= item[:, None], axis=1), n_tiles - 1)
    valid = item < total
    grp_of = jnp.where(valid, g_lo[tile] + item - first_item[tile], g_hi[n_tiles - 1])
    is_first = valid & (item == first_item[tile])
    is_last = valid & (item == first_item[tile] + cnt[tile] - 1)
    flag = valid.astype(jnp.int32) + 2 * is_first.astype(jnp.int32) + 4 * is_last.astype(jnp.int32)
    return order, tile.astype(jnp.int32), grp_of.astype(jnp.int32), flag


def moe_grouped(h, ffn_g, gate_t, grp, wg, wu, wd, *, layer, tm=MOE_TM, halves=MOE_FF_SPLIT):
    n, d = h.shape
    ff = wg.shape[3] // halves
    order, tile, grp_of, flag = _moe_worklist(grp.reshape(n), tm)
    gates_sorted = jnp.take(gate_t.T, order, axis=0)
    expert = lambda j, sub, tile_ref, grp_ref, flag_ref, tok_ref: grp_ref[j] * GROUP + sub // halves
    grid_spec = pltpu.PrefetchScalarGridSpec(
        num_scalar_prefetch=4,
        grid=(tile.shape[0], GROUP * halves),
        in_specs=[pl.BlockSpec(memory_space=pl.ANY),
                  pl.BlockSpec((1, d), lambda j, sub, *_: (0, 0)),
                  pl.BlockSpec((tm, N_EXPERTS), lambda j, sub, tile_ref, *_: (tile_ref[j], 0)),
                  pl.BlockSpec((1, 1, d, ff),
                               lambda j, sub, *p: (layer, expert(j, sub, *p), 0, sub % halves)),
                  pl.BlockSpec((1, 1, d, ff),
                               lambda j, sub, *p: (layer, expert(j, sub, *p), 0, sub % halves)),
                  pl.BlockSpec((1, 1, ff, d),
                               lambda j, sub, *p: (layer, expert(j, sub, *p), sub % halves, 0))],
        out_specs=pl.BlockSpec(memory_space=pl.ANY),
        scratch_shapes=[pltpu.VMEM((tm, d), jnp.float32),
                        pltpu.VMEM((tm, d), jnp.bfloat16),
                        pltpu.SemaphoreType.DMA(())])
    return pl.pallas_call(
        functools.partial(_moe_kernel, tm=tm, halves=halves),
        grid_spec=grid_spec,
        out_shape=jax.ShapeDtypeStruct((n, d), jnp.float32),
        compiler_params=_params(("arbitrary", "arbitrary"), 48),
        name="moe_grouped",
    )(tile, grp_of, flag, order, h, ffn_g.reshape(1, d), gates_sorted, wg, wu, wd)


def moe_layer(h, ffn_g, w_router_t, router_bias, wg, wu, wd, *, layer):
    gate_t, grp = router(h, ffn_g, w_router_t, router_bias)
    return moe_grouped(h, ffn_g, gate_t, grp, wg, wu, wd, layer=layer)


def _pad_cols(w, width):
    return jnp.pad(w, ((0, 0), (0, width - w.shape[1])))


def _pad_heads(w, heads, real, padded):
    k = w.shape[0]
    w = w.reshape(k, heads, real)
    return jnp.pad(w, ((0, 0), (0, 0), (0, padded - real))).reshape(k, heads * padded)


def kernel(x, attn_norm, ev_w_in, ev_q_a_norm, ev_w_q_b, ev_kv_a_norm, ev_w_kv_b, ev_qn_a, ev_kn_a, ev_qn_b, ev_kn_b, ev_w_o, od_w_qkv, od_qn, od_kn, od_w_o, ffn_norm, w_router, router_bias, w_gate, w_up, w_down):
    batch, seq, d = x.shape
    n = batch * seq
    bf = jnp.bfloat16
    h = x.reshape(n, d)
    w_router_t = w_router.T
    depth = attn_norm.shape[0]
    for layer in range(depth):
        i = layer // 2
        if layer % 2 == 0:
            w_in = _pad_cols(ev_w_in[i], EVEN_IN_PAD).astype(bf)
            proj = norm_matmul(h, attn_norm[layer], w_in)
            qa, ka, va = prep_a(proj, ev_qn_a[i], ev_kn_a[i])
            out_a = dilated_attention(qa, ka, va, batch=batch, seq=seq)
            w_qb = _pad_heads(ev_w_q_b[i], B_HEADS, MLA_QK, MLA_QK_PAD).astype(bf)
            qb = norm_matmul(proj, ev_q_a_norm[i], w_qb, col_block=A_QKV // Q_LORA)
            ckv = norm_matmul(proj, ev_kv_a_norm[i], ev_w_kv_b[i].astype(bf),
                              col_block=(A_QKV + Q_LORA) // KV_LORA)
            gq_pad = jnp.pad(ev_qn_b[i], (0, MLA_QK_PAD - MLA_QK)).reshape(1, -1)
            gk_nope = ev_kn_b[i][:HEAD_DIM].reshape(1, -1)
            gk_pe = jnp.pad(ev_kn_b[i][HEAD_DIM:], (0, LANES - ROPE_DIM)).reshape(1, -1)
            q_b, k_b, v_b = prep_b(qb, ckv, proj, gq_pad, gk_nope, gk_pe, rope_tables(seq), seq=seq)
            out_b = flash_attention(q_b, k_b, v_b, batch=batch, seq=seq,
                                    tq=min(FLASH_TQ, seq), ck=FLASH_CK)
            h = matmul_residual([out_a, out_b], ev_w_o[i].astype(bf), h)
        else:
            qkv = norm_matmul(h, attn_norm[layer], od_w_qkv[i].astype(bf))
            q_c, k_c, v_c = prep_c(qkv, od_qn[i], od_kn[i],
                                   alibi_query_columns(seq // MOBA_BLOCK), seq=seq)
            out_c = flash_attention(q_c, k_c, v_c, batch=batch, seq=seq, tq=min(FLASH_TQ, seq),
                                    ck=MOBA_BLOCK)
            h = matmul_residual([out_c], od_w_o[i].astype(bf), h)
        h = moe_layer(h, ffn_norm[layer], w_router_t, router_bias, w_gate, w_up, w_down, layer=layer)
    return h.reshape(batch, seq, d)
```

```python
import functools
import math

import jax
import jax.numpy as jnp
from jax import lax
from jax.experimental import pallas as pl
from jax.experimental.pallas import tpu as pltpu

EPS = 1e-6
NEG = -1e30
LANES = 128
HEAD_DIM = 128
ROPE_DIM = 64
ROPE_THETA = 10000.0
A_HEADS = 8
A_PATTERNS = ((128, 1), (512, 4), (2048, 16))
B_HEADS = 8
Q_LORA = 512
KV_LORA = 256
MLA_QK = 192
MLA_QK_PAD = 256
C_HEADS = 16
MOBA_BLOCK = 256
MOBA_TOPK = 3
N_EXPERTS = 16
N_GROUPS = 4
GROUP = N_EXPERTS // N_GROUPS
TOP_K = 2
EXPERT_FF = 512
A_QKV = 3 * A_HEADS * HEAD_DIM
EVEN_IN_PAD = 4096
KPE_COL = A_QKV + Q_LORA + KV_LORA
MIB = 1024 * 1024
FLASH_TQ = 1024
FLASH_CK = 256
ONES_ROWS = 16
ALIBI_COLS = 6
LOG2E = 1.4426950408889634
MOE_TM = 1024
MOE_FF_SPLIT = 2

_NT = (((1,), (1,)), ((), ()))


def _params(semantics, vmem_mib):
    return pltpu.CompilerParams(dimension_semantics=semantics,
                                vmem_limit_bytes=vmem_mib * MIB)


def _rms_scale(x, width):
    return lax.rsqrt(jnp.sum(x * x, axis=-1, keepdims=True) * (1.0 / width) + EPS)


def _norm_matmul_kernel(x_ref, g_ref, w_ref, o_ref, xn_ref, *, width):
    @pl.when(pl.program_id(1) == 0)
    def _():
        x = x_ref[...]
        xn_ref[...] = (x * _rms_scale(x, width) * g_ref[...]).astype(jnp.bfloat16)

    o_ref[...] = jnp.dot(xn_ref[...], w_ref[...],
                         preferred_element_type=jnp.float32).astype(o_ref.dtype)


def norm_matmul(x, g, w, *, col_block=0, tm=1024, tn=512):
    n = x.shape[0]
    k, m = w.shape
    return pl.pallas_call(
        functools.partial(_norm_matmul_kernel, width=k),
        grid=(n // tm, m // tn),
        in_specs=[pl.BlockSpec((tm, k), lambda i, j: (i, col_block)),
                  pl.BlockSpec((1, k), lambda i, j: (0, 0)),
                  pl.BlockSpec((k, tn), lambda i, j: (0, j))],
        out_specs=pl.BlockSpec((tm, tn), lambda i, j: (i, j)),
        out_shape=jax.ShapeDtypeStruct((n, m), jnp.float32),
        scratch_shapes=[pltpu.VMEM((tm, k), jnp.bfloat16)],
        compiler_params=_params(("parallel", "arbitrary"), 40),
        name="norm_matmul",
    )(x, g.reshape(1, k), w)


def _matmul_res_kernel(*refs, n_a):
    a_refs, w_refs = refs[:n_a], refs[n_a:2 * n_a]
    r_ref, o_ref = refs[2 * n_a], refs[2 * n_a + 1]
    acc = r_ref[...]
    for a_ref, w_ref in zip(a_refs, w_refs):
        acc = acc + jnp.dot(a_ref[...], w_ref[...], preferred_element_type=jnp.float32)
    o_ref[...] = acc


def matmul_residual(a_list, w, res, *, tm=512, tn=1024):
    n, m = res.shape
    n_a = len(a_list)
    ka = a_list[0].shape[1]
    in_specs = [pl.BlockSpec((tm, ka), lambda i, j: (i, 0)) for _ in a_list]
    in_specs += [pl.BlockSpec((ka, tn), functools.partial(lambda i, j, p: (p, j), p=p))
                 for p in range(n_a)]
    in_specs += [pl.BlockSpec((tm, tn), lambda i, j: (i, j))]
    return pl.pallas_call(
        functools.partial(_matmul_res_kernel, n_a=n_a),
        grid=(n // tm, m // tn),
        in_specs=in_specs,
        out_specs=pl.BlockSpec((tm, tn), lambda i, j: (i, j)),
        out_shape=jax.ShapeDtypeStruct((n, m), jnp.float32),
        compiler_params=_params(("parallel", "parallel"), 40),
        name="matmul_residual",
    )(*a_list, *([w] * n_a), res)


def _prep_a_kernel(q_ref, k_ref, v_ref, gq_ref, gk_ref, *refs, scale):
    n_pat = len(A_PATTERNS)
    outs, (qs, ks, vs) = refs[:3 * n_pat], refs[3 * n_pat:]
    heads = [slice(h * HEAD_DIM, (h + 1) * HEAD_DIM) for h in range(A_HEADS)]
    for h, sl in enumerate(heads):
        q = q_ref[:, sl]
        qs[h] = q * _rms_scale(q, HEAD_DIM) * gq_ref[...] * scale
        k = k_ref[:, sl]
        ks[h] = k * _rms_scale(k, HEAD_DIM) * gk_ref[...]
        vs[h] = v_ref[:, sl]
    rows = qs.shape[1]
    for p, (_, dil) in enumerate(A_PATTERNS):
        for src, dst in zip((qs, ks, vs), outs[3 * p:3 * p + 3]):
            for r in range(dil):
                for h, sl in enumerate(heads):
                    dst[0, r, :, sl] = src[h, pl.ds(r, rows // dil, stride=dil), :].astype(
                        jnp.bfloat16)


def prep_a(proj, gq, gk, *, batch, seq, tm=512):
    n = proj.shape[0]
    w = A_HEADS * HEAD_DIM
    per_seq = seq // tm
    blk = lambda c: pl.BlockSpec((tm, w), functools.partial(lambda i, c: (i, c), c=c))
    gspec = pl.BlockSpec((1, HEAD_DIM), lambda i: (0, 0))
    out_specs, out_shape = [], []
    for _, dil in A_PATTERNS:
        spec = pl.BlockSpec((1, dil, tm // dil, w), lambda i: (i // per_seq, 0, i % per_seq, 0))
        shape = jax.ShapeDtypeStruct((batch, dil, seq // dil, w), jnp.bfloat16)
        out_specs += [spec] * 3
        out_shape += [shape] * 3
    outs = pl.pallas_call(
        functools.partial(_prep_a_kernel, scale=HEAD_DIM ** -0.5),
        grid=(n // tm,),
        in_specs=[blk(0), blk(1), blk(2), gspec, gspec],
        out_specs=out_specs,
        out_shape=out_shape,
        scratch_shapes=[pltpu.VMEM((A_HEADS, tm, HEAD_DIM), jnp.float32)] * 3,
        compiler_params=_params(("parallel",), 48),
        name="prep_a",
    )(proj, proj, proj, gq.reshape(1, -1), gk.reshape(1, -1))
    return [outs[3 * p:3 * p + 3] for p in range(len(A_PATTERNS))]


def _dilated_kernel(q_ref, kc_ref, kp_ref, vc_ref, vp_ref, o_ref, lse_ref, *, dil):
    band = HEAD_DIM
    n = pl.program_id(2)
    qi = lax.broadcasted_iota(jnp.int32, (band, band), 0)
    ki = lax.broadcasted_iota(jnp.int32, (band, band), 1)
    dist_c = (qi - ki).astype(jnp.float32)
    dist_p = dist_c + float(band)
    valid_c = qi >= ki
    valid_p = jnp.logical_and(ki >= qi, n > 0)
    lse_tile = jnp.zeros((band, LANES), jnp.float32)
    for h in range(A_HEADS):
        sl = slice(h * HEAD_DIM, (h + 1) * HEAD_DIM)
        slope = 2.0 ** (-8.0 * (h + 1) / A_HEADS) * dil
        q = q_ref[0, 0, :, sl]
        sc = lax.dot_general(q, kc_ref[0, 0, :, sl], _NT, preferred_element_type=jnp.float32)
        sp = lax.dot_general(q, kp_ref[0, 0, :, sl], _NT, preferred_element_type=jnp.float32)
        sc = jnp.where(valid_c, sc - slope * dist_c, NEG)
        sp = jnp.where(valid_p, sp - slope * dist_p, NEG)
        m = jnp.maximum(jnp.max(sc, axis=1, keepdims=True), jnp.max(sp, axis=1, keepdims=True))
        pc = jnp.exp(sc - m)
        pp = jnp.exp(sp - m)
        l = jnp.sum(pc, axis=1, keepdims=True) + jnp.sum(pp, axis=1, keepdims=True)
        acc = jnp.dot(pc.astype(jnp.bfloat16), vc_ref[0, 0, :, sl],
                      preferred_element_type=jnp.float32)
        acc = acc + jnp.dot(pp.astype(jnp.bfloat16), vp_ref[0, 0, :, sl],
                            preferred_element_type=jnp.float32)
        o_ref[0, 0, :, sl] = acc / l
        lse_tile = jnp.where(ki == h, m + jnp.log(l), lse_tile)
    lse_ref[0, 0] = lse_tile


def dilated_pattern(q, k, v, *, dil):
    batch, _, sub, w = q.shape
    band = HEAD_DIM
    cur = pl.BlockSpec((1, 1, band, w), lambda b, r, n: (b, r, n, 0))
    prev = pl.BlockSpec((1, 1, band, w), lambda b, r, n: (b, r, jnp.maximum(n - 1, 0), 0))
    return pl.pallas_call(
        functools.partial(_dilated_kernel, dil=dil),
        grid=(batch, dil, sub // band),
        in_specs=[cur, cur, prev, cur, prev],
        out_specs=[cur, pl.BlockSpec((1, 1, band, LANES), lambda b, r, n: (b, r, n, 0))],
        out_shape=[jax.ShapeDtypeStruct((batch, dil, sub, w), jnp.float32),
                   jax.ShapeDtypeStruct((batch, dil, sub, LANES), jnp.float32)],
        compiler_params=_params(("parallel", "parallel", "arbitrary"), 40),
        name=f"dilated_d{dil}",
    )(q, k, k, v, v)


def _dilated_merge_kernel(*refs):
    n_pat = len(A_PATTERNS)
    ins, out_ref, scratch = refs[:2 * n_pat], refs[2 * n_pat], refs[2 * n_pat + 1:]
    rows = out_ref.shape[0]
    heads = [slice(h * HEAD_DIM, (h + 1) * HEAD_DIM) for h in range(A_HEADS)]
    outs, lses = [], []
    for p, (_, dil) in enumerate(A_PATTERNS):
        o_ref, l_ref = ins[2 * p], ins[2 * p + 1]
        if dil == 1:
            outs.append(lambda h, o_ref=o_ref: o_ref[0, 0, :, heads[h]])
            lses.append(l_ref[0, 0])
            continue
        o_nat, l_nat = scratch[2 * p], scratch[2 * p + 1]
        for r in range(dil):
            for h, sl in enumerate(heads):
                o_nat[h, pl.ds(r, rows // dil, stride=dil), :] = o_ref[0, r, :, sl]
            l_nat[pl.ds(r, rows // dil, stride=dil), :] = l_ref[0, r]
        outs.append(lambda h, o_nat=o_nat: o_nat[h])
        lses.append(l_nat[...])
    for h, sl in enumerate(heads):
        lse_h = [l[:, h:h + 1] for l in lses]
        top = functools.reduce(jnp.maximum, lse_h)
        wts = [jnp.exp(l - top) for l in lse_h]
        num = sum(wt * o(h) for wt, o in zip(wts, outs))
        out_ref[:, sl] = (num / sum(wts)).astype(out_ref.dtype)


def dilated_attention(qkv_by_pattern, *, batch, seq, tm=512):
    w = A_HEADS * HEAD_DIM
    per_seq = seq // tm
    ins, in_specs, scratch = [], [], []
    for (q, k, v), (window, dil) in zip(qkv_by_pattern, A_PATTERNS):
        assert window // dil == HEAD_DIM
        o, lse = dilated_pattern(q, k, v, dil=dil)
        ins += [o, lse]
        in_specs += [pl.BlockSpec((1, dil, tm // dil, w), lambda i: (i // per_seq, 0, i % per_seq, 0)),
                     pl.BlockSpec((1, dil, tm // dil, LANES),
                                  lambda i: (i // per_seq, 0, i % per_seq, 0))]
        scratch += [pltpu.VMEM((A_HEADS, tm, HEAD_DIM), jnp.float32),
                    pltpu.VMEM((tm, LANES), jnp.float32)]
    n = batch * seq
    return pl.pallas_call(
        _dilated_merge_kernel,
        grid=(n // tm,),
        in_specs=in_specs,
        out_specs=pl.BlockSpec((tm, w), lambda i: (i, 0)),
        out_shape=jax.ShapeDtypeStruct((n, w), jnp.bfloat16),
        scratch_shapes=scratch,
        compiler_params=_params(("parallel",), 48),
        name="dilated_merge",
    )(*ins)


def _rope(t, cos, sin_lo, sin_hi):
    return (t * cos + pltpu.roll(t, LANES - ROPE_DIM // 2, 1) * sin_lo
            + pltpu.roll(t, ROPE_DIM // 2, 1) * sin_hi)


def _prep_b_kernel(qb_ref, ckv_ref, kpe_ref, gq_ref, gkn_ref, gkp_ref, cos_ref, slo_ref, shi_ref,
                   qo_ref, ko_ref, vo_ref, *, scale):
    cos, slo, shi = cos_ref[...], slo_ref[...], shi_ref[...]
    kpe = kpe_ref[...]
    kpe_ss = jnp.sum(kpe * kpe, axis=-1, keepdims=True)
    kpe_rot = _rope(kpe * gkp_ref[...], cos, slo, shi)
    for h in range(B_HEADS):
        lo = slice(h * MLA_QK_PAD, h * MLA_QK_PAD + HEAD_DIM)
        hi = slice(h * MLA_QK_PAD + HEAD_DIM, (h + 1) * MLA_QK_PAD)
        q_n, q_p = qb_ref[:, lo], qb_ref[:, hi]
        ss = jnp.sum(q_n * q_n, axis=-1, keepdims=True) + jnp.sum(q_p * q_p, axis=-1, keepdims=True)
        rq = lax.rsqrt(ss * (1.0 / MLA_QK) + EPS) * scale
        qo_ref[h, :, :HEAD_DIM] = (q_n * rq * gq_ref[:, :HEAD_DIM]).astype(jnp.bfloat16)
        qo_ref[h, :, HEAD_DIM:] = _rope(q_p * rq * gq_ref[:, HEAD_DIM:], cos, slo, shi
                                        ).astype(jnp.bfloat16)
        k_n = ckv_ref[:, lo]
        ss = jnp.sum(k_n * k_n, axis=-1, keepdims=True) + kpe_ss
        rk = lax.rsqrt(ss * (1.0 / MLA_QK) + EPS)
        ko_ref[h, :, :HEAD_DIM] = (k_n * rk * gkn_ref[...]).astype(jnp.bfloat16)
        ko_ref[h, :, HEAD_DIM:] = (kpe_rot * rk).astype(jnp.bfloat16)
        vo_ref[h, :HEAD_DIM, :] = ckv_ref[:, hi].T.astype(jnp.bfloat16)
        vo_ref[h, HEAD_DIM:, :] = jnp.ones((ONES_ROWS, ckv_ref.shape[0]), jnp.bfloat16)


def prep_b(qb, ckv, proj, gq_pad, gk_nope, gk_pe_pad, rope_tabs, *, seq, tm=256):
    n = qb.shape[0]
    wide = B_HEADS * MLA_QK_PAD
    nblk = seq // tm
    row = lambda c: pl.BlockSpec((tm, wide), lambda i: (i, 0))
    tab = pl.BlockSpec((tm, LANES), lambda i: (i % nblk, 0))
    return pl.pallas_call(
        functools.partial(_prep_b_kernel, scale=MLA_QK ** -0.5 * LOG2E),
        grid=(n // tm,),
        in_specs=[row(0), row(0),
                  pl.BlockSpec((tm, LANES), lambda i: (i, KPE_COL // LANES)),
                  pl.BlockSpec((1, MLA_QK_PAD), lambda i: (0, 0)),
                  pl.BlockSpec((1, LANES), lambda i: (0, 0)),
                  pl.BlockSpec((1, LANES), lambda i: (0, 0)),
                  tab, tab, tab],
        out_specs=[pl.BlockSpec((B_HEADS, tm, MLA_QK_PAD), lambda i: (0, i, 0)),
                   pl.BlockSpec((B_HEADS, tm, MLA_QK_PAD), lambda i: (0, i, 0)),
                   pl.BlockSpec((B_HEADS, HEAD_DIM + ONES_ROWS, tm), lambda i: (0, 0, i))],
        out_shape=[jax.ShapeDtypeStruct((B_HEADS, n, MLA_QK_PAD), jnp.bfloat16),
                   jax.ShapeDtypeStruct((B_HEADS, n, MLA_QK_PAD), jnp.bfloat16),
                   jax.ShapeDtypeStruct((B_HEADS, HEAD_DIM + ONES_ROWS, n), jnp.bfloat16)],
        compiler_params=_params(("parallel",), 40),
        name="prep_b",
    )(qb, ckv, proj, gq_pad, gk_nope, gk_pe_pad, *rope_tabs)


def rope_tables(seq):
    half = ROPE_DIM // 2
    inv = ROPE_THETA ** (-jnp.arange(0, ROPE_DIM, 2, dtype=jnp.float32) / ROPE_DIM)
    ang = jnp.arange(seq, dtype=jnp.float32)[:, None] * inv[None, :]
    cos, sin = jnp.cos(ang), jnp.sin(ang)
    z = jnp.zeros((seq, half), jnp.float32)
    pad = jnp.zeros((seq, LANES - ROPE_DIM), jnp.float32)
    return (jnp.concatenate([cos, cos, pad], axis=1),
            jnp.concatenate([-sin, z, pad], axis=1),
            jnp.concatenate([z, sin, pad], axis=1))


def _flash_kernel(q_ref, k_ref, vt_ref, o_ref, m_ref, acc_ref, *, tq, ck, dv):
    i = pl.program_id(2)
    nb = tq // ck
    n_past = i * nb
    m_ref[...] = jnp.full_like(m_ref, NEG)
    acc_ref[...] = jnp.zeros_like(acc_ref)

    def scores(j, r0):
        start = pl.multiple_of(j * ck, ck)
        return lax.dot_general(k_ref[0, pl.ds(start, ck), :], q_ref[0, r0:, :], _NT,
                               preferred_element_type=jnp.float32)

    def fold(j, r0, s):
        start = pl.multiple_of(j * ck, ck)
        probs, alphas = [], []
        for c0 in range(0, tq - r0, LANES):
            cols = slice(r0 + c0, r0 + c0 + LANES)
            s_c = s[:, c0:c0 + LANES]
            m_prev = m_ref[:, cols]
            m_new = jnp.maximum(m_prev, jnp.max(s_c, axis=0, keepdims=True))
            m_ref[:, cols] = m_new
            alphas.append(jnp.exp2(m_prev - m_new))
            probs.append(jnp.exp2(s_c - m_new).astype(jnp.bfloat16))
        p = jnp.concatenate(probs, axis=1)
        alpha = jnp.concatenate(alphas, axis=1)
        acc_ref[:, r0:] = alpha * acc_ref[:, r0:] + jnp.dot(
            vt_ref[0, :, pl.ds(start, ck)], p, preferred_element_type=jnp.float32)

    def past(j, s):
        s_next = scores(j + 1, 0)
        fold(j, 0, s)
        return s_next

    s = lax.fori_loop(0, n_past, past, scores(0, 0))
    for jb in range(nb):
        j = n_past + jb
        r0 = jb * ck
        if jb > 0:
            s = scores(j, r0)
        ki = lax.broadcasted_iota(jnp.int32, (ck, ck), 0)
        qi = lax.broadcasted_iota(jnp.int32, (ck, ck), 1)
        own = jnp.where(qi >= ki, s[:, :ck], NEG)
        fold(j, r0, own if tq - r0 == ck else jnp.concatenate([own, s[:, ck:]], axis=1))
    acc = acc_ref[...]
    o_ref[...] = (acc[:dv] / acc[dv:dv + 1]).T.astype(o_ref.dtype)


def flash_attention(q, k, vt, *, batch, seq, tq, ck):
    heads, n, dq = q.shape
    dvp = vt.shape[1]
    dv = dvp - ONES_ROWS
    nq = seq // tq
    return pl.pallas_call(
        functools.partial(_flash_kernel, tq=tq, ck=ck, dv=dv),
        grid=(batch, heads, nq),
        in_specs=[pl.BlockSpec((1, tq, dq), lambda b, h, i: (h, b * nq + i, 0)),
                  pl.BlockSpec((1, seq, dq), lambda b, h, i: (h * batch + b, 0, 0)),
                  pl.BlockSpec((1, dvp, seq), lambda b, h, i: (h, 0, b))],
        out_specs=pl.BlockSpec((tq, dv), lambda b, h, i: (b * nq + i, h)),
        out_shape=jax.ShapeDtypeStruct((n, heads * dv), jnp.bfloat16),
        scratch_shapes=[pltpu.VMEM((1, tq), jnp.float32),
                        pltpu.VMEM((dvp, tq), jnp.float32)],
        compiler_params=_params(("parallel", "parallel", "arbitrary"), 48),
        name=f"flash_h{heads}",
    )(q, k.reshape(heads * batch, seq, dq), vt)


def _prep_c_kernel(q_ref, k_ref, v_ref, gq_ref, gk_ref, alibi_ref, qo_ref, ko_ref, vo_ref,
                   kmean_ref, *, blocks_per_seq, scale):
    i = pl.program_id(0)
    bq = i % blocks_per_seq

    @pl.when(bq == 0)
    def _():
        kmean_ref[...] = jnp.zeros_like(kmean_ref)

    rows = q_ref.shape[0]
    nblk = blocks_per_seq
    blk = lax.broadcasted_iota(jnp.int32, (nblk, rows), 0)
    lane = lax.broadcasted_iota(jnp.int32, (rows, LANES), 1)
    kpos = bq * rows + lax.broadcasted_iota(jnp.int32, (rows, LANES), 0)
    pos_part = jnp.where((lane - nblk) % 2 == 0, jnp.right_shift(kpos, 7), kpos & (LANES - 1))
    k_extra = jnp.where(lane < nblk, (lane == bq).astype(jnp.float32),
                        jnp.where(lane < nblk + ALIBI_COLS, pos_part.astype(jnp.float32), 0.0))
    k_extra = k_extra.astype(jnp.bfloat16)
    ones = jnp.ones((ONES_ROWS, rows), jnp.bfloat16)
    for h in range(C_HEADS):
        sl = slice(h * HEAD_DIM, (h + 1) * HEAD_DIM)
        q = q_ref[:, sl]
        qn = q * _rms_scale(q, HEAD_DIM) * gq_ref[...]
        k = k_ref[:, sl]
        kn = k * _rms_scale(k, HEAD_DIM) * gk_ref[...]
        qo_ref[h, :, :HEAD_DIM] = (qn * scale).astype(jnp.bfloat16)
        ko_ref[h, :, :HEAD_DIM] = kn.astype(jnp.bfloat16)
        ko_ref[h, :, HEAD_DIM:] = k_extra
        vo_ref[h, :HEAD_DIM, :] = v_ref[:, sl].T.astype(jnp.bfloat16)
        vo_ref[h, HEAD_DIM:, :] = ones
        gate = lax.dot_general(kmean_ref[h], qn, _NT, precision=lax.Precision.HIGHEST,
                               preferred_element_type=jnp.float32)
        gate = jnp.where(blk < bq, gate, NEG)
        kmean_ref[h, pl.ds(bq, 1), :] = jnp.mean(kn, axis=0, keepdims=True)
        allowed = blk == bq
        for r in range(MOBA_TOPK):
            best = jnp.max(gate, axis=0, keepdims=True)
            idx = jnp.min(jnp.where(gate == best, blk, nblk), axis=0, keepdims=True)
            allowed = allowed | ((blk == idx) & (r < bq))
            gate = jnp.where(blk == idx, 2.0 * NEG, gate)
        block_bias = jnp.where(allowed, 0.0, NEG)
        block_bias = jnp.concatenate(
            [block_bias, jnp.zeros((LANES - nblk, rows), jnp.float32)], axis=0)
        qo_ref[h, :, HEAD_DIM:] = (block_bias.T + alibi_ref[h]).astype(jnp.bfloat16)


def prep_c(qkv, gq, gk, alibi_q, *, seq):
    n = qkv.shape[0]
    tm = MOBA_BLOCK
    w = C_HEADS * HEAD_DIM
    blocks_per_seq = seq // tm
    assert blocks_per_seq + ALIBI_COLS <= LANES
    blk = lambda c: pl.BlockSpec((tm, w), functools.partial(lambda i, c: (i, c), c=c))
    gspec = pl.BlockSpec((1, HEAD_DIM), lambda i: (0, 0))
    hm = pl.BlockSpec((C_HEADS, tm, 2 * HEAD_DIM), lambda i: (0, i, 0))
    hm_shape = jax.ShapeDtypeStruct((C_HEADS, n, 2 * HEAD_DIM), jnp.bfloat16)
    dvp = HEAD_DIM + ONES_ROWS
    return pl.pallas_call(
        functools.partial(_prep_c_kernel, blocks_per_seq=blocks_per_seq,
                          scale=HEAD_DIM ** -0.5 * LOG2E),
        grid=(n // tm,),
        in_specs=[blk(0), blk(1), blk(2), gspec, gspec,
                  pl.BlockSpec((C_HEADS, 1, LANES), lambda i: (0, 0, 0))],
        out_specs=[hm, hm, pl.BlockSpec((C_HEADS, dvp, tm), lambda i: (0, 0, i))],
        out_shape=[hm_shape, hm_shape, jax.ShapeDtypeStruct((C_HEADS, dvp, n), jnp.bfloat16)],
        scratch_shapes=[pltpu.VMEM((C_HEADS, blocks_per_seq, HEAD_DIM), jnp.float32)],
        compiler_params=_params(("arbitrary",), 40),
        name="prep_c",
    )(qkv, qkv, qkv, gq.reshape(1, -1), gk.reshape(1, -1), alibi_q)


def alibi_query_columns(blocks_per_seq):
    sigma = 2.0 ** (-8.0 * jnp.arange(1, C_HEADS + 1, dtype=jnp.float32) / C_HEADS) * LOG2E
    pieces = []
    rest = sigma
    for _ in range(ALIBI_COLS // 2):
        piece = rest.astype(jnp.bfloat16).astype(jnp.float32)
        pieces += [piece * float(LANES), piece]
        rest = rest - piece
    cols = jnp.stack(pieces, axis=1)
    cols = jnp.pad(cols, ((0, 0), (blocks_per_seq, LANES - blocks_per_seq - ALIBI_COLS)))
    return cols.reshape(C_HEADS, 1, LANES)


def _router_kernel(h_ref, g_ref, wr_ref, b_ref, gate_ref, grp_ref):
    x = h_ref[...]
    t = x * _rms_scale(x, x.shape[1]) * g_ref[...]
    logits = lax.dot_general(wr_ref[...], t, _NT, precision=lax.Precision.HIGHEST,
                             preferred_element_type=jnp.float32)
    scores = 1.0 / (1.0 + jnp.exp(-logits))
    biased = scores + b_ref[...]
    s = [scores[e:e + 1, :] for e in range(N_EXPERTS)]
    b = [biased[e:e + 1, :] for e in range(N_EXPERTS)]
    best_val, best_grp = None, None
    for g in range(N_GROUPS):
        mem = b[g * GROUP:(g + 1) * GROUP]
        top2 = None
        for x1 in range(GROUP):
            for x2 in range(x1 + 1, GROUP):
                pair = mem[x1] + mem[x2]
                top2 = pair if top2 is None else jnp.maximum(top2, pair)
        if g == 0:
            best_val, best_grp = top2, jnp.zeros_like(top2, dtype=jnp.int32)
        else:
            better = top2 > best_val
            best_grp = jnp.where(better, g, best_grp)
            best_val = jnp.where(better, top2, best_val)
    picked = []
    for e in range(N_EXPERTS):
        g = e // GROUP
        ahead = jnp.zeros_like(best_grp)
        for o in range(g * GROUP, (g + 1) * GROUP):
            if o == e:
                continue
            wins = (b[o] > b[e]) | ((b[o] == b[e]) & (o < e))
            ahead = ahead + wins.astype(jnp.int32)
        picked.append(jnp.where((best_grp == g) & (ahead < TOP_K), s[e], 0.0))
    total = picked[0]
    for e in range(1, N_EXPERTS):
        total = total + picked[e]
    for e in range(N_EXPERTS):
        gate_ref[e:e + 1, :] = picked[e] / total
    grp_ref[...] = best_grp


def router(h, g, w_router_t, bias, *, tm=512):
    n, d = h.shape
    return pl.pallas_call(
        _router_kernel,
        grid=(n // tm,),
        in_specs=[pl.BlockSpec((tm, d), lambda i: (i, 0)),
                  pl.BlockSpec((1, d), lambda i: (0, 0)),
                  pl.BlockSpec((N_EXPERTS, d), lambda i: (0, 0)),
                  pl.BlockSpec((N_EXPERTS, 1), lambda i: (0, 0))],
        out_specs=[pl.BlockSpec((N_EXPERTS, tm), lambda i: (0, i)),
                   pl.BlockSpec((1, tm), lambda i: (0, i))],
        out_shape=[jax.ShapeDtypeStruct((N_EXPERTS, n), jnp.float32),
                   jax.ShapeDtypeStruct((1, n), jnp.int32)],
        compiler_params=_params(("parallel",), 40),
        name="router",
    )(h, g.reshape(1, d), w_router_t, bias.reshape(N_EXPERTS, 1))


def _row_copies(tok_ref, base, rows, hbm_ref, buf_ref, sem, to_hbm):
    def copy(r):
        hbm_row = hbm_ref.at[pl.ds(tok_ref[base + r], 1), :]
        buf_row = buf_ref.at[pl.ds(r, 1), :]
        src, dst = (buf_row, hbm_row) if to_hbm else (hbm_row, buf_row)
        return pltpu.make_async_copy(src, dst, sem)

    def start(r, carry):
        copy(r).start()
        return carry

    def wait(r, carry):
        copy(r).wait()
        return carry

    lax.fori_loop(0, rows, start, 0, unroll=8)
    lax.fori_loop(0, rows, wait, 0, unroll=8)


def _moe_kernel(tile_ref, grp_ref, flag_ref, tok_ref, h_hbm, g_ref, gate_ref, wg_ref, wu_ref, wd_ref,
                out_hbm, hbuf, tbuf, sem, *, tm, halves):
    j = pl.program_id(0)
    sub = pl.program_id(1)
    flag = flag_ref[j]
    base = tile_ref[j] * tm

    @pl.when((sub == 0) & ((flag & 2) != 0))
    def _():
        _row_copies(tok_ref, base, tm, h_hbm, hbuf, sem, to_hbm=False)
        x = hbuf[...]
        tbuf[...] = (x * _rms_scale(x, x.shape[1]) * g_ref[...]).astype(jnp.bfloat16)

    @pl.when((flag & 1) != 0)
    def _():
        e = grp_ref[j] * GROUP + sub // halves
        t = tbuf[...]
        a = jnp.dot(t, wg_ref[0, 0].astype(jnp.bfloat16), preferred_element_type=jnp.float32)
        u = jnp.dot(t, wu_ref[0, 0].astype(jnp.bfloat16), preferred_element_type=jnp.float32)
        hid = (a / (1.0 + jnp.exp(-a))) * u
        y = jnp.dot(hid.astype(jnp.bfloat16), wd_ref[0, 0].astype(jnp.bfloat16),
                    preferred_element_type=jnp.float32)
        gates = gate_ref[...]
        lane = lax.broadcasted_iota(jnp.int32, gates.shape, 1)
        g_col = jnp.sum(jnp.where(lane == e, gates, 0.0), axis=1, keepdims=True)
        hbuf[...] += g_col * y

    @pl.when((sub == pl.num_programs(1) - 1) & ((flag & 4) != 0))
    def _():
        _row_copies(tok_ref, base, tm, out_hbm, hbuf, sem, to_hbm=True)


def _moe_worklist(grp, tm):
    n = grp.shape[0]
    n_tiles = n // tm
    order = jnp.argsort(grp, stable=True).astype(jnp.int32)
    counts = jnp.sum(grp[None, :] == jnp.arange(N_GROUPS, dtype=jnp.int32)[:, None], axis=1)
    ends = jnp.cumsum(counts)
    first_slot = jnp.arange(n_tiles, dtype=jnp.int32) * tm
    g_lo = jnp.sum(ends[None, :] <= first_slot[:, None], axis=1)
    g_hi = jnp.sum(ends[None, :] <= (first_slot + tm - 1)[:, None], axis=1)
    cnt = g_hi - g_lo + 1
    first_item = jnp.cumsum(cnt) - cnt
    total = first_item[-1] + cnt[-1]
    n_items = n_tiles + N_GROUPS - 1
    item = jnp.arange(n_items, dtype=jnp.int32)
    tile = jnp.minimum(jnp.sum((first_item + cnt)[None, :] <= item[:, None], axis=1), n_tiles - 1)
    valid = item < total
    grp_of = jnp.where(valid, g_lo[tile] + item - first_item[tile], g_hi[n_tiles - 1])
    is_first = valid & (item == first_item[tile])
    is_last = valid & (item == first_item[tile] + cnt[tile] - 1)
    flag = valid.astype(jnp.int32) + 2 * is_first.astype(jnp.int32) + 4 * is_last.astype(jnp.int32)
    return order, tile.astype(jnp.int32), grp_of.astype(jnp.int32), flag


def moe_grouped(h, ffn_g, gate_t, grp, wg, wu, wd, *, layer, tm=MOE_TM, halves=MOE_FF_SPLIT):
    n, d = h.shape
    ff = wg.shape[3] // halves
    order, tile, grp_of, flag = _moe_worklist(grp.reshape(n), tm)
    gates_sorted = jnp.take(gate_t.T, order, axis=0)
    expert = lambda j, sub, tile_ref, grp_ref, flag_ref, tok_ref: grp_ref[j] * GROUP + sub // halves
    grid_spec = pltpu.PrefetchScalarGridSpec(
        num_scalar_prefetch=4,
        grid=(tile.shape[0], GROUP * halves),
        in_specs=[pl.BlockSpec(memory_space=pl.ANY),
                  pl.BlockSpec((1, d), lambda j, sub, *_: (0, 0)),
                  pl.BlockSpec((tm, N_EXPERTS), lambda j, sub, tile_ref, *_: (tile_ref[j], 0)),
                  pl.BlockSpec((1, 1, d, ff),
                               lambda j, sub, *p: (layer, expert(j, sub, *p), 0, sub % halves)),
                  pl.BlockSpec((1, 1, d, ff),
                               lambda j, sub, *p: (layer, expert(j, sub, *p), 0, sub % halves)),
                  pl.BlockSpec((1, 1, ff, d),
                               lambda j, sub, *p: (layer, expert(j, sub, *p), sub % halves, 0))],
        out_specs=pl.BlockSpec(memory_space=pl.ANY),
        scratch_shapes=[pltpu.VMEM((tm, d), jnp.float32),
                        pltpu.VMEM((tm, d), jnp.bfloat16),
                        pltpu.SemaphoreType.DMA(())])
    return pl.pallas_call(
        functools.partial(_moe_kernel, tm=tm, halves=halves),
        grid_spec=grid_spec,
        out_shape=jax.ShapeDtypeStruct((n, d), jnp.float32),
        compiler_params=_params(("arbitrary", "arbitrary"), 48),
        name="moe_grouped",
    )(tile, grp_of, flag, order, h, ffn_g.reshape(1, d), gates_sorted, wg, wu, wd)


def moe_layer(h, ffn_g, w_router_t, router_bias, wg, wu, wd, *, layer):
    gate_t, grp = router(h, ffn_g, w_router_t, router_bias)
    return moe_grouped(h, ffn_g, gate_t, grp, wg, wu, wd, layer=layer)


def _pad_cols(w, width):
    return jnp.pad(w, ((0, 0), (0, width - w.shape[1])))


def _pad_heads(w, heads, real, padded):
    k = w.shape[0]
    w = w.reshape(k, heads, real)
    return jnp.pad(w, ((0, 0), (0, 0), (0, padded - real))).reshape(k, heads * padded)


def kernel(x, attn_norm, ev_w_in, ev_q_a_norm, ev_w_q_b, ev_kv_a_norm, ev_w_kv_b, ev_qn_a, ev_kn_a, ev_qn_b, ev_kn_b, ev_w_o, od_w_qkv, od_qn, od_kn, od_w_o, ffn_norm, w_router, router_bias, w_gate, w_up, w_down):
    batch, seq, d = x.shape
    n = batch * seq
    bf = jnp.bfloat16
    h = x.reshape(n, d)
    w_router_t = w_router.T
    depth = attn_norm.shape[0]
    for layer in range(depth):
        i = layer // 2
        if layer % 2 == 0:
            w_in = _pad_cols(ev_w_in[i], EVEN_IN_PAD).astype(bf)
            proj = norm_matmul(h, attn_norm[layer], w_in)
            qkv_a = prep_a(proj, ev_qn_a[i], ev_kn_a[i], batch=batch, seq=seq)
            out_a = dilated_attention(qkv_a, batch=batch, seq=seq)
            w_qb = _pad_heads(ev_w_q_b[i], B_HEADS, MLA_QK, MLA_QK_PAD).astype(bf)
            qb = norm_matmul(proj, ev_q_a_norm[i], w_qb, col_block=A_QKV // Q_LORA)
            ckv = norm_matmul(proj, ev_kv_a_norm[i], ev_w_kv_b[i].astype(bf),
                              col_block=(A_QKV + Q_LORA) // KV_LORA)
            gq_pad = jnp.pad(ev_qn_b[i], (0, MLA_QK_PAD - MLA_QK)).reshape(1, -1)
            gk_nope = ev_kn_b[i][:HEAD_DIM].reshape(1, -1)
            gk_pe = jnp.pad(ev_kn_b[i][HEAD_DIM:], (0, LANES - ROPE_DIM)).reshape(1, -1)
            q_b, k_b, v_b = prep_b(qb, ckv, proj, gq_pad, gk_nope, gk_pe, rope_tables(seq), seq=seq)
            out_b = flash_attention(q_b, k_b, v_b, batch=batch, seq=seq,
                                    tq=min(FLASH_TQ, seq), ck=FLASH_CK)
            h = matmul_residual([out_a, out_b], ev_w_o[i].astype(bf), h)
        else:
            qkv = norm_matmul(h, attn_norm[layer], od_w_qkv[i].astype(bf))
            q_c, k_c, v_c = prep_c(qkv, od_qn[i], od_kn[i],
                                   alibi_query_columns(seq // MOBA_BLOCK), seq=seq)
            out_c = flash_attention(q_c, k_c, v_c, batch=batch, seq=seq, tq=min(FLASH_TQ, seq),
                                    ck=MOBA_BLOCK)
            h = matmul_residual([out_c], od_w_o[i].astype(bf), h)
        h = moe_layer(h, ffn_norm[layer], w_router_t, router_bias, w_gate, w_up, w_down, layer=layer)
    return h.reshape(batch, seq, d)
```

```python
import functools
import math

import jax
import jax.numpy as jnp
from jax import lax
from jax.experimental import pallas as pl
from jax.experimental.pallas import tpu as pltpu

EPS = 1e-6
NEG = -1e30
LANES = 128
HEAD_DIM = 128
ROPE_DIM = 64
ROPE_THETA = 10000.0
A_HEADS = 8
A_PATTERNS = ((128, 1), (512, 4), (2048, 16))
B_HEADS = 8
Q_LORA = 512
KV_LORA = 256
MLA_QK = 192
MLA_QK_PAD = 256
C_HEADS = 16
MOBA_BLOCK = 256
MOBA_TOPK = 3
N_EXPERTS = 16
N_GROUPS = 4
GROUP = N_EXPERTS // N_GROUPS
TOP_K = 2
EXPERT_FF = 512
A_QKV = 3 * A_HEADS * HEAD_DIM
EVEN_IN_PAD = 4096
KPE_COL = A_QKV + Q_LORA + KV_LORA
MIB = 1024 * 1024
FLASH_TQ = 1024
FLASH_CK = 512
ONES_ROWS = 16
ALIBI_COLS = 6
LOG2E = 1.4426950408889634
MOE_TM = 1024
MOE_FF_SPLIT = 2

_NT = (((1,), (1,)), ((), ()))


def _params(semantics, vmem_mib):
    return pltpu.CompilerParams(dimension_semantics=semantics,
                                vmem_limit_bytes=vmem_mib * MIB)


def _rms_scale(x, width):
    return lax.rsqrt(jnp.sum(x * x, axis=-1, keepdims=True) * (1.0 / width) + EPS)


def _norm_matmul_kernel(x_ref, g_ref, w_ref, o_ref, xn_ref, *, width):
    @pl.when(pl.program_id(1) == 0)
    def _():
        x = x_ref[...]
        xn_ref[...] = (x * _rms_scale(x, width) * g_ref[...]).astype(jnp.bfloat16)

    o_ref[...] = jnp.dot(xn_ref[...], w_ref[...],
                         preferred_element_type=jnp.float32).astype(o_ref.dtype)


def norm_matmul(x, g, w, *, col_block=0, tm=1024, tn=1024):
    n = x.shape[0]
    k, m = w.shape
    return pl.pallas_call(
        functools.partial(_norm_matmul_kernel, width=k),
        grid=(n // tm, m // tn),
        in_specs=[pl.BlockSpec((tm, k), lambda i, j: (i, col_block)),
                  pl.BlockSpec((1, k), lambda i, j: (0, 0)),
                  pl.BlockSpec((k, tn), lambda i, j: (0, j))],
        out_specs=pl.BlockSpec((tm, tn), lambda i, j: (i, j)),
        out_shape=jax.ShapeDtypeStruct((n, m), jnp.float32),
        scratch_shapes=[pltpu.VMEM((tm, k), jnp.bfloat16)],
        compiler_params=_params(("parallel", "arbitrary"), 48),
        name="norm_matmul",
    )(x, g.reshape(1, k), w)


def _matmul_res_kernel(*refs, n_a):
    a_refs, w_refs = refs[:n_a], refs[n_a:2 * n_a]
    r_ref, o_ref = refs[2 * n_a], refs[2 * n_a + 1]
    acc = r_ref[...]
    for a_ref, w_ref in zip(a_refs, w_refs):
        acc = acc + jnp.dot(a_ref[...], w_ref[...], preferred_element_type=jnp.float32)
    o_ref[...] = acc


def matmul_residual(a_list, w, res, *, tm=512, tn=1024):
    n, m = res.shape
    n_a = len(a_list)
    ka = a_list[0].shape[1]
    in_specs = [pl.BlockSpec((tm, ka), lambda i, j: (i, 0)) for _ in a_list]
    in_specs += [pl.BlockSpec((ka, tn), functools.partial(lambda i, j, p: (p, j), p=p))
                 for p in range(n_a)]
    in_specs += [pl.BlockSpec((tm, tn), lambda i, j: (i, j))]
    return pl.pallas_call(
        functools.partial(_matmul_res_kernel, n_a=n_a),
        grid=(n // tm, m // tn),
        in_specs=in_specs,
        out_specs=pl.BlockSpec((tm, tn), lambda i, j: (i, j)),
        out_shape=jax.ShapeDtypeStruct((n, m), jnp.float32),
        compiler_params=_params(("parallel", "parallel"), 40),
        name="matmul_residual",
    )(*a_list, *([w] * n_a), res)


def _prep_a_kernel(q_ref, k_ref, v_ref, gq_ref, gk_ref, *refs, scale):
    n_pat = len(A_PATTERNS)
    outs, (qs, ks, vs) = refs[:3 * n_pat], refs[3 * n_pat:]
    heads = [slice(h * HEAD_DIM, (h + 1) * HEAD_DIM) for h in range(A_HEADS)]
    for h, sl in enumerate(heads):
        q = q_ref[:, sl]
        qs[h] = q * _rms_scale(q, HEAD_DIM) * gq_ref[...] * scale
        k = k_ref[:, sl]
        ks[h] = k * _rms_scale(k, HEAD_DIM) * gk_ref[...]
        vs[h] = v_ref[:, sl]
    rows = qs.shape[1]
    for p, (_, dil) in enumerate(A_PATTERNS):
        for src, dst in zip((qs, ks, vs), outs[3 * p:3 * p + 3]):
            for r in range(dil):
                for h, sl in enumerate(heads):
                    dst[0, r, :, sl] = src[h, pl.ds(r, rows // dil, stride=dil), :].astype(
                        jnp.bfloat16)


def prep_a(proj, gq, gk, *, batch, seq, tm=512):
    n = proj.shape[0]
    w = A_HEADS * HEAD_DIM
    per_seq = seq // tm
    blk = lambda c: pl.BlockSpec((tm, w), functools.partial(lambda i, c: (i, c), c=c))
    gspec = pl.BlockSpec((1, HEAD_DIM), lambda i: (0, 0))
    out_specs, out_shape = [], []
    for _, dil in A_PATTERNS:
        spec = pl.BlockSpec((1, dil, tm // dil, w), lambda i: (i // per_seq, 0, i % per_seq, 0))
        shape = jax.ShapeDtypeStruct((batch, dil, seq // dil, w), jnp.bfloat16)
        out_specs += [spec] * 3
        out_shape += [shape] * 3
    outs = pl.pallas_call(
        functools.partial(_prep_a_kernel, scale=HEAD_DIM ** -0.5),
        grid=(n // tm,),
        in_specs=[blk(0), blk(1), blk(2), gspec, gspec],
        out_specs=out_specs,
        out_shape=out_shape,
        scratch_shapes=[pltpu.VMEM((A_HEADS, tm, HEAD_DIM), jnp.float32)] * 3,
        compiler_params=_params(("parallel",), 48),
        name="prep_a",
    )(proj, proj, proj, gq.reshape(1, -1), gk.reshape(1, -1))
    return [outs[3 * p:3 * p + 3] for p in range(len(A_PATTERNS))]


def _dilated_kernel(q_ref, kc_ref, kp_ref, vc_ref, vp_ref, o_ref, lse_ref, *, dil):
    band = HEAD_DIM
    n = pl.program_id(2)
    qi = lax.broadcasted_iota(jnp.int32, (band, band), 0)
    ki = lax.broadcasted_iota(jnp.int32, (band, band), 1)
    dist_c = (qi - ki).astype(jnp.float32)
    dist_p = dist_c + float(band)
    valid_c = qi >= ki
    valid_p = jnp.logical_and(ki >= qi, n > 0)
    lse_tile = jnp.zeros((band, LANES), jnp.float32)
    for h in range(A_HEADS):
        sl = slice(h * HEAD_DIM, (h + 1) * HEAD_DIM)
        slope = 2.0 ** (-8.0 * (h + 1) / A_HEADS) * dil
        q = q_ref[0, 0, :, sl]
        sc = lax.dot_general(q, kc_ref[0, 0, :, sl], _NT, preferred_element_type=jnp.float32)
        sp = lax.dot_general(q, kp_ref[0, 0, :, sl], _NT, preferred_element_type=jnp.float32)
        sc = jnp.where(valid_c, sc - slope * dist_c, NEG)
        sp = jnp.where(valid_p, sp - slope * dist_p, NEG)
        m = jnp.maximum(jnp.max(sc, axis=1, keepdims=True), jnp.max(sp, axis=1, keepdims=True))
        pc = jnp.exp(sc - m)
        pp = jnp.exp(sp - m)
        l = jnp.sum(pc, axis=1, keepdims=True) + jnp.sum(pp, axis=1, keepdims=True)
        acc = jnp.dot(pc.astype(jnp.bfloat16), vc_ref[0, 0, :, sl],
                      preferred_element_type=jnp.float32)
        acc = acc + jnp.dot(pp.astype(jnp.bfloat16), vp_ref[0, 0, :, sl],
                            preferred_element_type=jnp.float32)
        o_ref[0, 0, :, sl] = acc / l
        lse_tile = jnp.where(ki == h, m + jnp.log(l), lse_tile)
    lse_ref[0, 0] = lse_tile


def dilated_pattern(q, k, v, *, dil):
    batch, _, sub, w = q.shape
    band = HEAD_DIM
    cur = pl.BlockSpec((1, 1, band, w), lambda b, r, n: (b, r, n, 0))
    prev = pl.BlockSpec((1, 1, band, w), lambda b, r, n: (b, r, jnp.maximum(n - 1, 0), 0))
    return pl.pallas_call(
        functools.partial(_dilated_kernel, dil=dil),
        grid=(batch, dil, sub // band),
        in_specs=[cur, cur, prev, cur, prev],
        out_specs=[cur, pl.BlockSpec((1, 1, band, LANES), lambda b, r, n: (b, r, n, 0))],
        out_shape=[jax.ShapeDtypeStruct((batch, dil, sub, w), jnp.float32),
                   jax.ShapeDtypeStruct((batch, dil, sub, LANES), jnp.float32)],
        compiler_params=_params(("parallel", "parallel", "arbitrary"), 40),
        name=f"dilated_d{dil}",
    )(q, k, k, v, v)


def _dilated_merge_kernel(*refs):
    n_pat = len(A_PATTERNS)
    ins, out_ref, scratch = refs[:2 * n_pat], refs[2 * n_pat], refs[2 * n_pat + 1:]
    rows = out_ref.shape[0]
    heads = [slice(h * HEAD_DIM, (h + 1) * HEAD_DIM) for h in range(A_HEADS)]
    outs, lses = [], []
    for p, (_, dil) in enumerate(A_PATTERNS):
        o_ref, l_ref = ins[2 * p], ins[2 * p + 1]
        if dil == 1:
            outs.append(lambda h, o_ref=o_ref: o_ref[0, 0, :, heads[h]])
            lses.append(l_ref[0, 0])
            continue
        o_nat, l_nat = scratch[2 * p], scratch[2 * p + 1]
        for r in range(dil):
            for h, sl in enumerate(heads):
                o_nat[h, pl.ds(r, rows // dil, stride=dil), :] = o_ref[0, r, :, sl]
            l_nat[pl.ds(r, rows // dil, stride=dil), :] = l_ref[0, r]
        outs.append(lambda h, o_nat=o_nat: o_nat[h])
        lses.append(l_nat[...])
    for h, sl in enumerate(heads):
        lse_h = [l[:, h:h + 1] for l in lses]
        top = functools.reduce(jnp.maximum, lse_h)
        wts = [jnp.exp(l - top) for l in lse_h]
        num = sum(wt * o(h) for wt, o in zip(wts, outs))
        out_ref[:, sl] = (num / sum(wts)).astype(out_ref.dtype)


def dilated_attention(qkv_by_pattern, *, batch, seq, tm=512):
    w = A_HEADS * HEAD_DIM
    per_seq = seq // tm
    ins, in_specs, scratch = [], [], []
    for (q, k, v), (window, dil) in zip(qkv_by_pattern, A_PATTERNS):
        assert window // dil == HEAD_DIM
        o, lse = dilated_pattern(q, k, v, dil=dil)
        ins += [o, lse]
        in_specs += [pl.BlockSpec((1, dil, tm // dil, w), lambda i: (i // per_seq, 0, i % per_seq, 0)),
                     pl.BlockSpec((1, dil, tm // dil, LANES),
                                  lambda i: (i // per_seq, 0, i % per_seq, 0))]
        scratch += [pltpu.VMEM((A_HEADS, tm, HEAD_DIM), jnp.float32),
                    pltpu.VMEM((tm, LANES), jnp.float32)]
    n = batch * seq
    return pl.pallas_call(
        _dilated_merge_kernel,
        grid=(n // tm,),
        in_specs=in_specs,
        out_specs=pl.BlockSpec((tm, w), lambda i: (i, 0)),
        out_shape=jax.ShapeDtypeStruct((n, w), jnp.bfloat16),
        scratch_shapes=scratch,
        compiler_params=_params(("parallel",), 48),
        name="dilated_merge",
    )(*ins)


def _rope(t, cos, sin_lo, sin_hi):
    return (t * cos + pltpu.roll(t, LANES - ROPE_DIM // 2, 1) * sin_lo
            + pltpu.roll(t, ROPE_DIM // 2, 1) * sin_hi)


def _prep_b_kernel(qb_ref, ckv_ref, kpe_ref, gq_ref, gkn_ref, gkp_ref, cos_ref, slo_ref, shi_ref,
                   qo_ref, ko_ref, vo_ref, *, scale):
    cos, slo, shi = cos_ref[...], slo_ref[...], shi_ref[...]
    kpe = kpe_ref[...]
    kpe_ss = jnp.sum(kpe * kpe, axis=-1, keepdims=True)
    kpe_rot = _rope(kpe * gkp_ref[...], cos, slo, shi)
    for h in range(B_HEADS):
        lo = slice(h * MLA_QK_PAD, h * MLA_QK_PAD + HEAD_DIM)
        hi = slice(h * MLA_QK_PAD + HEAD_DIM, (h + 1) * MLA_QK_PAD)
        q_n, q_p = qb_ref[:, lo], qb_ref[:, hi]
        ss = jnp.sum(q_n * q_n, axis=-1, keepdims=True) + jnp.sum(q_p * q_p, axis=-1, keepdims=True)
        rq = lax.rsqrt(ss * (1.0 / MLA_QK) + EPS) * scale
        qo_ref[h, :, :HEAD_DIM] = (q_n * rq * gq_ref[:, :HEAD_DIM]).astype(jnp.bfloat16)
        qo_ref[h, :, HEAD_DIM:] = _rope(q_p * rq * gq_ref[:, HEAD_DIM:], cos, slo, shi
                                        ).astype(jnp.bfloat16)
        k_n = ckv_ref[:, lo]
        ss = jnp.sum(k_n * k_n, axis=-1, keepdims=True) + kpe_ss
        rk = lax.rsqrt(ss * (1.0 / MLA_QK) + EPS)
        ko_ref[h, :, :HEAD_DIM] = (k_n * rk * gkn_ref[...]).astype(jnp.bfloat16)
        ko_ref[h, :, HEAD_DIM:] = (kpe_rot * rk).astype(jnp.bfloat16)
        vo_ref[h, :HEAD_DIM, :] = ckv_ref[:, hi].T.astype(jnp.bfloat16)
        vo_ref[h, HEAD_DIM:, :] = jnp.ones((ONES_ROWS, ckv_ref.shape[0]), jnp.bfloat16)


def prep_b(qb, ckv, proj, gq_pad, gk_nope, gk_pe_pad, rope_tabs, *, seq, tm=256):
    n = qb.shape[0]
    wide = B_HEADS * MLA_QK_PAD
    nblk = seq // tm
    row = lambda c: pl.BlockSpec((tm, wide), lambda i: (i, 0))
    tab = pl.BlockSpec((tm, LANES), lambda i: (i % nblk, 0))
    return pl.pallas_call(
        functools.partial(_prep_b_kernel, scale=MLA_QK ** -0.5 * LOG2E),
        grid=(n // tm,),
        in_specs=[row(0), row(0),
                  pl.BlockSpec((tm, LANES), lambda i: (i, KPE_COL // LANES)),
                  pl.BlockSpec((1, MLA_QK_PAD), lambda i: (0, 0)),
                  pl.BlockSpec((1, LANES), lambda i: (0, 0)),
                  pl.BlockSpec((1, LANES), lambda i: (0, 0)),
                  tab, tab, tab],
        out_specs=[pl.BlockSpec((B_HEADS, tm, MLA_QK_PAD), lambda i: (0, i, 0)),
                   pl.BlockSpec((B_HEADS, tm, MLA_QK_PAD), lambda i: (0, i, 0)),
                   pl.BlockSpec((B_HEADS, HEAD_DIM + ONES_ROWS, tm), lambda i: (0, 0, i))],
        out_shape=[jax.ShapeDtypeStruct((B_HEADS, n, MLA_QK_PAD), jnp.bfloat16),
                   jax.ShapeDtypeStruct((B_HEADS, n, MLA_QK_PAD), jnp.bfloat16),
                   jax.ShapeDtypeStruct((B_HEADS, HEAD_DIM + ONES_ROWS, n), jnp.bfloat16)],
        compiler_params=_params(("parallel",), 40),
        name="prep_b",
    )(qb, ckv, proj, gq_pad, gk_nope, gk_pe_pad, *rope_tabs)


def rope_tables(seq):
    half = ROPE_DIM // 2
    inv = ROPE_THETA ** (-jnp.arange(0, ROPE_DIM, 2, dtype=jnp.float32) / ROPE_DIM)
    ang = jnp.arange(seq, dtype=jnp.float32)[:, None] * inv[None, :]
    cos, sin = jnp.cos(ang), jnp.sin(ang)
    z = jnp.zeros((seq, half), jnp.float32)
    pad = jnp.zeros((seq, LANES - ROPE_DIM), jnp.float32)
    return (jnp.concatenate([cos, cos, pad], axis=1),
            jnp.concatenate([-sin, z, pad], axis=1),
            jnp.concatenate([z, sin, pad], axis=1))


def _flash_kernel(q_ref, k_ref, vt_ref, o_ref, m_ref, acc_ref, *, tq, ck, dv):
    i = pl.program_id(2)
    nb = tq // ck
    n_past = i * nb
    m_ref[...] = jnp.full_like(m_ref, NEG)
    acc_ref[...] = jnp.zeros_like(acc_ref)

    def scores(j, r0):
        start = pl.multiple_of(j * ck, ck)
        return lax.dot_general(k_ref[0, pl.ds(start, ck), :], q_ref[0, r0:, :], _NT,
                               preferred_element_type=jnp.float32)

    def fold(j, r0, s):
        start = pl.multiple_of(j * ck, ck)
        probs, alphas = [], []
        for c0 in range(0, tq - r0, LANES):
            cols = slice(r0 + c0, r0 + c0 + LANES)
            s_c = s[:, c0:c0 + LANES]
            m_prev = m_ref[:, cols]
            m_new = jnp.maximum(m_prev, jnp.max(s_c, axis=0, keepdims=True))
            m_ref[:, cols] = m_new
            alphas.append(jnp.exp2(m_prev - m_new))
            probs.append(jnp.exp2(s_c - m_new).astype(jnp.bfloat16))
        p = jnp.concatenate(probs, axis=1)
        alpha = jnp.concatenate(alphas, axis=1)
        acc_ref[:, r0:] = alpha * acc_ref[:, r0:] + jnp.dot(
            vt_ref[0, :, pl.ds(start, ck)], p, preferred_element_type=jnp.float32)

    def past(j, s):
        s_next = scores(j + 1, 0)
        fold(j, 0, s)
        return s_next

    s = lax.fori_loop(0, n_past, past, scores(0, 0))
    for jb in range(nb):
        j = n_past + jb
        r0 = jb * ck
        if jb > 0:
            s = scores(j, r0)
        ki = lax.broadcasted_iota(jnp.int32, (ck, ck), 0)
        qi = lax.broadcasted_iota(jnp.int32, (ck, ck), 1)
        own = jnp.where(qi >= ki, s[:, :ck], NEG)
        fold(j, r0, own if tq - r0 == ck else jnp.concatenate([own, s[:, ck:]], axis=1))
    acc = acc_ref[...]
    o_ref[...] = (acc[:dv] / acc[dv:dv + 1]).T.astype(o_ref.dtype)


def flash_attention(q, k, vt, *, batch, seq, tq, ck):
    heads, n, dq = q.shape
    dvp = vt.shape[1]
    dv = dvp - ONES_ROWS
    nq = seq // tq
    return pl.pallas_call(
        functools.partial(_flash_kernel, tq=tq, ck=ck, dv=dv),
        grid=(batch, heads, nq),
        in_specs=[pl.BlockSpec((1, tq, dq), lambda b, h, i: (h, b * nq + i, 0)),
                  pl.BlockSpec((1, seq, dq), lambda b, h, i: (h * batch + b, 0, 0)),
                  pl.BlockSpec((1, dvp, seq), lambda b, h, i: (h, 0, b))],
        out_specs=pl.BlockSpec((tq, dv), lambda b, h, i: (b * nq + i, h)),
        out_shape=jax.ShapeDtypeStruct((n, heads * dv), jnp.bfloat16),
        scratch_shapes=[pltpu.VMEM((1, tq), jnp.float32),
                        pltpu.VMEM((dvp, tq), jnp.float32)],
        compiler_params=_params(("parallel", "parallel", "arbitrary"), 48),
        name=f"flash_h{heads}",
    )(q, k.reshape(heads * batch, seq, dq), vt)


def _prep_c_kernel(q_ref, k_ref, v_ref, gq_ref, gk_ref, alibi_ref, qo_ref, ko_ref, vo_ref,
                   kmean_ref, *, blocks_per_seq, scale):
    i = pl.program_id(0)
    bq = i % blocks_per_seq

    @pl.when(bq == 0)
    def _():
        kmean_ref[...] = jnp.zeros_like(kmean_ref)

    rows = q_ref.shape[0]
    nblk = blocks_per_seq
    blk = lax.broadcasted_iota(jnp.int32, (nblk, rows), 0)
    lane = lax.broadcasted_iota(jnp.int32, (rows, LANES), 1)
    kpos = bq * rows + lax.broadcasted_iota(jnp.int32, (rows, LANES), 0)
    pos_part = jnp.where((lane - nblk) % 2 == 0, jnp.right_shift(kpos, 7), kpos & (LANES - 1))
    k_extra = jnp.where(lane < nblk, (lane == bq).astype(jnp.float32),
                        jnp.where(lane < nblk + ALIBI_COLS, pos_part.astype(jnp.float32), 0.0))
    k_extra = k_extra.astype(jnp.bfloat16)
    ones = jnp.ones((ONES_ROWS, rows), jnp.bfloat16)
    for h in range(C_HEADS):
        sl = slice(h * HEAD_DIM, (h + 1) * HEAD_DIM)
        q = q_ref[:, sl]
        qn = q * _rms_scale(q, HEAD_DIM) * gq_ref[...]
        k = k_ref[:, sl]
        kn = k * _rms_scale(k, HEAD_DIM) * gk_ref[...]
        qo_ref[h, :, :HEAD_DIM] = (qn * scale).astype(jnp.bfloat16)
        ko_ref[h, :, :HEAD_DIM] = kn.astype(jnp.bfloat16)
        ko_ref[h, :, HEAD_DIM:] = k_extra
        vo_ref[h, :HEAD_DIM, :] = v_ref[:, sl].T.astype(jnp.bfloat16)
        vo_ref[h, HEAD_DIM:, :] = ones
        gate = lax.dot_general(kmean_ref[h], qn, _NT, precision=lax.Precision.HIGHEST,
                               preferred_element_type=jnp.float32)
        gate = jnp.where(blk < bq, gate, NEG)
        kmean_ref[h, pl.ds(bq, 1), :] = jnp.mean(kn, axis=0, keepdims=True)
        allowed = blk == bq
        for r in range(MOBA_TOPK):
            best = jnp.max(gate, axis=0, keepdims=True)
            idx = jnp.min(jnp.where(gate == best, blk, nblk), axis=0, keepdims=True)
            allowed = allowed | ((blk == idx) & (r < bq))
            gate = jnp.where(blk == idx, 2.0 * NEG, gate)
        block_bias = jnp.where(allowed, 0.0, NEG)
        block_bias = jnp.concatenate(
            [block_bias, jnp.zeros((LANES - nblk, rows), jnp.float32)], axis=0)
        qo_ref[h, :, HEAD_DIM:] = (block_bias.T + alibi_ref[h]).astype(jnp.bfloat16)


def prep_c(qkv, gq, gk, alibi_q, *, seq):
    n = qkv.shape[0]
    tm = MOBA_BLOCK
    w = C_HEADS * HEAD_DIM
    blocks_per_seq = seq // tm
    assert blocks_per_seq + ALIBI_COLS <= LANES
    blk = lambda c: pl.BlockSpec((tm, w), functools.partial(lambda i, c: (i, c), c=c))
    gspec = pl.BlockSpec((1, HEAD_DIM), lambda i: (0, 0))
    hm = pl.BlockSpec((C_HEADS, tm, 2 * HEAD_DIM), lambda i: (0, i, 0))
    hm_shape = jax.ShapeDtypeStruct((C_HEADS, n, 2 * HEAD_DIM), jnp.bfloat16)
    dvp = HEAD_DIM + ONES_ROWS
    return pl.pallas_call(
        functools.partial(_prep_c_kernel, blocks_per_seq=blocks_per_seq,
                          scale=HEAD_DIM ** -0.5 * LOG2E),
        grid=(n // tm,),
        in_specs=[blk(0), blk(1), blk(2), gspec, gspec,
                  pl.BlockSpec((C_HEADS, 1, LANES), lambda i: (0, 0, 0))],
        out_specs=[hm, hm, pl.BlockSpec((C_HEADS, dvp, tm), lambda i: (0, 0, i))],
        out_shape=[hm_shape, hm_shape, jax.ShapeDtypeStruct((C_HEADS, dvp, n), jnp.bfloat16)],
        scratch_shapes=[pltpu.VMEM((C_HEADS, blocks_per_seq, HEAD_DIM), jnp.float32)],
        compiler_params=_params(("arbitrary",), 40),
        name="prep_c",
    )(qkv, qkv, qkv, gq.reshape(1, -1), gk.reshape(1, -1), alibi_q)


def alibi_query_columns(blocks_per_seq):
    sigma = 2.0 ** (-8.0 * jnp.arange(1, C_HEADS + 1, dtype=jnp.float32) / C_HEADS) * LOG2E
    pieces = []
    rest = sigma
    for _ in range(ALIBI_COLS // 2):
        piece = rest.astype(jnp.bfloat16).astype(jnp.float32)
        pieces += [piece * float(LANES), piece]
        rest = rest - piece
    cols = jnp.stack(pieces, axis=1)
    cols = jnp.pad(cols, ((0, 0), (blocks_per_seq, LANES - blocks_per_seq - ALIBI_COLS)))
    return cols.reshape(C_HEADS, 1, LANES)


def _router_kernel(h_ref, g_ref, wr_ref, b_ref, gate_ref, grp_ref):
    x = h_ref[...]
    t = x * _rms_scale(x, x.shape[1]) * g_ref[...]
    logits = lax.dot_general(wr_ref[...], t, _NT, precision=lax.Precision.HIGHEST,
                             preferred_element_type=jnp.float32)
    scores = 1.0 / (1.0 + jnp.exp(-logits))
    biased = scores + b_ref[...]
    s = [scores[e:e + 1, :] for e in range(N_EXPERTS)]
    b = [biased[e:e + 1, :] for e in range(N_EXPERTS)]
    best_val, best_grp = None, None
    for g in range(N_GROUPS):
        mem = b[g * GROUP:(g + 1) * GROUP]
        top2 = None
        for x1 in range(GROUP):
            for x2 in range(x1 + 1, GROUP):
                pair = mem[x1] + mem[x2]
                top2 = pair if top2 is None else jnp.maximum(top2, pair)
        if g == 0:
            best_val, best_grp = top2, jnp.zeros_like(top2, dtype=jnp.int32)
        else:
            better = top2 > best_val
            best_grp = jnp.where(better, g, best_grp)
            best_val = jnp.where(better, top2, best_val)
    picked = []
    for e in range(N_EXPERTS):
        g = e // GROUP
        ahead = jnp.zeros_like(best_grp)
        for o in range(g * GROUP, (g + 1) * GROUP):
            if o == e:
                continue
            wins = (b[o] > b[e]) | ((b[o] == b[e]) & (o < e))
            ahead = ahead + wins.astype(jnp.int32)
        picked.append(jnp.where((best_grp == g) & (ahead < TOP_K), s[e], 0.0))
    total = picked[0]
    for e in range(1, N_EXPERTS):
        total = total + picked[e]
    for e in range(N_EXPERTS):
        gate_ref[e:e + 1, :] = picked[e] / total
    grp_ref[...] = best_grp


def router(h, g, w_router_t, bias, *, tm=512):
    n, d = h.shape
    return pl.pallas_call(
        _router_kernel,
        grid=(n // tm,),
        in_specs=[pl.BlockSpec((tm, d), lambda i: (i, 0)),
                  pl.BlockSpec((1, d), lambda i: (0, 0)),
                  pl.BlockSpec((N_EXPERTS, d), lambda i: (0, 0)),
                  pl.BlockSpec((N_EXPERTS, 1), lambda i: (0, 0))],
        out_specs=[pl.BlockSpec((N_EXPERTS, tm), lambda i: (0, i)),
                   pl.BlockSpec((1, tm), lambda i: (0, i))],
        out_shape=[jax.ShapeDtypeStruct((N_EXPERTS, n), jnp.float32),
                   jax.ShapeDtypeStruct((1, n), jnp.int32)],
        compiler_params=_params(("parallel",), 40),
        name="router",
    )(h, g.reshape(1, d), w_router_t, bias.reshape(N_EXPERTS, 1))


def _row_copies(tok_ref, base, rows, hbm_ref, buf_ref, sem, to_hbm):
    def copy(r):
        hbm_row = hbm_ref.at[pl.ds(tok_ref[base + r], 1), :]
        buf_row = buf_ref.at[pl.ds(r, 1), :]
        src, dst = (buf_row, hbm_row) if to_hbm else (hbm_row, buf_row)
        return pltpu.make_async_copy(src, dst, sem)

    def start(r, carry):
        copy(r).start()
        return carry

    def wait(r, carry):
        copy(r).wait()
        return carry

    lax.fori_loop(0, rows, start, 0, unroll=8)
    lax.fori_loop(0, rows, wait, 0, unroll=8)


def _moe_kernel(tile_ref, grp_ref, flag_ref, tok_ref, h_hbm, g_ref, gate_ref, wg_ref, wu_ref, wd_ref,
                out_hbm, hbuf, tbuf, sem, *, tm, halves):
    j = pl.program_id(0)
    sub = pl.program_id(1)
    flag = flag_ref[j]
    base = tile_ref[j] * tm

    @pl.when((sub == 0) & ((flag & 2) != 0))
    def _():
        _row_copies(tok_ref, base, tm, h_hbm, hbuf, sem, to_hbm=False)
        x = hbuf[...]
        tbuf[...] = (x * _rms_scale(x, x.shape[1]) * g_ref[...]).astype(jnp.bfloat16)

    @pl.when((flag & 1) != 0)
    def _():
        e = grp_ref[j] * GROUP + sub // halves
        t = tbuf[...]
        a = jnp.dot(t, wg_ref[0, 0].astype(jnp.bfloat16), preferred_element_type=jnp.float32)
        u = jnp.dot(t, wu_ref[0, 0].astype(jnp.bfloat16), preferred_element_type=jnp.float32)
        hid = (a / (1.0 + jnp.exp(-a))) * u
        y = jnp.dot(hid.astype(jnp.bfloat16), wd_ref[0, 0].astype(jnp.bfloat16),
                    preferred_element_type=jnp.float32)
        gates = gate_ref[...]
        lane = lax.broadcasted_iota(jnp.int32, gates.shape, 1)
        g_col = jnp.sum(jnp.where(lane == e, gates, 0.0), axis=1, keepdims=True)
        hbuf[...] += g_col * y

    @pl.when((sub == pl.num_programs(1) - 1) & ((flag & 4) != 0))
    def _():
        _row_copies(tok_ref, base, tm, out_hbm, hbuf, sem, to_hbm=True)


def _moe_worklist(grp, tm):
    n = grp.shape[0]
    n_tiles = n // tm
    order = jnp.argsort(grp, stable=True).astype(jnp.int32)
    counts = jnp.sum(grp[None, :] == jnp.arange(N_GROUPS, dtype=jnp.int32)[:, None], axis=1)
    ends = jnp.cumsum(counts)
    first_slot = jnp.arange(n_tiles, dtype=jnp.int32) * tm
    g_lo = jnp.sum(ends[None, :] <= first_slot[:, None], axis=1)
    g_hi = jnp.sum(ends[None, :] <= (first_slot + tm - 1)[:, None], axis=1)
    cnt = g_hi - g_lo + 1
    first_item = jnp.cumsum(cnt) - cnt
    total = first_item[-1] + cnt[-1]
    n_items = n_tiles + N_GROUPS - 1
    item = jnp.arange(n_items, dtype=jnp.int32)
    tile = jnp.minimum(jnp.sum((first_item + cnt)[None, :] <= item[:, None], axis=1), n_tiles - 1)
    valid = item < total
    grp_of = jnp.where(valid, g_lo[tile] + item - first_item[tile], g_hi[n_tiles - 1])
    is_first = valid & (item == first_item[tile])
    is_last = valid & (item == first_item[tile] + cnt[tile] - 1)
    flag = valid.astype(jnp.int32) + 2 * is_first.astype(jnp.int32) + 4 * is_last.astype(jnp.int32)
    return order, tile.astype(jnp.int32), grp_of.astype(jnp.int32), flag


def moe_grouped(h, ffn_g, gate_t, grp, wg, wu, wd, *, layer, tm=MOE_TM, halves=MOE_FF_SPLIT):
    n, d = h.shape
    ff = wg.shape[3] // halves
    order, tile, grp_of, flag = _moe_worklist(grp.reshape(n), tm)
    gates_sorted = jnp.take(gate_t.T, order, axis=0)
    expert = lambda j, sub, tile_ref, grp_ref, flag_ref, tok_ref: grp_ref[j] * GROUP + sub // halves
    grid_spec = pltpu.PrefetchScalarGridSpec(
        num_scalar_prefetch=4,
        grid=(tile.shape[0], GROUP * halves),
        in_specs=[pl.BlockSpec(memory_space=pl.ANY),
                  pl.BlockSpec((1, d), lambda j, sub, *_: (0, 0)),
                  pl.BlockSpec((tm, N_EXPERTS), lambda j, sub, tile_ref, *_: (tile_ref[j], 0)),
                  pl.BlockSpec((1, 1, d, ff),
                               lambda j, sub, *p: (layer, expert(j, sub, *p), 0, sub % halves)),
                  pl.BlockSpec((1, 1, d, ff),
                               lambda j, sub, *p: (layer, expert(j, sub, *p), 0, sub % halves)),
                  pl.BlockSpec((1, 1, ff, d),
                               lambda j, sub, *p: (layer, expert(j, sub, *p), sub % halves, 0))],
        out_specs=pl.BlockSpec(memory_space=pl.ANY),
        scratch_shapes=[pltpu.VMEM((tm, d), jnp.float32),
                        pltpu.VMEM((tm, d), jnp.bfloat16),
                        pltpu.SemaphoreType.DMA(())])
    return pl.pallas_call(
        functools.partial(_moe_kernel, tm=tm, halves=halves),
        grid_spec=grid_spec,
        out_shape=jax.ShapeDtypeStruct((n, d), jnp.float32),
        compiler_params=_params(("arbitrary", "arbitrary"), 48),
        name="moe_grouped",
    )(tile, grp_of, flag, order, h, ffn_g.reshape(1, d), gates_sorted, wg, wu, wd)


def moe_layer(h, ffn_g, w_router_t, router_bias, wg, wu, wd, *, layer):
    gate_t, grp = router(h, ffn_g, w_router_t, router_bias)
    return moe_grouped(h, ffn_g, gate_t, grp, wg, wu, wd, layer=layer)


def _pad_cols(w, width):
    return jnp.pad(w, ((0, 0), (0, width - w.shape[1])))


def _pad_heads(w, heads, real, padded):
    k = w.shape[0]
    w = w.reshape(k, heads, real)
    return jnp.pad(w, ((0, 0), (0, 0), (0, padded - real))).reshape(k, heads * padded)


def kernel(x, attn_norm, ev_w_in, ev_q_a_norm, ev_w_q_b, ev_kv_a_norm, ev_w_kv_b, ev_qn_a, ev_kn_a, ev_qn_b, ev_kn_b, ev_w_o, od_w_qkv, od_qn, od_kn, od_w_o, ffn_norm, w_router, router_bias, w_gate, w_up, w_down):
    batch, seq, d = x.shape
    n = batch * seq
    bf = jnp.bfloat16
    h = x.reshape(n, d)
    w_router_t = w_router.T
    depth = attn_norm.shape[0]
    for layer in range(depth):
        i = layer // 2
        if layer % 2 == 0:
            w_in = _pad_cols(ev_w_in[i], EVEN_IN_PAD).astype(bf)
            proj = norm_matmul(h, attn_norm[layer], w_in)
            qkv_a = prep_a(proj, ev_qn_a[i], ev_kn_a[i], batch=batch, seq=seq)
            out_a = dilated_attention(qkv_a, batch=batch, seq=seq)
            w_qb = _pad_heads(ev_w_q_b[i], B_HEADS, MLA_QK, MLA_QK_PAD).astype(bf)
            qb = norm_matmul(proj, ev_q_a_norm[i], w_qb, col_block=A_QKV // Q_LORA)
            ckv = norm_matmul(proj, ev_kv_a_norm[i], ev_w_kv_b[i].astype(bf),
                              col_block=(A_QKV + Q_LORA) // KV_LORA)
            gq_pad = jnp.pad(ev_qn_b[i], (0, MLA_QK_PAD - MLA_QK)).reshape(1, -1)
            gk_nope = ev_kn_b[i][:HEAD_DIM].reshape(1, -1)
            gk_pe = jnp.pad(ev_kn_b[i][HEAD_DIM:], (0, LANES - ROPE_DIM)).reshape(1, -1)
            q_b, k_b, v_b = prep_b(qb, ckv, proj, gq_pad, gk_nope, gk_pe, rope_tables(seq), seq=seq)
            out_b = flash_attention(q_b, k_b, v_b, batch=batch, seq=seq,
                                    tq=min(FLASH_TQ, seq), ck=FLASH_CK)
            h = matmul_residual([out_a, out_b], ev_w_o[i].astype(bf), h)
        else:
            qkv = norm_matmul(h, attn_norm[layer], od_w_qkv[i].astype(bf))
            q_c, k_c, v_c = prep_c(qkv, od_qn[i], od_kn[i],
                                   alibi_query_columns(seq // MOBA_BLOCK), seq=seq)
            out_c = flash_attention(q_c, k_c, v_c, batch=batch, seq=seq, tq=min(FLASH_TQ, seq),
                                    ck=FLASH_CK)
            h = matmul_residual([out_c], od_w_o[i].astype(bf), h)
        h = moe_layer(h, ffn_norm[layer], w_router_t, router_bias, w_gate, w_up, w_down, layer=layer)
    return h.reshape(batch, seq, d)
```

```python
import functools
import math

import jax
import jax.numpy as jnp
from jax import lax
from jax.experimental import pallas as pl
from jax.experimental.pallas import tpu as pltpu

EPS = 1e-6
NEG = -1e30
LANES = 128
HEAD_DIM = 128
ROPE_DIM = 64
ROPE_THETA = 10000.0
A_HEADS = 8
A_PATTERNS = ((128, 1), (512, 4), (2048, 16))
B_HEADS = 8
Q_LORA = 512
KV_LORA = 256
MLA_QK = 192
MLA_QK_PAD = 256
C_HEADS = 16
MOBA_BLOCK = 256
MOBA_TOPK = 3
N_EXPERTS = 16
N_GROUPS = 4
GROUP = N_EXPERTS // N_GROUPS
TOP_K = 2
EXPERT_FF = 512
A_QKV = 3 * A_HEADS * HEAD_DIM
EVEN_IN_PAD = 4096
KPE_COL = A_QKV + Q_LORA + KV_LORA
MIB = 1024 * 1024
FLASH_TQ = 1024
FLASH_CK = 512
ONES_ROWS = 16
ALIBI_COLS = 6
LOG2E = 1.4426950408889634
MOE_TM = 1024
MOE_FF_SPLIT = 2

_NT = (((1,), (1,)), ((), ()))


def _params(semantics, vmem_mib):
    return pltpu.CompilerParams(dimension_semantics=semantics,
                                vmem_limit_bytes=vmem_mib * MIB)


def _rms_scale(x, width):
    return lax.rsqrt(jnp.sum(x * x, axis=-1, keepdims=True) * (1.0 / width) + EPS)


def _norm_matmul_kernel(x_ref, g_ref, w_ref, o_ref, xn_ref, *, width):
    @pl.when(pl.program_id(1) == 0)
    def _():
        x = x_ref[...]
        xn_ref[...] = (x * _rms_scale(x, width) * g_ref[...]).astype(jnp.bfloat16)

    o_ref[...] = jnp.dot(xn_ref[...], w_ref[...],
                         preferred_element_type=jnp.float32).astype(o_ref.dtype)


def norm_matmul(x, g, w, *, col_block=0, tm=1024, tn=1024):
    n = x.shape[0]
    k, m = w.shape
    return pl.pallas_call(
        functools.partial(_norm_matmul_kernel, width=k),
        grid=(n // tm, m // tn),
        in_specs=[pl.BlockSpec((tm, k), lambda i, j: (i, col_block)),
                  pl.BlockSpec((1, k), lambda i, j: (0, 0)),
                  pl.BlockSpec((k, tn), lambda i, j: (0, j))],
        out_specs=pl.BlockSpec((tm, tn), lambda i, j: (i, j)),
        out_shape=jax.ShapeDtypeStruct((n, m), jnp.float32),
        scratch_shapes=[pltpu.VMEM((tm, k), jnp.bfloat16)],
        compiler_params=_params(("parallel", "arbitrary"), 48),
        name="norm_matmul",
    )(x, g.reshape(1, k), w)


def _matmul_res_kernel(*refs, n_a):
    a_refs, w_refs = refs[:n_a], refs[n_a:2 * n_a]
    r_ref, o_ref = refs[2 * n_a], refs[2 * n_a + 1]
    acc = r_ref[...]
    for a_ref, w_ref in zip(a_refs, w_refs):
        acc = acc + jnp.dot(a_ref[...], w_ref[...], preferred_element_type=jnp.float32)
    o_ref[...] = acc


def matmul_residual(a_list, w, res, *, tm=512, tn=1024):
    n, m = res.shape
    n_a = len(a_list)
    ka = a_list[0].shape[1]
    in_specs = [pl.BlockSpec((tm, ka), lambda i, j: (i, 0)) for _ in a_list]
    in_specs += [pl.BlockSpec((ka, tn), functools.partial(lambda i, j, p: (p, j), p=p))
                 for p in range(n_a)]
    in_specs += [pl.BlockSpec((tm, tn), lambda i, j: (i, j))]
    return pl.pallas_call(
        functools.partial(_matmul_res_kernel, n_a=n_a),
        grid=(n // tm, m // tn),
        in_specs=in_specs,
        out_specs=pl.BlockSpec((tm, tn), lambda i, j: (i, j)),
        out_shape=jax.ShapeDtypeStruct((n, m), jnp.float32),
        compiler_params=_params(("parallel", "parallel"), 40),
        name="matmul_residual",
    )(*a_list, *([w] * n_a), res)


def _prep_a_kernel(q_ref, k_ref, v_ref, gq_ref, gk_ref, *refs, scale):
    n_pat = len(A_PATTERNS)
    outs, (qs, ks, vs) = refs[:3 * n_pat], refs[3 * n_pat:]
    heads = [slice(h * HEAD_DIM, (h + 1) * HEAD_DIM) for h in range(A_HEADS)]
    for h, sl in enumerate(heads):
        q = q_ref[:, sl]
        qs[h] = q * _rms_scale(q, HEAD_DIM) * gq_ref[...] * scale
        k = k_ref[:, sl]
        ks[h] = k * _rms_scale(k, HEAD_DIM) * gk_ref[...]
        vs[h] = v_ref[:, sl]
    rows = qs.shape[1]
    for p, (_, dil) in enumerate(A_PATTERNS):
        for src, dst in zip((qs, ks, vs), outs[3 * p:3 * p + 3]):
            for r in range(dil):
                for h, sl in enumerate(heads):
                    dst[0, r, :, sl] = src[h, pl.ds(r, rows // dil, stride=dil), :].astype(
                        jnp.bfloat16)


def prep_a(proj, gq, gk, *, batch, seq, tm=512):
    n = proj.shape[0]
    w = A_HEADS * HEAD_DIM
    per_seq = seq // tm
    blk = lambda c: pl.BlockSpec((tm, w), functools.partial(lambda i, c: (i, c), c=c))
    gspec = pl.BlockSpec((1, HEAD_DIM), lambda i: (0, 0))
    out_specs, out_shape = [], []
    for _, dil in A_PATTERNS:
        spec = pl.BlockSpec((1, dil, tm // dil, w), lambda i: (i // per_seq, 0, i % per_seq, 0))
        shape = jax.ShapeDtypeStruct((batch, dil, seq // dil, w), jnp.bfloat16)
        out_specs += [spec] * 3
        out_shape += [shape] * 3
    outs = pl.pallas_call(
        functools.partial(_prep_a_kernel, scale=HEAD_DIM ** -0.5),
        grid=(n // tm,),
        in_specs=[blk(0), blk(1), blk(2), gspec, gspec],
        out_specs=out_specs,
        out_shape=out_shape,
        scratch_shapes=[pltpu.VMEM((A_HEADS, tm, HEAD_DIM), jnp.float32)] * 3,
        compiler_params=_params(("parallel",), 48),
        name="prep_a",
    )(proj, proj, proj, gq.reshape(1, -1), gk.reshape(1, -1))
    return [outs[3 * p:3 * p + 3] for p in range(len(A_PATTERNS))]


def _dilated_kernel(q_ref, kc_ref, kp_ref, vc_ref, vp_ref, o_ref, lse_ref, *, dil):
    band = HEAD_DIM
    n = pl.program_id(2)
    qi = lax.broadcasted_iota(jnp.int32, (band, band), 0)
    ki = lax.broadcasted_iota(jnp.int32, (band, band), 1)
    dist_c = (qi - ki).astype(jnp.float32)
    dist_p = dist_c + float(band)
    valid_c = qi >= ki
    valid_p = jnp.logical_and(ki >= qi, n > 0)
    lse_tile = jnp.zeros((band, LANES), jnp.float32)
    for h in range(A_HEADS):
        sl = slice(h * HEAD_DIM, (h + 1) * HEAD_DIM)
        slope = 2.0 ** (-8.0 * (h + 1) / A_HEADS) * dil
        q = q_ref[0, 0, :, sl]
        sc = lax.dot_general(q, kc_ref[0, 0, :, sl], _NT, preferred_element_type=jnp.float32)
        sp = lax.dot_general(q, kp_ref[0, 0, :, sl], _NT, preferred_element_type=jnp.float32)
        sc = jnp.where(valid_c, sc - slope * dist_c, NEG)
        sp = jnp.where(valid_p, sp - slope * dist_p, NEG)
        m = jnp.maximum(jnp.max(sc, axis=1, keepdims=True), jnp.max(sp, axis=1, keepdims=True))
        pc = jnp.exp(sc - m)
        pp = jnp.exp(sp - m)
        l = jnp.sum(pc, axis=1, keepdims=True) + jnp.sum(pp, axis=1, keepdims=True)
        acc = jnp.dot(pc.astype(jnp.bfloat16), vc_ref[0, 0, :, sl],
                      preferred_element_type=jnp.float32)
        acc = acc + jnp.dot(pp.astype(jnp.bfloat16), vp_ref[0, 0, :, sl],
                            preferred_element_type=jnp.float32)
        o_ref[0, 0, :, sl] = acc / l
        lse_tile = jnp.where(ki == h, m + jnp.log(l), lse_tile)
    lse_ref[0, 0] = lse_tile


def dilated_pattern(q, k, v, *, dil):
    batch, _, sub, w = q.shape
    band = HEAD_DIM
    cur = pl.BlockSpec((1, 1, band, w), lambda b, r, n: (b, r, n, 0))
    prev = pl.BlockSpec((1, 1, band, w), lambda b, r, n: (b, r, jnp.maximum(n - 1, 0), 0))
    return pl.pallas_call(
        functools.partial(_dilated_kernel, dil=dil),
        grid=(batch, dil, sub // band),
        in_specs=[cur, cur, prev, cur, prev],
        out_specs=[cur, pl.BlockSpec((1, 1, band, LANES), lambda b, r, n: (b, r, n, 0))],
        out_shape=[jax.ShapeDtypeStruct((batch, dil, sub, w), jnp.float32),
                   jax.ShapeDtypeStruct((batch, dil, sub, LANES), jnp.float32)],
        compiler_params=_params(("parallel", "parallel", "arbitrary"), 40),
        name=f"dilated_d{dil}",
    )(q, k, k, v, v)


def _dilated_merge_kernel(*refs):
    n_pat = len(A_PATTERNS)
    ins, out_ref, scratch = refs[:2 * n_pat], refs[2 * n_pat], refs[2 * n_pat + 1:]
    rows = out_ref.shape[0]
    heads = [slice(h * HEAD_DIM, (h + 1) * HEAD_DIM) for h in range(A_HEADS)]
    outs, lses = [], []
    for p, (_, dil) in enumerate(A_PATTERNS):
        o_ref, l_ref = ins[2 * p], ins[2 * p + 1]
        if dil == 1:
            outs.append(lambda h, o_ref=o_ref: o_ref[0, 0, :, heads[h]])
            lses.append(l_ref[0, 0])
            continue
        o_nat, l_nat = scratch[2 * p], scratch[2 * p + 1]
        for r in range(dil):
            for h, sl in enumerate(heads):
                o_nat[h, pl.ds(r, rows // dil, stride=dil), :] = o_ref[0, r, :, sl]
            l_nat[pl.ds(r, rows // dil, stride=dil), :] = l_ref[0, r]
        outs.append(lambda h, o_nat=o_nat: o_nat[h])
        lses.append(l_nat[...])
    for h, sl in enumerate(heads):
        lse_h = [l[:, h:h + 1] for l in lses]
        top = functools.reduce(jnp.maximum, lse_h)
        wts = [jnp.exp(l - top) for l in lse_h]
        num = sum(wt * o(h) for wt, o in zip(wts, outs))
        out_ref[:, sl] = (num / sum(wts)).astype(out_ref.dtype)


def dilated_attention(qkv_by_pattern, *, batch, seq, tm=512):
    w = A_HEADS * HEAD_DIM
    per_seq = seq // tm
    ins, in_specs, scratch = [], [], []
    for (q, k, v), (window, dil) in zip(qkv_by_pattern, A_PATTERNS):
        assert window // dil == HEAD_DIM
        o, lse = dilated_pattern(q, k, v, dil=dil)
        ins += [o, lse]
        in_specs += [pl.BlockSpec((1, dil, tm // dil, w), lambda i: (i // per_seq, 0, i % per_seq, 0)),
                     pl.BlockSpec((1, dil, tm // dil, LANES),
                                  lambda i: (i // per_seq, 0, i % per_seq, 0))]
        scratch += [pltpu.VMEM((A_HEADS, tm, HEAD_DIM), jnp.float32),
                    pltpu.VMEM((tm, LANES), jnp.float32)]
    n = batch * seq
    return pl.pallas_call(
        _dilated_merge_kernel,
        grid=(n // tm,),
        in_specs=in_specs,
        out_specs=pl.BlockSpec((tm, w), lambda i: (i, 0)),
        out_shape=jax.ShapeDtypeStruct((n, w), jnp.bfloat16),
        scratch_shapes=scratch,
        compiler_params=_params(("parallel",), 48),
        name="dilated_merge",
    )(*ins)


def _rope(t, cos, sin_lo, sin_hi):
    return (t * cos + pltpu.roll(t, LANES - ROPE_DIM // 2, 1) * sin_lo
            + pltpu.roll(t, ROPE_DIM // 2, 1) * sin_hi)


def _prep_b_kernel(qb_ref, ckv_ref, kpe_ref, gq_ref, gkn_ref, gkp_ref, cos_ref, slo_ref, shi_ref,
                   qo_ref, ko_ref, vo_ref, *, scale):
    cos, slo, shi = cos_ref[...], slo_ref[...], shi_ref[...]
    kpe = kpe_ref[...]
    kpe_ss = jnp.sum(kpe * kpe, axis=-1, keepdims=True)
    kpe_rot = _rope(kpe * gkp_ref[...], cos, slo, shi)
    for h in range(B_HEADS):
        lo = slice(h * MLA_QK_PAD, h * MLA_QK_PAD + HEAD_DIM)
        hi = slice(h * MLA_QK_PAD + HEAD_DIM, (h + 1) * MLA_QK_PAD)
        q_n, q_p = qb_ref[:, lo], qb_ref[:, hi]
        ss = jnp.sum(q_n * q_n, axis=-1, keepdims=True) + jnp.sum(q_p * q_p, axis=-1, keepdims=True)
        rq = lax.rsqrt(ss * (1.0 / MLA_QK) + EPS) * scale
        qo_ref[h, :, :HEAD_DIM] = (q_n * rq * gq_ref[:, :HEAD_DIM]).astype(jnp.bfloat16)
        qo_ref[h, :, HEAD_DIM:] = _rope(q_p * rq * gq_ref[:, HEAD_DIM:], cos, slo, shi
                                        ).astype(jnp.bfloat16)
        k_n = ckv_ref[:, lo]
        ss = jnp.sum(k_n * k_n, axis=-1, keepdims=True) + kpe_ss
        rk = lax.rsqrt(ss * (1.0 / MLA_QK) + EPS)
        ko_ref[h, :, :HEAD_DIM] = (k_n * rk * gkn_ref[...]).astype(jnp.bfloat16)
        ko_ref[h, :, HEAD_DIM:] = (kpe_rot * rk).astype(jnp.bfloat16)
        vo_ref[h, :HEAD_DIM, :] = ckv_ref[:, hi].T.astype(jnp.bfloat16)
        vo_ref[h, HEAD_DIM:, :] = jnp.ones((ONES_ROWS, ckv_ref.shape[0]), jnp.bfloat16)


def prep_b(qb, ckv, proj, gq_pad, gk_nope, gk_pe_pad, rope_tabs, *, seq, tm=256):
    n = qb.shape[0]
    wide = B_HEADS * MLA_QK_PAD
    nblk = seq // tm
    row = lambda c: pl.BlockSpec((tm, wide), lambda i: (i, 0))
    tab = pl.BlockSpec((tm, LANES), lambda i: (i % nblk, 0))
    return pl.pallas_call(
        functools.partial(_prep_b_kernel, scale=MLA_QK ** -0.5 * LOG2E),
        grid=(n // tm,),
        in_specs=[row(0), row(0),
                  pl.BlockSpec((tm, LANES), lambda i: (i, KPE_COL // LANES)),
                  pl.BlockSpec((1, MLA_QK_PAD), lambda i: (0, 0)),
                  pl.BlockSpec((1, LANES), lambda i: (0, 0)),
                  pl.BlockSpec((1, LANES), lambda i: (0, 0)),
                  tab, tab, tab],
        out_specs=[pl.BlockSpec((B_HEADS, tm, MLA_QK_PAD), lambda i: (0, i, 0)),
                   pl.BlockSpec((B_HEADS, tm, MLA_QK_PAD), lambda i: (0, i, 0)),
                   pl.BlockSpec((B_HEADS, HEAD_DIM + ONES_ROWS, tm), lambda i: (0, 0, i))],
        out_shape=[jax.ShapeDtypeStruct((B_HEADS, n, MLA_QK_PAD), jnp.bfloat16),
                   jax.ShapeDtypeStruct((B_HEADS, n, MLA_QK_PAD), jnp.bfloat16),
                   jax.ShapeDtypeStruct((B_HEADS, HEAD_DIM + ONES_ROWS, n), jnp.bfloat16)],
        compiler_params=_params(("parallel",), 40),
        name="prep_b",
    )(qb, ckv, proj, gq_pad, gk_nope, gk_pe_pad, *rope_tabs)


def rope_tables(seq):
    half = ROPE_DIM // 2
    inv = ROPE_THETA ** (-jnp.arange(0, ROPE_DIM, 2, dtype=jnp.float32) / ROPE_DIM)
    ang = jnp.arange(seq, dtype=jnp.float32)[:, None] * inv[None, :]
    cos, sin = jnp.cos(ang), jnp.sin(ang)
    z = jnp.zeros((seq, half), jnp.float32)
    pad = jnp.zeros((seq, LANES - ROPE_DIM), jnp.float32)
    return (jnp.concatenate([cos, cos, pad], axis=1),
            jnp.concatenate([-sin, z, pad], axis=1),
            jnp.concatenate([z, sin, pad], axis=1))


def _flash_kernel(q_ref, k_ref, vt_ref, o_ref, m_ref, acc_ref, *, tq, ck, dv):
    i = pl.program_id(2)
    nb = tq // ck
    n_past = i * nb
    m_ref[...] = jnp.full_like(m_ref, NEG)
    acc_ref[...] = jnp.zeros_like(acc_ref)

    def scores(j, r0):
        start = pl.multiple_of(j * ck, ck)
        return lax.dot_general(k_ref[0, pl.ds(start, ck), :], q_ref[0, r0:, :], _NT,
                               preferred_element_type=jnp.float32)

    def fold(j, r0, s):
        start = pl.multiple_of(j * ck, ck)
        probs, alphas = [], []
        for c0 in range(0, tq - r0, LANES):
            cols = slice(r0 + c0, r0 + c0 + LANES)
            s_c = s[:, c0:c0 + LANES]
            m_prev = m_ref[:, cols]
            m_new = jnp.maximum(m_prev, jnp.max(s_c, axis=0, keepdims=True))
            m_ref[:, cols] = m_new
            alphas.append(jnp.exp2(m_prev - m_new))
            probs.append(jnp.exp2(s_c - m_new).astype(jnp.bfloat16))
        p = jnp.concatenate(probs, axis=1)
        alpha = jnp.concatenate(alphas, axis=1)
        acc_ref[:, r0:] = alpha * acc_ref[:, r0:] + jnp.dot(
            vt_ref[0, :, pl.ds(start, ck)], p, preferred_element_type=jnp.float32)

    def past(j, s):
        s_next = scores(j + 1, 0)
        fold(j, 0, s)
        return s_next

    s = lax.fori_loop(0, n_past, past, scores(0, 0))
    for jb in range(nb):
        j = n_past + jb
        r0 = jb * ck
        if jb > 0:
            s = scores(j, r0)
        ki = lax.broadcasted_iota(jnp.int32, (ck, ck), 0)
        qi = lax.broadcasted_iota(jnp.int32, (ck, ck), 1)
        own = jnp.where(qi >= ki, s[:, :ck], NEG)
        fold(j, r0, own if tq - r0 == ck else jnp.concatenate([own, s[:, ck:]], axis=1))
    acc = acc_ref[...]
    o_ref[...] = (acc[:dv] / acc[dv:dv + 1]).T.astype(o_ref.dtype)


def flash_attention(q, k, vt, *, batch, seq, tq, ck):
    heads, n, dq = q.shape
    dvp = vt.shape[1]
    dv = dvp - ONES_ROWS
    nq = seq // tq
    return pl.pallas_call(
        functools.partial(_flash_kernel, tq=tq, ck=ck, dv=dv),
        grid=(batch, heads, nq),
        in_specs=[pl.BlockSpec((1, tq, dq), lambda b, h, i: (h, b * nq + i, 0)),
                  pl.BlockSpec((1, seq, dq), lambda b, h, i: (h * batch + b, 0, 0)),
                  pl.BlockSpec((1, dvp, seq), lambda b, h, i: (h, 0, b))],
        out_specs=pl.BlockSpec((tq, dv), lambda b, h, i: (b * nq + i, h)),
        out_shape=jax.ShapeDtypeStruct((n, heads * dv), jnp.bfloat16),
        scratch_shapes=[pltpu.VMEM((1, tq), jnp.float32),
                        pltpu.VMEM((dvp, tq), jnp.float32)],
        compiler_params=_params(("parallel", "parallel", "arbitrary"), 48),
        name=f"flash_h{heads}",
    )(q, k.reshape(heads * batch, seq, dq), vt)


def _prep_c_kernel(q_ref, k_ref, v_ref, gq_ref, gk_ref, alibi_ref, qo_ref, ko_ref, vo_ref,
                   kmean_ref, *, blocks_per_seq, scale):
    i = pl.program_id(0)
    bq = i % blocks_per_seq

    @pl.when(bq == 0)
    def _():
        kmean_ref[...] = jnp.zeros_like(kmean_ref)

    rows = q_ref.shape[0]
    nblk = blocks_per_seq
    blk = lax.broadcasted_iota(jnp.int32, (nblk, rows), 0)
    lane = lax.broadcasted_iota(jnp.int32, (rows, LANES), 1)
    kpos = bq * rows + lax.broadcasted_iota(jnp.int32, (rows, LANES), 0)
    pos_part = jnp.where((lane - nblk) % 2 == 0, jnp.right_shift(kpos, 7), kpos & (LANES - 1))
    k_extra = jnp.where(lane < nblk, (lane == bq).astype(jnp.float32),
                        jnp.where(lane < nblk + ALIBI_COLS, pos_part.astype(jnp.float32), 0.0))
    k_extra = k_extra.astype(jnp.bfloat16)
    ones = jnp.ones((ONES_ROWS, rows), jnp.bfloat16)
    for h in range(C_HEADS):
        sl = slice(h * HEAD_DIM, (h + 1) * HEAD_DIM)
        q = q_ref[:, sl]
        qn = q * _rms_scale(q, HEAD_DIM) * gq_ref[...]
        k = k_ref[:, sl]
        kn = k * _rms_scale(k, HEAD_DIM) * gk_ref[...]
        qo_ref[h, :, :HEAD_DIM] = (qn * scale).astype(jnp.bfloat16)
        ko_ref[h, :, :HEAD_DIM] = kn.astype(jnp.bfloat16)
        ko_ref[h, :, HEAD_DIM:] = k_extra
        vo_ref[h, :HEAD_DIM, :] = v_ref[:, sl].T.astype(jnp.bfloat16)
        vo_ref[h, HEAD_DIM:, :] = ones
        gate = lax.dot_general(kmean_ref[h], qn, _NT, precision=lax.Precision.HIGHEST,
                               preferred_element_type=jnp.float32)
        gate = jnp.where(blk < bq, gate, NEG)
        kmean_ref[h, pl.ds(bq, 1), :] = jnp.mean(kn, axis=0, keepdims=True)
        allowed = blk == bq
        for r in range(MOBA_TOPK):
            best = jnp.max(gate, axis=0, keepdims=True)
            idx = jnp.min(jnp.where(gate == best, blk, nblk), axis=0, keepdims=True)
            allowed = allowed | ((blk == idx) & (r < bq))
            gate = jnp.where(blk == idx, 2.0 * NEG, gate)
        block_bias = jnp.where(allowed, 0.0, NEG)
        block_bias = jnp.concatenate(
            [block_bias, jnp.zeros((LANES - nblk, rows), jnp.float32)], axis=0)
        qo_ref[h, :, HEAD_DIM:] = (block_bias.T + alibi_ref[h]).astype(jnp.bfloat16)


def prep_c(qkv, gq, gk, alibi_q, *, seq):
    n = qkv.shape[0]
    tm = MOBA_BLOCK
    w = C_HEADS * HEAD_DIM
    blocks_per_seq = seq // tm
    assert blocks_per_seq + ALIBI_COLS <= LANES
    blk = lambda c: pl.BlockSpec((tm, w), functools.partial(lambda i, c: (i, c), c=c))
    gspec = pl.BlockSpec((1, HEAD_DIM), lambda i: (0, 0))
    hm = pl.BlockSpec((C_HEADS, tm, 2 * HEAD_DIM), lambda i: (0, i, 0))
    hm_shape = jax.ShapeDtypeStruct((C_HEADS, n, 2 * HEAD_DIM), jnp.bfloat16)
    dvp = HEAD_DIM + ONES_ROWS
    return pl.pallas_call(
        functools.partial(_prep_c_kernel, blocks_per_seq=blocks_per_seq,
                          scale=HEAD_DIM ** -0.5 * LOG2E),
        grid=(n // tm,),
        in_specs=[blk(0), blk(1), blk(2), gspec, gspec,
                  pl.BlockSpec((C_HEADS, 1, LANES), lambda i: (0, 0, 0))],
        out_specs=[hm, hm, pl.BlockSpec((C_HEADS, dvp, tm), lambda i: (0, 0, i))],
        out_shape=[hm_shape, hm_shape, jax.ShapeDtypeStruct((C_HEADS, dvp, n), jnp.bfloat16)],
        scratch_shapes=[pltpu.VMEM((C_HEADS, blocks_per_seq, HEAD_DIM), jnp.float32)],
        compiler_params=_params(("arbitrary",), 40),
        name="prep_c",
    )(qkv, qkv, qkv, gq.reshape(1, -1), gk.reshape(1, -1), alibi_q)


def alibi_query_columns(blocks_per_seq):
    sigma = 2.0 ** (-8.0 * jnp.arange(1, C_HEADS + 1, dtype=jnp.float32) / C_HEADS) * LOG2E
    pieces = []
    rest = sigma
    for _ in range(ALIBI_COLS // 2):
        piece = rest.astype(jnp.bfloat16).astype(jnp.float32)
        pieces += [piece * float(LANES), piece]
        rest = rest - piece
    cols = jnp.stack(pieces, axis=1)
    cols = jnp.pad(cols, ((0, 0), (blocks_per_seq, LANES - blocks_per_seq - ALIBI_COLS)))
    return cols.reshape(C_HEADS, 1, LANES)


def _router_kernel(h_ref, g_ref, wr_ref, b_ref, gate_ref, grp_ref):
    x = h_ref[...]
    t = x * _rms_scale(x, x.shape[1]) * g_ref[...]
    logits = lax.dot_general(wr_ref[...], t, _NT, precision=lax.Precision.HIGHEST,
                             preferred_element_type=jnp.float32)
    scores = 1.0 / (1.0 + jnp.exp(-logits))
    biased = scores + b_ref[...]
    s = [scores[e:e + 1, :] for e in range(N_EXPERTS)]
    b = [biased[e:e + 1, :] for e in range(N_EXPERTS)]
    best_val, best_grp = None, None
    for g in range(N_GROUPS):
        mem = b[g * GROUP:(g + 1) * GROUP]
        top2 = None
        for x1 in range(GROUP):
            for x2 in range(x1 + 1, GROUP):
                pair = mem[x1] + mem[x2]
                top2 = pair if top2 is None else jnp.maximum(top2, pair)
        if g == 0:
            best_val, best_grp = top2, jnp.zeros_like(top2, dtype=jnp.int32)
        else:
            better = top2 > best_val
            best_grp = jnp.where(better, g, best_grp)
            best_val = jnp.where(better, top2, best_val)
    picked = []
    for e in range(N_EXPERTS):
        g = e // GROUP
        ahead = jnp.zeros_like(best_grp)
        for o in range(g * GROUP, (g + 1) * GROUP):
            if o == e:
                continue
            wins = (b[o] > b[e]) | ((b[o] == b[e]) & (o < e))
            ahead = ahead + wins.astype(jnp.int32)
        picked.append(jnp.where((best_grp == g) & (ahead < TOP_K), s[e], 0.0))
    total = picked[0]
    for e in range(1, N_EXPERTS):
        total = total + picked[e]
    for e in range(N_EXPERTS):
        gate_ref[e:e + 1, :] = picked[e] / total
    grp_ref[...] = best_grp


def router(h, g, w_router_t, bias, *, tm=512):
    n, d = h.shape
    return pl.pallas_call(
        _router_kernel,
        grid=(n // tm,),
        in_specs=[pl.BlockSpec((tm, d), lambda i: (i, 0)),
                  pl.BlockSpec((1, d), lambda i: (0, 0)),
                  pl.BlockSpec((N_EXPERTS, d), lambda i: (0, 0)),
                  pl.BlockSpec((N_EXPERTS, 1), lambda i: (0, 0))],
        out_specs=[pl.BlockSpec((N_EXPERTS, tm), lambda i: (0, i)),
                   pl.BlockSpec((1, tm), lambda i: (0, i))],
        out_shape=[jax.ShapeDtypeStruct((N_EXPERTS, n), jnp.float32),
                   jax.ShapeDtypeStruct((1, n), jnp.int32)],
        compiler_params=_params(("parallel",), 40),
        name="router",
    )(h, g.reshape(1, d), w_router_t, bias.reshape(N_EXPERTS, 1))


def _row_copies(tok_ref, tile, rows, hbm_ref, buf_ref, sem_ref, to_hbm, wait):
    slot = tile % 2

    def step(r, carry):
        hbm_row = hbm_ref.at[pl.ds(tok_ref[tile * rows + r], 1), :]
        buf_row = buf_ref.at[slot, pl.ds(r, 1), :]
        src, dst = (buf_row, hbm_row) if to_hbm else (hbm_row, buf_row)
        copy = pltpu.make_async_copy(src, dst, sem_ref.at[slot])
        if wait:
            copy.wait()
        else:
            copy.start()
        return carry

    lax.fori_loop(0, rows, step, 0, unroll=8)


def _moe_kernel(tile_ref, grp_ref, flag_ref, tok_ref, h_hbm, g_ref, gate_ref, wg_ref, wu_ref, wd_ref,
                out_hbm, hbuf, tbuf, gsem, ssem, *, tm, halves, n_tiles):
    j = pl.program_id(0)
    sub = pl.program_id(1)
    flag = flag_ref[j]
    tile = tile_ref[j]
    slot = tile % 2
    gather = functools.partial(_row_copies, tok_ref, rows=tm, hbm_ref=h_hbm, buf_ref=hbuf,
                               sem_ref=gsem, to_hbm=False)
    scatter = functools.partial(_row_copies, tok_ref, rows=tm, hbm_ref=out_hbm, buf_ref=hbuf,
                                sem_ref=ssem, to_hbm=True)

    @pl.when((sub == 0) & ((flag & 2) != 0))
    def _():
        @pl.when(tile == 0)
        def _():
            gather(tile, wait=False)

        gather(tile, wait=True)
        x = hbuf[slot]
        tbuf[...] = (x * _rms_scale(x, x.shape[1]) * g_ref[...]).astype(jnp.bfloat16)

        @pl.when(tile >= 1)
        def _():
            scatter(tile - 1, wait=True)

        @pl.when(tile + 1 < n_tiles)
        def _():
            gather(tile + 1, wait=False)

    @pl.when((flag & 1) != 0)
    def _():
        e = grp_ref[j] * GROUP + sub // halves
        t = tbuf[...]
        a = jnp.dot(t, wg_ref[0, 0].astype(jnp.bfloat16), preferred_element_type=jnp.float32)
        u = jnp.dot(t, wu_ref[0, 0].astype(jnp.bfloat16), preferred_element_type=jnp.float32)
        hid = (a / (1.0 + jnp.exp(-a))) * u
        y = jnp.dot(hid.astype(jnp.bfloat16), wd_ref[0, 0].astype(jnp.bfloat16),
                    preferred_element_type=jnp.float32)
        gates = gate_ref[...]
        lane = lax.broadcasted_iota(jnp.int32, gates.shape, 1)
        g_col = jnp.sum(jnp.where(lane == e, gates, 0.0), axis=1, keepdims=True)
        hbuf[slot] += g_col * y

    @pl.when((sub == pl.num_programs(1) - 1) & ((flag & 4) != 0))
    def _():
        scatter(tile, wait=False)

        @pl.when(tile == n_tiles - 1)
        def _():
            scatter(tile, wait=True)


def _moe_worklist(grp, tm):
    n = grp.shape[0]
    n_tiles = n // tm
    order = jnp.argsort(grp, stable=True).astype(jnp.int32)
    counts = jnp.sum(grp[None, :] == jnp.arange(N_GROUPS, dtype=jnp.int32)[:, None], axis=1)
    ends = jnp.cumsum(counts)
    first_slot = jnp.arange(n_tiles, dtype=jnp.int32) * tm
    g_lo = jnp.sum(ends[None, :] <= first_slot[:, None], axis=1)
    g_hi = jnp.sum(ends[None, :] <= (first_slot + tm - 1)[:, None], axis=1)
    cnt = g_hi - g_lo + 1
    first_item = jnp.cumsum(cnt) - cnt
    total = first_item[-1] + cnt[-1]
    n_items = n_tiles + N_GROUPS - 1
    item = jnp.arange(n_items, dtype=jnp.int32)
    tile = jnp.minimum(jnp.sum((first_item + cnt)[None, :] <= item[:, None], axis=1), n_tiles - 1)
    valid = item < total
    grp_of = jnp.where(valid, g_lo[tile] + item - first_item[tile], g_hi[n_tiles - 1])
    is_first = valid & (item == first_item[tile])
    is_last = valid & (item == first_item[tile] + cnt[tile] - 1)
    flag = valid.astype(jnp.int32) + 2 * is_first.astype(jnp.int32) + 4 * is_last.astype(jnp.int32)
    return order, tile.astype(jnp.int32), grp_of.astype(jnp.int32), flag


def moe_grouped(h, ffn_g, gate_t, grp, wg, wu, wd, *, layer, tm=MOE_TM, halves=MOE_FF_SPLIT):
    n, d = h.shape
    ff = wg.shape[3] // halves
    order, tile, grp_of, flag = _moe_worklist(grp.reshape(n), tm)
    gates_sorted = jnp.take(gate_t.T, order, axis=0)
    expert = lambda j, sub, tile_ref, grp_ref, flag_ref, tok_ref: grp_ref[j] * GROUP + sub // halves
    grid_spec = pltpu.PrefetchScalarGridSpec(
        num_scalar_prefetch=4,
        grid=(tile.shape[0], GROUP * halves),
        in_specs=[pl.BlockSpec(memory_space=pl.ANY),
                  pl.BlockSpec((1, d), lambda j, sub, *_: (0, 0)),
                  pl.BlockSpec((tm, N_EXPERTS), lambda j, sub, tile_ref, *_: (tile_ref[j], 0)),
                  pl.BlockSpec((1, 1, d, ff),
                               lambda j, sub, *p: (layer, expert(j, sub, *p), 0, sub % halves)),
                  pl.BlockSpec((1, 1, d, ff),
                               lambda j, sub, *p: (layer, expert(j, sub, *p), 0, sub % halves)),
                  pl.BlockSpec((1, 1, ff, d),
                               lambda j, sub, *p: (layer, expert(j, sub, *p), sub % halves, 0))],
        out_specs=pl.BlockSpec(memory_space=pl.ANY),
        scratch_shapes=[pltpu.VMEM((2, tm, d), jnp.float32),
                        pltpu.VMEM((tm, d), jnp.bfloat16),
                        pltpu.SemaphoreType.DMA((2,)),
                        pltpu.SemaphoreType.DMA((2,))])
    return pl.pallas_call(
        functools.partial(_moe_kernel, tm=tm, halves=halves, n_tiles=n // tm),
        grid_spec=grid_spec,
        out_shape=jax.ShapeDtypeStruct((n, d), jnp.float32),
        compiler_params=_params(("arbitrary", "arbitrary"), 56),
        name="moe_grouped",
    )(tile, grp_of, flag, order, h, ffn_g.reshape(1, d), gates_sorted, wg, wu, wd)


def moe_layer(h, ffn_g, w_router_t, router_bias, wg, wu, wd, *, layer):
    gate_t, grp = router(h, ffn_g, w_router_t, router_bias)
    return moe_grouped(h, ffn_g, gate_t, grp, wg, wu, wd, layer=layer)


def _pad_cols(w, width):
    return jnp.pad(w, ((0, 0), (0, width - w.shape[1])))


def _pad_heads(w, heads, real, padded):
    k = w.shape[0]
    w = w.reshape(k, heads, real)
    return jnp.pad(w, ((0, 0), (0, 0), (0, padded - real))).reshape(k, heads * padded)


def kernel(x, attn_norm, ev_w_in, ev_q_a_norm, ev_w_q_b, ev_kv_a_norm, ev_w_kv_b, ev_qn_a, ev_kn_a, ev_qn_b, ev_kn_b, ev_w_o, od_w_qkv, od_qn, od_kn, od_w_o, ffn_norm, w_router, router_bias, w_gate, w_up, w_down):
    batch, seq, d = x.shape
    n = batch * seq
    bf = jnp.bfloat16
    h = x.reshape(n, d)
    w_router_t = w_router.T
    depth = attn_norm.shape[0]
    for layer in range(depth):
        i = layer // 2
        if layer % 2 == 0:
            w_in = _pad_cols(ev_w_in[i], EVEN_IN_PAD).astype(bf)
            proj = norm_matmul(h, attn_norm[layer], w_in)
            qkv_a = prep_a(proj, ev_qn_a[i], ev_kn_a[i], batch=batch, seq=seq)
            out_a = dilated_attention(qkv_a, batch=batch, seq=seq)
            w_qb = _pad_heads(ev_w_q_b[i], B_HEADS, MLA_QK, MLA_QK_PAD).astype(bf)
            qb = norm_matmul(proj, ev_q_a_norm[i], w_qb, col_block=A_QKV // Q_LORA)
            ckv = norm_matmul(proj, ev_kv_a_norm[i], ev_w_kv_b[i].astype(bf),
                              col_block=(A_QKV + Q_LORA) // KV_LORA)
            gq_pad = jnp.pad(ev_qn_b[i], (0, MLA_QK_PAD - MLA_QK)).reshape(1, -1)
            gk_nope = ev_kn_b[i][:HEAD_DIM].reshape(1, -1)
            gk_pe = jnp.pad(ev_kn_b[i][HEAD_DIM:], (0, LANES - ROPE_DIM)).reshape(1, -1)
            q_b, k_b, v_b = prep_b(qb, ckv, proj, gq_pad, gk_nope, gk_pe, rope_tables(seq), seq=seq)
            out_b = flash_attention(q_b, k_b, v_b, batch=batch, seq=seq,
                                    tq=min(FLASH_TQ, seq), ck=FLASH_CK)
            h = matmul_residual([out_a, out_b], ev_w_o[i].astype(bf), h)
        else:
            qkv = norm_matmul(h, attn_norm[layer], od_w_qkv[i].astype(bf))
            q_c, k_c, v_c = prep_c(qkv, od_qn[i], od_kn[i],
                                   alibi_query_columns(seq // MOBA_BLOCK), seq=seq)
            out_c = flash_attention(q_c, k_c, v_c, batch=batch, seq=seq, tq=min(FLASH_TQ, seq),
                                    ck=FLASH_CK)
            h = matmul_residual([out_c], od_w_o[i].astype(bf), h)
        h = moe_layer(h, ffn_norm[layer], w_router_t, router_bias, w_gate, w_up, w_down, layer=layer)
    return h.reshape(batch, seq, d)
```

```python
import functools
import math

import jax
import jax.numpy as jnp
from jax import lax
from jax.experimental import pallas as pl
from jax.experimental.pallas import tpu as pltpu

EPS = 1e-6
NEG = -1e30
LANES = 128
HEAD_DIM = 128
ROPE_DIM = 64
ROPE_THETA = 10000.0
A_HEADS = 8
A_PATTERNS = ((128, 1), (512, 4), (2048, 16))
B_HEADS = 8
Q_LORA = 512
KV_LORA = 256
MLA_QK = 192
MLA_QK_PAD = 256
C_HEADS = 16
MOBA_BLOCK = 256
MOBA_TOPK = 3
N_EXPERTS = 16
N_GROUPS = 4
GROUP = N_EXPERTS // N_GROUPS
TOP_K = 2
EXPERT_FF = 512
A_QKV = 3 * A_HEADS * HEAD_DIM
EVEN_IN_PAD = 4096
KPE_COL = A_QKV + Q_LORA + KV_LORA
MIB = 1024 * 1024
FLASH_TQ = 1024
FLASH_CK = 512
ONES_ROWS = 16
ALIBI_COLS = 6
LOG2E = 1.4426950408889634
MOE_TM = 1024
MOE_FF_SPLIT = 2

_NT = (((1,), (1,)), ((), ()))


def _params(semantics, vmem_mib):
    return pltpu.CompilerParams(dimension_semantics=semantics,
                                vmem_limit_bytes=vmem_mib * MIB)


def _rms_scale(x, width):
    return lax.rsqrt(jnp.sum(x * x, axis=-1, keepdims=True) * (1.0 / width) + EPS)


def _norm_matmul_kernel(x_ref, g_ref, w_ref, o_ref, xn_ref, *, width):
    @pl.when(pl.program_id(1) == 0)
    def _():
        x = x_ref[...]
        xn_ref[...] = (x * _rms_scale(x, width) * g_ref[...]).astype(jnp.bfloat16)

    o_ref[...] = jnp.dot(xn_ref[...], w_ref[...],
                         preferred_element_type=jnp.float32).astype(o_ref.dtype)


def norm_matmul(x, g, w, *, col_block=0, tm=1024, tn=1024):
    n = x.shape[0]
    k, m = w.shape
    return pl.pallas_call(
        functools.partial(_norm_matmul_kernel, width=k),
        grid=(n // tm, m // tn),
        in_specs=[pl.BlockSpec((tm, k), lambda i, j: (i, col_block)),
                  pl.BlockSpec((1, k), lambda i, j: (0, 0)),
                  pl.BlockSpec((k, tn), lambda i, j: (0, j))],
        out_specs=pl.BlockSpec((tm, tn), lambda i, j: (i, j)),
        out_shape=jax.ShapeDtypeStruct((n, m), jnp.float32),
        scratch_shapes=[pltpu.VMEM((tm, k), jnp.bfloat16)],
        compiler_params=_params(("parallel", "arbitrary"), 48),
        name="norm_matmul",
    )(x, g.reshape(1, k), w)


def _matmul_res_kernel(*refs, n_a):
    a_refs, w_refs = refs[:n_a], refs[n_a:2 * n_a]
    r_ref, o_ref = refs[2 * n_a], refs[2 * n_a + 1]
    acc = r_ref[...]
    for a_ref, w_ref in zip(a_refs, w_refs):
        acc = acc + jnp.dot(a_ref[...], w_ref[...], preferred_element_type=jnp.float32)
    o_ref[...] = acc


def matmul_residual(a_list, w, res, *, tm=512, tn=1024):
    n, m = res.shape
    n_a = len(a_list)
    ka = a_list[0].shape[1]
    in_specs = [pl.BlockSpec((tm, ka), lambda i, j: (i, 0)) for _ in a_list]
    in_specs += [pl.BlockSpec((ka, tn), functools.partial(lambda i, j, p: (p, j), p=p))
                 for p in range(n_a)]
    in_specs += [pl.BlockSpec((tm, tn), lambda i, j: (i, j))]
    return pl.pallas_call(
        functools.partial(_matmul_res_kernel, n_a=n_a),
        grid=(n // tm, m // tn),
        in_specs=in_specs,
        out_specs=pl.BlockSpec((tm, tn), lambda i, j: (i, j)),
        out_shape=jax.ShapeDtypeStruct((n, m), jnp.float32),
        compiler_params=_params(("parallel", "parallel"), 40),
        name="matmul_residual",
    )(*a_list, *([w] * n_a), res)


def _prep_a_kernel(q_ref, k_ref, v_ref, gq_ref, gk_ref, *refs, scale):
    n_pat = len(A_PATTERNS)
    outs, (qs, ks, vs) = refs[:3 * n_pat], refs[3 * n_pat:]
    heads = [slice(h * HEAD_DIM, (h + 1) * HEAD_DIM) for h in range(A_HEADS)]
    for h, sl in enumerate(heads):
        q = q_ref[:, sl]
        qs[h] = q * _rms_scale(q, HEAD_DIM) * gq_ref[...] * scale
        k = k_ref[:, sl]
        ks[h] = k * _rms_scale(k, HEAD_DIM) * gk_ref[...]
        vs[h] = v_ref[:, sl]
    rows = qs.shape[1]
    for p, (_, dil) in enumerate(A_PATTERNS):
        for src, dst in zip((qs, ks, vs), outs[3 * p:3 * p + 3]):
            for r in range(dil):
                for h, sl in enumerate(heads):
                    dst[0, r, :, sl] = src[h, pl.ds(r, rows // dil, stride=dil), :].astype(
                        jnp.bfloat16)


def prep_a(proj, gq, gk, *, batch, seq, tm=512):
    n = proj.shape[0]
    w = A_HEADS * HEAD_DIM
    per_seq = seq // tm
    blk = lambda c: pl.BlockSpec((tm, w), functools.partial(lambda i, c: (i, c), c=c))
    gspec = pl.BlockSpec((1, HEAD_DIM), lambda i: (0, 0))
    out_specs, out_shape = [], []
    for _, dil in A_PATTERNS:
        spec = pl.BlockSpec((1, dil, tm // dil, w), lambda i: (i // per_seq, 0, i % per_seq, 0))
        shape = jax.ShapeDtypeStruct((batch, dil, seq // dil, w), jnp.bfloat16)
        out_specs += [spec] * 3
        out_shape += [shape] * 3
    outs = pl.pallas_call(
        functools.partial(_prep_a_kernel, scale=HEAD_DIM ** -0.5),
        grid=(n // tm,),
        in_specs=[blk(0), blk(1), blk(2), gspec, gspec],
        out_specs=out_specs,
        out_shape=out_shape,
        scratch_shapes=[pltpu.VMEM((A_HEADS, tm, HEAD_DIM), jnp.float32)] * 3,
        compiler_params=_params(("parallel",), 48),
        name="prep_a",
    )(proj, proj, proj, gq.reshape(1, -1), gk.reshape(1, -1))
    return [outs[3 * p:3 * p + 3] for p in range(len(A_PATTERNS))]


def _dilated_kernel(q_ref, kc_ref, kp_ref, vc_ref, vp_ref, o_ref, lse_ref, *, dil):
    band = HEAD_DIM
    n = pl.program_id(2)
    qi = lax.broadcasted_iota(jnp.int32, (band, band), 0)
    ki = lax.broadcasted_iota(jnp.int32, (band, band), 1)
    dist_c = (qi - ki).astype(jnp.float32)
    dist_p = dist_c + float(band)
    valid_c = qi >= ki
    valid_p = jnp.logical_and(ki >= qi, n > 0)
    lse_tile = jnp.zeros((band, LANES), jnp.float32)
    for h in range(A_HEADS):
        sl = slice(h * HEAD_DIM, (h + 1) * HEAD_DIM)
        slope = 2.0 ** (-8.0 * (h + 1) / A_HEADS) * dil
        q = q_ref[0, 0, :, sl]
        sc = lax.dot_general(q, kc_ref[0, 0, :, sl], _NT, preferred_element_type=jnp.float32)
        sp = lax.dot_general(q, kp_ref[0, 0, :, sl], _NT, preferred_element_type=jnp.float32)
        sc = jnp.where(valid_c, sc - slope * dist_c, NEG)
        sp = jnp.where(valid_p, sp - slope * dist_p, NEG)
        m = jnp.maximum(jnp.max(sc, axis=1, keepdims=True), jnp.max(sp, axis=1, keepdims=True))
        pc = jnp.exp(sc - m)
        pp = jnp.exp(sp - m)
        l = jnp.sum(pc, axis=1, keepdims=True) + jnp.sum(pp, axis=1, keepdims=True)
        acc = jnp.dot(pc.astype(jnp.bfloat16), vc_ref[0, 0, :, sl],
                      preferred_element_type=jnp.float32)
        acc = acc + jnp.dot(pp.astype(jnp.bfloat16), vp_ref[0, 0, :, sl],
                            preferred_element_type=jnp.float32)
        o_ref[0, 0, :, sl] = acc / l
        lse_tile = jnp.where(ki == h, m + jnp.log(l), lse_tile)
    lse_ref[0, 0] = lse_tile


def dilated_pattern(q, k, v, *, dil):
    batch, _, sub, w = q.shape
    band = HEAD_DIM
    cur = pl.BlockSpec((1, 1, band, w), lambda b, r, n: (b, r, n, 0))
    prev = pl.BlockSpec((1, 1, band, w), lambda b, r, n: (b, r, jnp.maximum(n - 1, 0), 0))
    return pl.pallas_call(
        functools.partial(_dilated_kernel, dil=dil),
        grid=(batch, dil, sub // band),
        in_specs=[cur, cur, prev, cur, prev],
        out_specs=[cur, pl.BlockSpec((1, 1, band, LANES), lambda b, r, n: (b, r, n, 0))],
        out_shape=[jax.ShapeDtypeStruct((batch, dil, sub, w), jnp.float32),
                   jax.ShapeDtypeStruct((batch, dil, sub, LANES), jnp.float32)],
        compiler_params=_params(("parallel", "parallel", "arbitrary"), 40),
        name=f"dilated_d{dil}",
    )(q, k, k, v, v)


def _dilated_merge_kernel(*refs):
    n_pat = len(A_PATTERNS)
    ins, out_ref, scratch = refs[:2 * n_pat], refs[2 * n_pat], refs[2 * n_pat + 1:]
    rows = out_ref.shape[0]
    heads = [slice(h * HEAD_DIM, (h + 1) * HEAD_DIM) for h in range(A_HEADS)]
    outs, lses = [], []
    for p, (_, dil) in enumerate(A_PATTERNS):
        o_ref, l_ref = ins[2 * p], ins[2 * p + 1]
        if dil == 1:
            outs.append(lambda h, o_ref=o_ref: o_ref[0, 0, :, heads[h]])
            lses.append(l_ref[0, 0])
            continue
        o_nat, l_nat = scratch[2 * p], scratch[2 * p + 1]
        for r in range(dil):
            for h, sl in enumerate(heads):
                o_nat[h, pl.ds(r, rows // dil, stride=dil), :] = o_ref[0, r, :, sl]
            l_nat[pl.ds(r, rows // dil, stride=dil), :] = l_ref[0, r]
        outs.append(lambda h, o_nat=o_nat: o_nat[h])
        lses.append(l_nat[...])
    for h, sl in enumerate(heads):
        lse_h = [l[:, h:h + 1] for l in lses]
        top = functools.reduce(jnp.maximum, lse_h)
        wts = [jnp.exp(l - top) for l in lse_h]
        num = sum(wt * o(h) for wt, o in zip(wts, outs))
        out_ref[:, sl] = (num / sum(wts)).astype(out_ref.dtype)


def dilated_attention(qkv_by_pattern, *, batch, seq, tm=512):
    w = A_HEADS * HEAD_DIM
    per_seq = seq // tm
    ins, in_specs, scratch = [], [], []
    for (q, k, v), (window, dil) in zip(qkv_by_pattern, A_PATTERNS):
        assert window // dil == HEAD_DIM
        o, lse = dilated_pattern(q, k, v, dil=dil)
        ins += [o, lse]
        in_specs += [pl.BlockSpec((1, dil, tm // dil, w), lambda i: (i // per_seq, 0, i % per_seq, 0)),
                     pl.BlockSpec((1, dil, tm // dil, LANES),
                                  lambda i: (i // per_seq, 0, i % per_seq, 0))]
        scratch += [pltpu.VMEM((A_HEADS, tm, HEAD_DIM), jnp.float32),
                    pltpu.VMEM((tm, LANES), jnp.float32)]
    n = batch * seq
    return pl.pallas_call(
        _dilated_merge_kernel,
        grid=(n // tm,),
        in_specs=in_specs,
        out_specs=pl.BlockSpec((tm, w), lambda i: (i, 0)),
        out_shape=jax.ShapeDtypeStruct((n, w), jnp.bfloat16),
        scratch_shapes=scratch,
        compiler_params=_params(("parallel",), 48),
        name="dilated_merge",
    )(*ins)


def _rope(t, cos, sin_lo, sin_hi):
    return (t * cos + pltpu.roll(t, LANES - ROPE_DIM // 2, 1) * sin_lo
            + pltpu.roll(t, ROPE_DIM // 2, 1) * sin_hi)


def _prep_b_kernel(qb_ref, ckv_ref, kpe_ref, gq_ref, gkn_ref, gkp_ref, cos_ref, slo_ref, shi_ref,
                   qo_ref, ko_ref, vo_ref, *, scale):
    cos, slo, shi = cos_ref[...], slo_ref[...], shi_ref[...]
    kpe = kpe_ref[...]
    kpe_ss = jnp.sum(kpe * kpe, axis=-1, keepdims=True)
    kpe_rot = _rope(kpe * gkp_ref[...], cos, slo, shi)
    for h in range(B_HEADS):
        lo = slice(h * MLA_QK_PAD, h * MLA_QK_PAD + HEAD_DIM)
        hi = slice(h * MLA_QK_PAD + HEAD_DIM, (h + 1) * MLA_QK_PAD)
        q_n, q_p = qb_ref[:, lo], qb_ref[:, hi]
        ss = jnp.sum(q_n * q_n, axis=-1, keepdims=True) + jnp.sum(q_p * q_p, axis=-1, keepdims=True)
        rq = lax.rsqrt(ss * (1.0 / MLA_QK) + EPS) * scale
        qo_ref[h, :, :HEAD_DIM] = (q_n * rq * gq_ref[:, :HEAD_DIM]).astype(jnp.bfloat16)
        qo_ref[h, :, HEAD_DIM:] = _rope(q_p * rq * gq_ref[:, HEAD_DIM:], cos, slo, shi
                                        ).astype(jnp.bfloat16)
        k_n = ckv_ref[:, lo]
        ss = jnp.sum(k_n * k_n, axis=-1, keepdims=True) + kpe_ss
        rk = lax.rsqrt(ss * (1.0 / MLA_QK) + EPS)
        ko_ref[h, :, :HEAD_DIM] = (k_n * rk * gkn_ref[...]).astype(jnp.bfloat16)
        ko_ref[h, :, HEAD_DIM:] = (kpe_rot * rk).astype(jnp.bfloat16)
        vo_ref[h, :HEAD_DIM, :] = ckv_ref[:, hi].T.astype(jnp.bfloat16)
        vo_ref[h, HEAD_DIM:, :] = jnp.ones((ONES_ROWS, ckv_ref.shape[0]), jnp.bfloat16)


def prep_b(qb, ckv, proj, gq_pad, gk_nope, gk_pe_pad, rope_tabs, *, seq, tm=256):
    n = qb.shape[0]
    wide = B_HEADS * MLA_QK_PAD
    nblk = seq // tm
    row = lambda c: pl.BlockSpec((tm, wide), lambda i: (i, 0))
    tab = pl.BlockSpec((tm, LANES), lambda i: (i % nblk, 0))
    return pl.pallas_call(
        functools.partial(_prep_b_kernel, scale=MLA_QK ** -0.5 * LOG2E),
        grid=(n // tm,),
        in_specs=[row(0), row(0),
                  pl.BlockSpec((tm, LANES), lambda i: (i, KPE_COL // LANES)),
                  pl.BlockSpec((1, MLA_QK_PAD), lambda i: (0, 0)),
                  pl.BlockSpec((1, LANES), lambda i: (0, 0)),
                  pl.BlockSpec((1, LANES), lambda i: (0, 0)),
                  tab, tab, tab],
        out_specs=[pl.BlockSpec((B_HEADS, tm, MLA_QK_PAD), lambda i: (0, i, 0)),
                   pl.BlockSpec((B_HEADS, tm, MLA_QK_PAD), lambda i: (0, i, 0)),
                   pl.BlockSpec((B_HEADS, HEAD_DIM + ONES_ROWS, tm), lambda i: (0, 0, i))],
        out_shape=[jax.ShapeDtypeStruct((B_HEADS, n, MLA_QK_PAD), jnp.bfloat16),
                   jax.ShapeDtypeStruct((B_HEADS, n, MLA_QK_PAD), jnp.bfloat16),
                   jax.ShapeDtypeStruct((B_HEADS, HEAD_DIM + ONES_ROWS, n), jnp.bfloat16)],
        compiler_params=_params(("parallel",), 40),
        name="prep_b",
    )(qb, ckv, proj, gq_pad, gk_nope, gk_pe_pad, *rope_tabs)


def rope_tables(seq):
    half = ROPE_DIM // 2
    inv = ROPE_THETA ** (-jnp.arange(0, ROPE_DIM, 2, dtype=jnp.float32) / ROPE_DIM)
    ang = jnp.arange(seq, dtype=jnp.float32)[:, None] * inv[None, :]
    cos, sin = jnp.cos(ang), jnp.sin(ang)
    z = jnp.zeros((seq, half), jnp.float32)
    pad = jnp.zeros((seq, LANES - ROPE_DIM), jnp.float32)
    return (jnp.concatenate([cos, cos, pad], axis=1),
            jnp.concatenate([-sin, z, pad], axis=1),
            jnp.concatenate([z, sin, pad], axis=1))


def _flash_kernel(q_ref, k_ref, vt_ref, o_ref, m_ref, acc_ref, s_buf, p_buf, alpha_buf, *, tq, ck, dv):
    i = pl.program_id(2)
    nb = tq // ck
    n_past = i * nb
    m_ref[...] = jnp.full_like(m_ref, NEG)
    acc_ref[...] = jnp.zeros_like(acc_ref)

    def scores(j, r0):
        start = pl.multiple_of(j * ck, ck)
        return lax.dot_general(k_ref[0, pl.ds(start, ck), :], q_ref[0, r0:, :], _NT,
                               preferred_element_type=jnp.float32)

    def fold(j, r0, s):
        start = pl.multiple_of(j * ck, ck)
        probs, alphas = [], []
        for c0 in range(0, tq - r0, LANES):
            cols = slice(r0 + c0, r0 + c0 + LANES)
            s_c = s[:, c0:c0 + LANES]
            m_prev = m_ref[:, cols]
            m_new = jnp.maximum(m_prev, jnp.max(s_c, axis=0, keepdims=True))
            m_ref[:, cols] = m_new
            alphas.append(jnp.exp2(m_prev - m_new))
            probs.append(jnp.exp2(s_c - m_new).astype(jnp.bfloat16))
        p = jnp.concatenate(probs, axis=1)
        alpha = jnp.concatenate(alphas, axis=1)
        acc_ref[:, r0:] = alpha * acc_ref[:, r0:] + jnp.dot(
            vt_ref[0, :, pl.ds(start, ck)], p, preferred_element_type=jnp.float32)

    def stage(c, slot):
        other = 1 - slot
        s_buf[other] = scores(c + 1, 0)
        start = pl.multiple_of(jnp.maximum(c - 1, 0) * ck, ck)
        acc_ref[...] = alpha_buf[other] * acc_ref[...] + jnp.dot(
            vt_ref[0, :, pl.ds(start, ck)], p_buf[other], preferred_element_type=jnp.float32)
        for c0 in range(0, tq, LANES):
            cols = slice(c0, c0 + LANES)
            s_c = s_buf[slot, :, cols]
            m_prev = m_ref[:, cols]
            m_new = jnp.maximum(m_prev, jnp.max(s_c, axis=0, keepdims=True))
            m_ref[:, cols] = m_new
            alpha_buf[slot, :, cols] = jnp.exp2(m_prev - m_new)
            p_buf[slot, :, cols] = jnp.exp2(s_c - m_new).astype(jnp.bfloat16)

    def pair(jj, carry):
        stage(2 * jj, 0)
        stage(2 * jj + 1, 1)
        return carry

    s_buf[0] = scores(0, 0)
    p_buf[1] = jnp.zeros((ck, tq), jnp.bfloat16)
    alpha_buf[1] = jnp.ones((1, tq), jnp.float32)
    lax.fori_loop(0, n_past // 2, pair, 0)
    start = pl.multiple_of(jnp.maximum(n_past - 1, 0) * ck, ck)
    acc_ref[...] = alpha_buf[1] * acc_ref[...] + jnp.dot(
        vt_ref[0, :, pl.ds(start, ck)], p_buf[1], preferred_element_type=jnp.float32)
    s = s_buf[0]
    for jb in range(nb):
        j = n_past + jb
        r0 = jb * ck
        if jb > 0:
            s = scores(j, r0)
        ki = lax.broadcasted_iota(jnp.int32, (ck, ck), 0)
        qi = lax.broadcasted_iota(jnp.int32, (ck, ck), 1)
        own = jnp.where(qi >= ki, s[:, :ck], NEG)
        fold(j, r0, own if tq - r0 == ck else jnp.concatenate([own, s[:, ck:]], axis=1))
    acc = acc_ref[...]
    o_ref[...] = (acc[:dv] / acc[dv:dv + 1]).T.astype(o_ref.dtype)


def flash_attention(q, k, vt, *, batch, seq, tq, ck):
    heads, n, dq = q.shape
    dvp = vt.shape[1]
    dv = dvp - ONES_ROWS
    nq = seq // tq
    return pl.pallas_call(
        functools.partial(_flash_kernel, tq=tq, ck=ck, dv=dv),
        grid=(batch, heads, nq),
        in_specs=[pl.BlockSpec((1, tq, dq), lambda b, h, i: (h, b * nq + i, 0)),
                  pl.BlockSpec((1, seq, dq), lambda b, h, i: (h * batch + b, 0, 0)),
                  pl.BlockSpec((1, dvp, seq), lambda b, h, i: (h, 0, b))],
        out_specs=pl.BlockSpec((tq, dv), lambda b, h, i: (b * nq + i, h)),
        out_shape=jax.ShapeDtypeStruct((n, heads * dv), jnp.bfloat16),
        scratch_shapes=[pltpu.VMEM((1, tq), jnp.float32),
                        pltpu.VMEM((dvp, tq), jnp.float32),
                        pltpu.VMEM((2, ck, tq), jnp.float32),
                        pltpu.VMEM((2, ck, tq), jnp.bfloat16),
                        pltpu.VMEM((2, 1, tq), jnp.float32)],
        compiler_params=_params(("parallel", "parallel", "arbitrary"), 48),
        name=f"flash_h{heads}",
    )(q, k.reshape(heads * batch, seq, dq), vt)


def _prep_c_kernel(q_ref, k_ref, v_ref, gq_ref, gk_ref, alibi_ref, qo_ref, ko_ref, vo_ref,
                   kmean_ref, *, blocks_per_seq, scale):
    i = pl.program_id(0)
    bq = i % blocks_per_seq

    @pl.when(bq == 0)
    def _():
        kmean_ref[...] = jnp.zeros_like(kmean_ref)

    rows = q_ref.shape[0]
    nblk = blocks_per_seq
    blk = lax.broadcasted_iota(jnp.int32, (nblk, rows), 0)
    lane = lax.broadcasted_iota(jnp.int32, (rows, LANES), 1)
    kpos = bq * rows + lax.broadcasted_iota(jnp.int32, (rows, LANES), 0)
    pos_part = jnp.where((lane - nblk) % 2 == 0, jnp.right_shift(kpos, 7), kpos & (LANES - 1))
    k_extra = jnp.where(lane < nblk, (lane == bq).astype(jnp.float32),
                        jnp.where(lane < nblk + ALIBI_COLS, pos_part.astype(jnp.float32), 0.0))
    k_extra = k_extra.astype(jnp.bfloat16)
    ones = jnp.ones((ONES_ROWS, rows), jnp.bfloat16)
    for h in range(C_HEADS):
        sl = slice(h * HEAD_DIM, (h + 1) * HEAD_DIM)
        q = q_ref[:, sl]
        qn = q * _rms_scale(q, HEAD_DIM) * gq_ref[...]
        k = k_ref[:, sl]
        kn = k * _rms_scale(k, HEAD_DIM) * gk_ref[...]
        qo_ref[h, :, :HEAD_DIM] = (qn * scale).astype(jnp.bfloat16)
        ko_ref[h, :, :HEAD_DIM] = kn.astype(jnp.bfloat16)
        ko_ref[h, :, HEAD_DIM:] = k_extra
        vo_ref[h, :HEAD_DIM, :] = v_ref[:, sl].T.astype(jnp.bfloat16)
        vo_ref[h, HEAD_DIM:, :] = ones
        gate = lax.dot_general(kmean_ref[h], qn, _NT, precision=lax.Precision.HIGHEST,
                               preferred_element_type=jnp.float32)
        gate = jnp.where(blk < bq, gate, NEG)
        kmean_ref[h, pl.ds(bq, 1), :] = jnp.mean(kn, axis=0, keepdims=True)
        allowed = blk == bq
        for r in range(MOBA_TOPK):
            best = jnp.max(gate, axis=0, keepdims=True)
            idx = jnp.min(jnp.where(gate == best, blk, nblk), axis=0, keepdims=True)
            allowed = allowed | ((blk == idx) & (r < bq))
            gate = jnp.where(blk == idx, 2.0 * NEG, gate)
        block_bias = jnp.where(allowed, 0.0, NEG)
        block_bias = jnp.concatenate(
            [block_bias, jnp.zeros((LANES - nblk, rows), jnp.float32)], axis=0)
        qo_ref[h, :, HEAD_DIM:] = (block_bias.T + alibi_ref[h]).astype(jnp.bfloat16)


def prep_c(qkv, gq, gk, alibi_q, *, seq):
    n = qkv.shape[0]
    tm = MOBA_BLOCK
    w = C_HEADS * HEAD_DIM
    blocks_per_seq = seq // tm
    assert blocks_per_seq + ALIBI_COLS <= LANES
    blk = lambda c: pl.BlockSpec((tm, w), functools.partial(lambda i, c: (i, c), c=c))
    gspec = pl.BlockSpec((1, HEAD_DIM), lambda i: (0, 0))
    hm = pl.BlockSpec((C_HEADS, tm, 2 * HEAD_DIM), lambda i: (0, i, 0))
    hm_shape = jax.ShapeDtypeStruct((C_HEADS, n, 2 * HEAD_DIM), jnp.bfloat16)
    dvp = HEAD_DIM + ONES_ROWS
    return pl.pallas_call(
        functools.partial(_prep_c_kernel, blocks_per_seq=blocks_per_seq,
                          scale=HEAD_DIM ** -0.5 * LOG2E),
        grid=(n // tm,),
        in_specs=[blk(0), blk(1), blk(2), gspec, gspec,
                  pl.BlockSpec((C_HEADS, 1, LANES), lambda i: (0, 0, 0))],
        out_specs=[hm, hm, pl.BlockSpec((C_HEADS, dvp, tm), lambda i: (0, 0, i))],
        out_shape=[hm_shape, hm_shape, jax.ShapeDtypeStruct((C_HEADS, dvp, n), jnp.bfloat16)],
        scratch_shapes=[pltpu.VMEM((C_HEADS, blocks_per_seq, HEAD_DIM), jnp.float32)],
        compiler_params=_params(("arbitrary",), 40),
        name="prep_c",
    )(qkv, qkv, qkv, gq.reshape(1, -1), gk.reshape(1, -1), alibi_q)


def alibi_query_columns(blocks_per_seq):
    sigma = 2.0 ** (-8.0 * jnp.arange(1, C_HEADS + 1, dtype=jnp.float32) / C_HEADS) * LOG2E
    pieces = []
    rest = sigma
    for _ in range(ALIBI_COLS // 2):
        piece = rest.astype(jnp.bfloat16).astype(jnp.float32)
        pieces += [piece * float(LANES), piece]
        rest = rest - piece
    cols = jnp.stack(pieces, axis=1)
    cols = jnp.pad(cols, ((0, 0), (blocks_per_seq, LANES - blocks_per_seq - ALIBI_COLS)))
    return cols.reshape(C_HEADS, 1, LANES)


def _router_kernel(h_ref, g_ref, wr_ref, b_ref, gate_ref, grp_ref):
    x = h_ref[...]
    t = x * _rms_scale(x, x.shape[1]) * g_ref[...]
    logits = lax.dot_general(wr_ref[...], t, _NT, precision=lax.Precision.HIGHEST,
                             preferred_element_type=jnp.float32)
    scores = 1.0 / (1.0 + jnp.exp(-logits))
    biased = scores + b_ref[...]
    s = [scores[e:e + 1, :] for e in range(N_EXPERTS)]
    b = [biased[e:e + 1, :] for e in range(N_EXPERTS)]
    best_val, best_grp = None, None
    for g in range(N_GROUPS):
        mem = b[g * GROUP:(g + 1) * GROUP]
        top2 = None
        for x1 in range(GROUP):
            for x2 in range(x1 + 1, GROUP):
                pair = mem[x1] + mem[x2]
                top2 = pair if top2 is None else jnp.maximum(top2, pair)
        if g == 0:
            best_val, best_grp = top2, jnp.zeros_like(top2, dtype=jnp.int32)
        else:
            better = top2 > best_val
            best_grp = jnp.where(better, g, best_grp)
            best_val = jnp.where(better, top2, best_val)
    picked = []
    for e in range(N_EXPERTS):
        g = e // GROUP
        ahead = jnp.zeros_like(best_grp)
        for o in range(g * GROUP, (g + 1) * GROUP):
            if o == e:
                continue
            wins = (b[o] > b[e]) | ((b[o] == b[e]) & (o < e))
            ahead = ahead + wins.astype(jnp.int32)
        picked.append(jnp.where((best_grp == g) & (ahead < TOP_K), s[e], 0.0))
    total = picked[0]
    for e in range(1, N_EXPERTS):
        total = total + picked[e]
    for e in range(N_EXPERTS):
        gate_ref[e:e + 1, :] = picked[e] / total
    grp_ref[...] = best_grp


def router(h, g, w_router_t, bias, *, tm=512):
    n, d = h.shape
    return pl.pallas_call(
        _router_kernel,
        grid=(n // tm,),
        in_specs=[pl.BlockSpec((tm, d), lambda i: (i, 0)),
                  pl.BlockSpec((1, d), lambda i: (0, 0)),
                  pl.BlockSpec((N_EXPERTS, d), lambda i: (0, 0)),
                  pl.BlockSpec((N_EXPERTS, 1), lambda i: (0, 0))],
        out_specs=[pl.BlockSpec((N_EXPERTS, tm), lambda i: (0, i)),
                   pl.BlockSpec((1, tm), lambda i: (0, i))],
        out_shape=[jax.ShapeDtypeStruct((N_EXPERTS, n), jnp.float32),
                   jax.ShapeDtypeStruct((1, n), jnp.int32)],
        compiler_params=_params(("parallel",), 40),
        name="router",
    )(h, g.reshape(1, d), w_router_t, bias.reshape(N_EXPERTS, 1))


def _row_copies(tok_ref, base, rows, hbm_ref, buf_ref, sem, to_hbm):
    def copy(r):
        hbm_row = hbm_ref.at[pl.ds(tok_ref[base + r], 1), :]
        buf_row = buf_ref.at[pl.ds(r, 1), :]
        src, dst = (buf_row, hbm_row) if to_hbm else (hbm_row, buf_row)
        return pltpu.make_async_copy(src, dst, sem)

    def start(r, carry):
        copy(r).start()
        return carry

    def wait(r, carry):
        copy(r).wait()
        return carry

    lax.fori_loop(0, rows, start, 0, unroll=8)
    lax.fori_loop(0, rows, wait, 0, unroll=8)


def _moe_kernel(tile_ref, grp_ref, flag_ref, tok_ref, h_hbm, g_ref, gate_ref, wg_ref, wu_ref, wd_ref,
                out_hbm, hbuf, tbuf, sem, *, tm, halves):
    j = pl.program_id(0)
    sub = pl.program_id(1)
    flag = flag_ref[j]
    base = tile_ref[j] * tm

    @pl.when((sub == 0) & ((flag & 2) != 0))
    def _():
        _row_copies(tok_ref, base, tm, h_hbm, hbuf, sem, to_hbm=False)
        x = hbuf[...]
        tbuf[...] = (x * _rms_scale(x, x.shape[1]) * g_ref[...]).astype(jnp.bfloat16)

    @pl.when((flag & 1) != 0)
    def _():
        e = grp_ref[j] * GROUP + sub // halves
        t = tbuf[...]
        a = jnp.dot(t, wg_ref[0, 0].astype(jnp.bfloat16), preferred_element_type=jnp.float32)
        u = jnp.dot(t, wu_ref[0, 0].astype(jnp.bfloat16), preferred_element_type=jnp.float32)
        hid = (a / (1.0 + jnp.exp(-a))) * u
        y = jnp.dot(hid.astype(jnp.bfloat16), wd_ref[0, 0].astype(jnp.bfloat16),
                    preferred_element_type=jnp.float32)
        gates = gate_ref[...]
        lane = lax.broadcasted_iota(jnp.int32, gates.shape, 1)
        g_col = jnp.sum(jnp.where(lane == e, gates, 0.0), axis=1, keepdims=True)
        hbuf[...] += g_col * y

    @pl.when((sub == pl.num_programs(1) - 1) & ((flag & 4) != 0))
    def _():
        _row_copies(tok_ref, base, tm, out_hbm, hbuf, sem, to_hbm=True)


def _moe_worklist(grp, tm):
    n = grp.shape[0]
    n_tiles = n // tm
    order = jnp.argsort(grp, stable=True).astype(jnp.int32)
    counts = jnp.sum(grp[None, :] == jnp.arange(N_GROUPS, dtype=jnp.int32)[:, None], axis=1)
    ends = jnp.cumsum(counts)
    first_slot = jnp.arange(n_tiles, dtype=jnp.int32) * tm
    g_lo = jnp.sum(ends[None, :] <= first_slot[:, None], axis=1)
    g_hi = jnp.sum(ends[None, :] <= (first_slot + tm - 1)[:, None], axis=1)
    cnt = g_hi - g_lo + 1
    first_item = jnp.cumsum(cnt) - cnt
    total = first_item[-1] + cnt[-1]
    n_items = n_tiles + N_GROUPS - 1
    item = jnp.arange(n_items, dtype=jnp.int32)
    tile = jnp.minimum(jnp.sum((first_item + cnt)[None, :] <= item[:, None], axis=1), n_tiles - 1)
    valid = item < total
    grp_of = jnp.where(valid, g_lo[tile] + item - first_item[tile], g_hi[n_tiles - 1])
    is_first = valid & (item == first_item[tile])
    is_last = valid & (item == first_item[tile] + cnt[tile] - 1)
    flag = valid.astype(jnp.int32) + 2 * is_first.astype(jnp.int32) + 4 * is_last.astype(jnp.int32)
    return order, tile.astype(jnp.int32), grp_of.astype(jnp.int32), flag


def moe_grouped(h, ffn_g, gate_t, grp, wg, wu, wd, *, layer, tm=MOE_TM, halves=MOE_FF_SPLIT):
    n, d = h.shape
    ff = wg.shape[3] // halves
    order, tile, grp_of, flag = _moe_worklist(grp.reshape(n), tm)
    gates_sorted = jnp.take(gate_t.T, order, axis=0)
    expert = lambda j, sub, tile_ref, grp_ref, flag_ref, tok_ref: grp_ref[j] * GROUP + sub // halves
    grid_spec = pltpu.PrefetchScalarGridSpec(
        num_scalar_prefetch=4,
        grid=(tile.shape[0], GROUP * halves),
        in_specs=[pl.BlockSpec(memory_space=pl.ANY),
                  pl.BlockSpec((1, d), lambda j, sub, *_: (0, 0)),
                  pl.BlockSpec((tm, N_EXPERTS), lambda j, sub, tile_ref, *_: (tile_ref[j], 0)),
                  pl.BlockSpec((1, 1, d, ff),
                               lambda j, sub, *p: (layer, expert(j, sub, *p), 0, sub % halves)),
                  pl.BlockSpec((1, 1, d, ff),
                               lambda j, sub, *p: (layer, expert(j, sub, *p), 0, sub % halves)),
                  pl.BlockSpec((1, 1, ff, d),
                               lambda j, sub, *p: (layer, expert(j, sub, *p), sub % halves, 0))],
        out_specs=pl.BlockSpec(memory_space=pl.ANY),
        scratch_shapes=[pltpu.VMEM((tm, d), jnp.float32),
                        pltpu.VMEM((tm, d), jnp.bfloat16),
                        pltpu.SemaphoreType.DMA(())])
    return pl.pallas_call(
        functools.partial(_moe_kernel, tm=tm, halves=halves),
        grid_spec=grid_spec,
        out_shape=jax.ShapeDtypeStruct((n, d), jnp.float32),
        compiler_params=_params(("arbitrary", "arbitrary"), 48),
        name="moe_grouped",
    )(tile, grp_of, flag, order, h, ffn_g.reshape(1, d), gates_sorted, wg, wu, wd)


def moe_layer(h, ffn_g, w_router_t, router_bias, wg, wu, wd, *, layer):
    gate_t, grp = router(h, ffn_g, w_router_t, router_bias)
    return moe_grouped(h, ffn_g, gate_t, grp, wg, wu, wd, layer=layer)


def _pad_cols(w, width):
    return jnp.pad(w, ((0, 0), (0, width - w.shape[1])))


def _pad_heads(w, heads, real, padded):
    k = w.shape[0]
    w = w.reshape(k, heads, real)
    return jnp.pad(w, ((0, 0), (0, 0), (0, padded - real))).reshape(k, heads * padded)


def kernel(x, attn_norm, ev_w_in, ev_q_a_norm, ev_w_q_b, ev_kv_a_norm, ev_w_kv_b, ev_qn_a, ev_kn_a, ev_qn_b, ev_kn_b, ev_w_o, od_w_qkv, od_qn, od_kn, od_w_o, ffn_norm, w_router, router_bias, w_gate, w_up, w_down):
    batch, seq, d = x.shape
    n = batch * seq
    bf = jnp.bfloat16
    h = x.reshape(n, d)
    w_router_t = w_router.T
    depth = attn_norm.shape[0]
    for layer in range(depth):
        i = layer // 2
        if layer % 2 == 0:
            w_in = _pad_cols(ev_w_in[i], EVEN_IN_PAD).astype(bf)
            proj = norm_matmul(h, attn_norm[layer], w_in)
            qkv_a = prep_a(proj, ev_qn_a[i], ev_kn_a[i], batch=batch, seq=seq)
            out_a = dilated_attention(qkv_a, batch=batch, seq=seq)
            w_qb = _pad_heads(ev_w_q_b[i], B_HEADS, MLA_QK, MLA_QK_PAD).astype(bf)
            qb = norm_matmul(proj, ev_q_a_norm[i], w_qb, col_block=A_QKV // Q_LORA)
            ckv = norm_matmul(proj, ev_kv_a_norm[i], ev_w_kv_b[i].astype(bf),
                              col_block=(A_QKV + Q_LORA) // KV_LORA)
            gq_pad = jnp.pad(ev_qn_b[i], (0, MLA_QK_PAD - MLA_QK)).reshape(1, -1)
            gk_nope = ev_kn_b[i][:HEAD_DIM].reshape(1, -1)
            gk_pe = jnp.pad(ev_kn_b[i][HEAD_DIM:], (0, LANES - ROPE_DIM)).reshape(1, -1)
            q_b, k_b, v_b = prep_b(qb, ckv, proj, gq_pad, gk_nope, gk_pe, rope_tables(seq), seq=seq)
            out_b = flash_attention(q_b, k_b, v_b, batch=batch, seq=seq,
                                    tq=min(FLASH_TQ, seq), ck=FLASH_CK)
            h = matmul_residual([out_a, out_b], ev_w_o[i].astype(bf), h)
        else:
            qkv = norm_matmul(h, attn_norm[layer], od_w_qkv[i].astype(bf))
            q_c, k_c, v_c = prep_c(qkv, od_qn[i], od_kn[i],
                                   alibi_query_columns(seq // MOBA_BLOCK), seq=seq)
            out_c = flash_attention(q_c, k_c, v_c, batch=batch, seq=seq, tq=min(FLASH_TQ, seq),
                                    ck=FLASH_CK)
            h = matmul_residual([out_c], od_w_o[i].astype(bf), h)
        h = moe_layer(h, ffn_norm[layer], w_router_t, router_bias, w_gate, w_up, w_down, layer=layer)
    return h.reshape(batch, seq, d)
```

```python
import functools
import math

import jax
import jax.numpy as jnp
from jax import lax
from jax.experimental import pallas as pl
from jax.experimental.pallas import tpu as pltpu

EPS = 1e-6
NEG = -1e30
LANES = 128
HEAD_DIM = 128
ROPE_DIM = 64
ROPE_THETA = 10000.0
A_HEADS = 8
A_PATTERNS = ((128, 1), (512, 4), (2048, 16))
B_HEADS = 8
Q_LORA = 512
KV_LORA = 256
MLA_QK = 192
MLA_QK_PAD = 256
C_HEADS = 16
MOBA_BLOCK = 256
MOBA_TOPK = 3
N_EXPERTS = 16
N_GROUPS = 4
GROUP = N_EXPERTS // N_GROUPS
TOP_K = 2
EXPERT_FF = 512
A_QKV = 3 * A_HEADS * HEAD_DIM
EVEN_IN_PAD = 4096
KPE_COL = A_QKV + Q_LORA + KV_LORA
MIB = 1024 * 1024
FLASH_TQ = 1024
FLASH_CK = 512
ONES_ROWS = 16
ALIBI_COLS = 6
LOG2E = 1.4426950408889634
MOE_TM = 1024
MOE_FF_SPLIT = 2

_NT = (((1,), (1,)), ((), ()))


def _params(semantics, vmem_mib):
    return pltpu.CompilerParams(dimension_semantics=semantics,
                                vmem_limit_bytes=vmem_mib * MIB)


def _rms_scale(x, width):
    return lax.rsqrt(jnp.sum(x * x, axis=-1, keepdims=True) * (1.0 / width) + EPS)


def _norm_matmul_kernel(x_ref, g_ref, w_ref, o_ref, xn_ref, *, width):
    @pl.when(pl.program_id(1) == 0)
    def _():
        x = x_ref[...]
        xn_ref[...] = (x * _rms_scale(x, width) * g_ref[...]).astype(jnp.bfloat16)

    o_ref[...] = jnp.dot(xn_ref[...], w_ref[...],
                         preferred_element_type=jnp.float32).astype(o_ref.dtype)


def norm_matmul(x, g, w, *, col_block=0, tm=1024, tn=1024):
    n = x.shape[0]
    k, m = w.shape
    return pl.pallas_call(
        functools.partial(_norm_matmul_kernel, width=k),
        grid=(n // tm, m // tn),
        in_specs=[pl.BlockSpec((tm, k), lambda i, j: (i, col_block)),
                  pl.BlockSpec((1, k), lambda i, j: (0, 0)),
                  pl.BlockSpec((k, tn), lambda i, j: (0, j))],
        out_specs=pl.BlockSpec((tm, tn), lambda i, j: (i, j)),
        out_shape=jax.ShapeDtypeStruct((n, m), jnp.float32),
        scratch_shapes=[pltpu.VMEM((tm, k), jnp.bfloat16)],
        compiler_params=_params(("parallel", "arbitrary"), 48),
        name="norm_matmul",
    )(x, g.reshape(1, k), w)


def _matmul_res_kernel(*refs, n_a):
    a_refs, w_refs = refs[:n_a], refs[n_a:2 * n_a]
    r_ref, o_ref = refs[2 * n_a], refs[2 * n_a + 1]
    acc = r_ref[...]
    for a_ref, w_ref in zip(a_refs, w_refs):
        acc = acc + jnp.dot(a_ref[...], w_ref[...], preferred_element_type=jnp.float32)
    o_ref[...] = acc


def matmul_residual(a_list, w, res, *, tm=512, tn=1024):
    n, m = res.shape
    n_a = len(a_list)
    ka = a_list[0].shape[1]
    in_specs = [pl.BlockSpec((tm, ka), lambda i, j: (i, 0)) for _ in a_list]
    in_specs += [pl.BlockSpec((ka, tn), functools.partial(lambda i, j, p: (p, j), p=p))
                 for p in range(n_a)]
    in_specs += [pl.BlockSpec((tm, tn), lambda i, j: (i, j))]
    return pl.pallas_call(
        functools.partial(_matmul_res_kernel, n_a=n_a),
        grid=(n // tm, m // tn),
        in_specs=in_specs,
        out_specs=pl.BlockSpec((tm, tn), lambda i, j: (i, j)),
        out_shape=jax.ShapeDtypeStruct((n, m), jnp.float32),
        compiler_params=_params(("parallel", "parallel"), 40),
        name="matmul_residual",
    )(*a_list, *([w] * n_a), res)


def _prep_a_kernel(q_ref, k_ref, v_ref, gq_ref, gk_ref, *refs, scale):
    n_pat = len(A_PATTERNS)
    outs, (qs, ks, vs) = refs[:3 * n_pat], refs[3 * n_pat:]
    heads = [slice(h * HEAD_DIM, (h + 1) * HEAD_DIM) for h in range(A_HEADS)]
    for h, sl in enumerate(heads):
        q = q_ref[:, sl]
        qs[h] = q * _rms_scale(q, HEAD_DIM) * gq_ref[...] * scale
        k = k_ref[:, sl]
        ks[h] = k * _rms_scale(k, HEAD_DIM) * gk_ref[...]
        vs[h] = v_ref[:, sl]
    rows = qs.shape[1]
    for p, (_, dil) in enumerate(A_PATTERNS):
        for src, dst in zip((qs, ks, vs), outs[3 * p:3 * p + 3]):
            for r in range(dil):
                for h, sl in enumerate(heads):
                    dst[0, r, :, sl] = src[h, pl.ds(r, rows // dil, stride=dil), :].astype(
                        jnp.bfloat16)


def prep_a(proj, gq, gk, *, batch, seq, tm=512):
    n = proj.shape[0]
    w = A_HEADS * HEAD_DIM
    per_seq = seq // tm
    blk = lambda c: pl.BlockSpec((tm, w), functools.partial(lambda i, c: (i, c), c=c))
    gspec = pl.BlockSpec((1, HEAD_DIM), lambda i: (0, 0))
    out_specs, out_shape = [], []
    for _, dil in A_PATTERNS:
        spec = pl.BlockSpec((1, dil, tm // dil, w), lambda i: (i // per_seq, 0, i % per_seq, 0))
        shape = jax.ShapeDtypeStruct((batch, dil, seq // dil, w), jnp.bfloat16)
        out_specs += [spec] * 3
        out_shape += [shape] * 3
    outs = pl.pallas_call(
        functools.partial(_prep_a_kernel, scale=HEAD_DIM ** -0.5),
        grid=(n // tm,),
        in_specs=[blk(0), blk(1), blk(2), gspec, gspec],
        out_specs=out_specs,
        out_shape=out_shape,
        scratch_shapes=[pltpu.VMEM((A_HEADS, tm, HEAD_DIM), jnp.float32)] * 3,
        compiler_params=_params(("parallel",), 48),
        name="prep_a",
    )(proj, proj, proj, gq.reshape(1, -1), gk.reshape(1, -1))
    return [outs[3 * p:3 * p + 3] for p in range(len(A_PATTERNS))]


def _dilated_kernel(q_ref, kc_ref, kp_ref, vc_ref, vp_ref, o_ref, lse_ref, *, dil):
    band = HEAD_DIM
    n = pl.program_id(2)
    qi = lax.broadcasted_iota(jnp.int32, (band, band), 0)
    ki = lax.broadcasted_iota(jnp.int32, (band, band), 1)
    dist_c = (qi - ki).astype(jnp.float32)
    dist_p = dist_c + float(band)
    valid_c = qi >= ki
    valid_p = jnp.logical_and(ki >= qi, n > 0)
    heads = [slice(h * HEAD_DIM, (h + 1) * HEAD_DIM) for h in range(A_HEADS)]
    logits = []
    for sl in heads:
        q = q_ref[0, 0, :, sl]
        logits.append((lax.dot_general(q, kc_ref[0, 0, :, sl], _NT, preferred_element_type=jnp.float32),
                       lax.dot_general(q, kp_ref[0, 0, :, sl], _NT, preferred_element_type=jnp.float32)))
    probs = []
    lse_tile = jnp.zeros((band, LANES), jnp.float32)
    for h, (sc, sp) in enumerate(logits):
        slope = 2.0 ** (-8.0 * (h + 1) / A_HEADS) * dil
        sc = jnp.where(valid_c, sc - slope * dist_c, NEG)
        sp = jnp.where(valid_p, sp - slope * dist_p, NEG)
        m = jnp.maximum(jnp.max(sc, axis=1, keepdims=True), jnp.max(sp, axis=1, keepdims=True))
        pc = jnp.exp(sc - m)
        pp = jnp.exp(sp - m)
        l = jnp.sum(pc, axis=1, keepdims=True) + jnp.sum(pp, axis=1, keepdims=True)
        probs.append((pc.astype(jnp.bfloat16), pp.astype(jnp.bfloat16), l))
        lse_tile = jnp.where(ki == h, m + jnp.log(l), lse_tile)
    lse_ref[0, 0] = lse_tile
    for sl, (pc, pp, l) in zip(heads, probs):
        acc = jnp.dot(pc, vc_ref[0, 0, :, sl], preferred_element_type=jnp.float32)
        acc = acc + jnp.dot(pp, vp_ref[0, 0, :, sl], preferred_element_type=jnp.float32)
        o_ref[0, 0, :, sl] = acc / l


def dilated_pattern(q, k, v, *, dil):
    batch, _, sub, w = q.shape
    band = HEAD_DIM
    cur = pl.BlockSpec((1, 1, band, w), lambda b, r, n: (b, r, n, 0))
    prev = pl.BlockSpec((1, 1, band, w), lambda b, r, n: (b, r, jnp.maximum(n - 1, 0), 0))
    return pl.pallas_call(
        functools.partial(_dilated_kernel, dil=dil),
        grid=(batch, dil, sub // band),
        in_specs=[cur, cur, prev, cur, prev],
        out_specs=[cur, pl.BlockSpec((1, 1, band, LANES), lambda b, r, n: (b, r, n, 0))],
        out_shape=[jax.ShapeDtypeStruct((batch, dil, sub, w), jnp.float32),
                   jax.ShapeDtypeStruct((batch, dil, sub, LANES), jnp.float32)],
        compiler_params=_params(("parallel", "parallel", "arbitrary"), 40),
        name=f"dilated_d{dil}",
    )(q, k, k, v, v)


def _dilated_merge_kernel(*refs):
    n_pat = len(A_PATTERNS)
    ins, out_ref, scratch = refs[:2 * n_pat], refs[2 * n_pat], refs[2 * n_pat + 1:]
    rows = out_ref.shape[0]
    heads = [slice(h * HEAD_DIM, (h + 1) * HEAD_DIM) for h in range(A_HEADS)]
    outs, lses = [], []
    for p, (_, dil) in enumerate(A_PATTERNS):
        o_ref, l_ref = ins[2 * p], ins[2 * p + 1]
        if dil == 1:
            outs.append(lambda h, o_ref=o_ref: o_ref[0, 0, :, heads[h]])
            lses.append(l_ref[0, 0])
            continue
        o_nat, l_nat = scratch[2 * p], scratch[2 * p + 1]
        for r in range(dil):
            for h, sl in enumerate(heads):
                o_nat[h, pl.ds(r, rows // dil, stride=dil), :] = o_ref[0, r, :, sl]
            l_nat[pl.ds(r, rows // dil, stride=dil), :] = l_ref[0, r]
        outs.append(lambda h, o_nat=o_nat: o_nat[h])
        lses.append(l_nat[...])
    for h, sl in enumerate(heads):
        lse_h = [l[:, h:h + 1] for l in lses]
        top = functools.reduce(jnp.maximum, lse_h)
        wts = [jnp.exp(l - top) for l in lse_h]
        num = sum(wt * o(h) for wt, o in zip(wts, outs))
        out_ref[:, sl] = (num / sum(wts)).astype(out_ref.dtype)


def dilated_attention(qkv_by_pattern, *, batch, seq, tm=512):
    w = A_HEADS * HEAD_DIM
    per_seq = seq // tm
    ins, in_specs, scratch = [], [], []
    for (q, k, v), (window, dil) in zip(qkv_by_pattern, A_PATTERNS):
        assert window // dil == HEAD_DIM
        o, lse = dilated_pattern(q, k, v, dil=dil)
        ins += [o, lse]
        in_specs += [pl.BlockSpec((1, dil, tm // dil, w), lambda i: (i // per_seq, 0, i % per_seq, 0)),
                     pl.BlockSpec((1, dil, tm // dil, LANES),
                                  lambda i: (i // per_seq, 0, i % per_seq, 0))]
        scratch += [pltpu.VMEM((A_HEADS, tm, HEAD_DIM), jnp.float32),
                    pltpu.VMEM((tm, LANES), jnp.float32)]
    n = batch * seq
    return pl.pallas_call(
        _dilated_merge_kernel,
        grid=(n // tm,),
        in_specs=in_specs,
        out_specs=pl.BlockSpec((tm, w), lambda i: (i, 0)),
        out_shape=jax.ShapeDtypeStruct((n, w), jnp.bfloat16),
        scratch_shapes=scratch,
        compiler_params=_params(("parallel",), 48),
        name="dilated_merge",
    )(*ins)


def _rope(t, cos, sin_lo, sin_hi):
    return (t * cos + pltpu.roll(t, LANES - ROPE_DIM // 2, 1) * sin_lo
            + pltpu.roll(t, ROPE_DIM // 2, 1) * sin_hi)


def _prep_b_kernel(qb_ref, ckv_ref, kpe_ref, gq_ref, gkn_ref, gkp_ref, cos_ref, slo_ref, shi_ref,
                   qo_ref, ko_ref, vo_ref, *, scale):
    cos, slo, shi = cos_ref[...], slo_ref[...], shi_ref[...]
    kpe = kpe_ref[...]
    kpe_ss = jnp.sum(kpe * kpe, axis=-1, keepdims=True)
    kpe_rot = _rope(kpe * gkp_ref[...], cos, slo, shi)
    for h in range(B_HEADS):
        lo = slice(h * MLA_QK_PAD, h * MLA_QK_PAD + HEAD_DIM)
        hi = slice(h * MLA_QK_PAD + HEAD_DIM, (h + 1) * MLA_QK_PAD)
        q_n, q_p = qb_ref[:, lo], qb_ref[:, hi]
        ss = jnp.sum(q_n * q_n, axis=-1, keepdims=True) + jnp.sum(q_p * q_p, axis=-1, keepdims=True)
        rq = lax.rsqrt(ss * (1.0 / MLA_QK) + EPS) * scale
        qo_ref[h, :, :HEAD_DIM] = (q_n * rq * gq_ref[:, :HEAD_DIM]).astype(jnp.bfloat16)
        qo_ref[h, :, HEAD_DIM:] = _rope(q_p * rq * gq_ref[:, HEAD_DIM:], cos, slo, shi
                                        ).astype(jnp.bfloat16)
        k_n = ckv_ref[:, lo]
        ss = jnp.sum(k_n * k_n, axis=-1, keepdims=True) + kpe_ss
        rk = lax.rsqrt(ss * (1.0 / MLA_QK) + EPS)
        ko_ref[h, :, :HEAD_DIM] = (k_n * rk * gkn_ref[...]).astype(jnp.bfloat16)
        ko_ref[h, :, HEAD_DIM:] = (kpe_rot * rk).astype(jnp.bfloat16)
        vo_ref[h, :HEAD_DIM, :] = ckv_ref[:, hi].T.astype(jnp.bfloat16)
        vo_ref[h, HEAD_DIM:, :] = jnp.ones((ONES_ROWS, ckv_ref.shape[0]), jnp.bfloat16)


def prep_b(qb, ckv, proj, gq_pad, gk_nope, gk_pe_pad, rope_tabs, *, seq, tm=256):
    n = qb.shape[0]
    wide = B_HEADS * MLA_QK_PAD
    nblk = seq // tm
    row = lambda c: pl.BlockSpec((tm, wide), lambda i: (i, 0))
    tab = pl.BlockSpec((tm, LANES), lambda i: (i % nblk, 0))
    return pl.pallas_call(
        functools.partial(_prep_b_kernel, scale=MLA_QK ** -0.5 * LOG2E),
        grid=(n // tm,),
        in_specs=[row(0), row(0),
                  pl.BlockSpec((tm, LANES), lambda i: (i, KPE_COL // LANES)),
                  pl.BlockSpec((1, MLA_QK_PAD), lambda i: (0, 0)),
                  pl.BlockSpec((1, LANES), lambda i: (0, 0)),
                  pl.BlockSpec((1, LANES), lambda i: (0, 0)),
                  tab, tab, tab],
        out_specs=[pl.BlockSpec((B_HEADS, tm, MLA_QK_PAD), lambda i: (0, i, 0)),
                   pl.BlockSpec((B_HEADS, tm, MLA_QK_PAD), lambda i: (0, i, 0)),
                   pl.BlockSpec((B_HEADS, HEAD_DIM + ONES_ROWS, tm), lambda i: (0, 0, i))],
        out_shape=[jax.ShapeDtypeStruct((B_HEADS, n, MLA_QK_PAD), jnp.bfloat16),
                   jax.ShapeDtypeStruct((B_HEADS, n, MLA_QK_PAD), jnp.bfloat16),
                   jax.ShapeDtypeStruct((B_HEADS, HEAD_DIM + ONES_ROWS, n), jnp.bfloat16)],
        compiler_params=_params(("parallel",), 40),
        name="prep_b",
    )(qb, ckv, proj, gq_pad, gk_nope, gk_pe_pad, *rope_tabs)


def rope_tables(seq):
    half = ROPE_DIM // 2
    inv = ROPE_THETA ** (-jnp.arange(0, ROPE_DIM, 2, dtype=jnp.float32) / ROPE_DIM)
    ang = jnp.arange(seq, dtype=jnp.float32)[:, None] * inv[None, :]
    cos, sin = jnp.cos(ang), jnp.sin(ang)
    z = jnp.zeros((seq, half), jnp.float32)
    pad = jnp.zeros((seq, LANES - ROPE_DIM), jnp.float32)
    return (jnp.concatenate([cos, cos, pad], axis=1),
            jnp.concatenate([-sin, z, pad], axis=1),
            jnp.concatenate([z, sin, pad], axis=1))


def _flash_kernel(q_ref, k_ref, vt_ref, o_ref, m_ref, acc_ref, s_buf, p_buf, alpha_buf, *, tq, ck, dv):
    i = pl.program_id(2)
    nb = tq // ck
    n_past = i * nb
    m_ref[...] = jnp.full_like(m_ref, NEG)
    acc_ref[...] = jnp.zeros_like(acc_ref)

    def scores(j, r0):
        start = pl.multiple_of(j * ck, ck)
        return lax.dot_general(k_ref[0, pl.ds(start, ck), :], q_ref[0, r0:, :], _NT,
                               preferred_element_type=jnp.float32)

    def fold(j, r0, s):
        start = pl.multiple_of(j * ck, ck)
        probs, alphas = [], []
        for c0 in range(0, tq - r0, LANES):
            cols = slice(r0 + c0, r0 + c0 + LANES)
            s_c = s[:, c0:c0 + LANES]
            m_prev = m_ref[:, cols]
            m_new = jnp.maximum(m_prev, jnp.max(s_c, axis=0, keepdims=True))
            m_ref[:, cols] = m_new
            alphas.append(jnp.exp2(m_prev - m_new))
            probs.append(jnp.exp2(s_c - m_new).astype(jnp.bfloat16))
        p = jnp.concatenate(probs, axis=1)
        alpha = jnp.concatenate(alphas, axis=1)
        acc_ref[:, r0:] = alpha * acc_ref[:, r0:] + jnp.dot(
            vt_ref[0, :, pl.ds(start, ck)], p, preferred_element_type=jnp.float32)

    def stage(c, slot):
        other = 1 - slot
        s_buf[other] = scores(c + 1, 0)
        start = pl.multiple_of(jnp.maximum(c - 1, 0) * ck, ck)
        acc_ref[...] = alpha_buf[other] * acc_ref[...] + jnp.dot(
            vt_ref[0, :, pl.ds(start, ck)], p_buf[other], preferred_element_type=jnp.float32)
        for c0 in range(0, tq, LANES):
            cols = slice(c0, c0 + LANES)
            s_c = s_buf[slot, :, cols]
            m_prev = m_ref[:, cols]
            m_new = jnp.maximum(m_prev, jnp.max(s_c, axis=0, keepdims=True))
            m_ref[:, cols] = m_new
            alpha_buf[slot, :, cols] = jnp.exp2(m_prev - m_new)
            p_buf[slot, :, cols] = jnp.exp2(s_c - m_new).astype(jnp.bfloat16)

    def pair(jj, carry):
        stage(2 * jj, 0)
        stage(2 * jj + 1, 1)
        return carry

    s_buf[0] = scores(0, 0)
    p_buf[1] = jnp.zeros((ck, tq), jnp.bfloat16)
    alpha_buf[1] = jnp.ones((1, tq), jnp.float32)
    lax.fori_loop(0, n_past // 2, pair, 0)
    start = pl.multiple_of(jnp.maximum(n_past - 1, 0) * ck, ck)
    acc_ref[...] = alpha_buf[1] * acc_ref[...] + jnp.dot(
        vt_ref[0, :, pl.ds(start, ck)], p_buf[1], preferred_element_type=jnp.float32)
    s = s_buf[0]
    for jb in range(nb):
        j = n_past + jb
        r0 = jb * ck
        if jb > 0:
            s = scores(j, r0)
        ki = lax.broadcasted_iota(jnp.int32, (ck, ck), 0)
        qi = lax.broadcasted_iota(jnp.int32, (ck, ck), 1)
        own = jnp.where(qi >= ki, s[:, :ck], NEG)
        fold(j, r0, own if tq - r0 == ck else jnp.concatenate([own, s[:, ck:]], axis=1))
    acc = acc_ref[...]
    o_ref[...] = (acc[:dv] / acc[dv:dv + 1]).T.astype(o_ref.dtype)


def flash_attention(q, k, vt, *, batch, seq, tq, ck):
    heads, n, dq = q.shape
    dvp = vt.shape[1]
    dv = dvp - ONES_ROWS
    nq = seq // tq
    return pl.pallas_call(
        functools.partial(_flash_kernel, tq=tq, ck=ck, dv=dv),
        grid=(batch, heads, nq),
        in_specs=[pl.BlockSpec((1, tq, dq), lambda b, h, i: (h, b * nq + i, 0)),
                  pl.BlockSpec((1, seq, dq), lambda b, h, i: (h * batch + b, 0, 0)),
                  pl.BlockSpec((1, dvp, seq), lambda b, h, i: (h, 0, b))],
        out_specs=pl.BlockSpec((tq, dv), lambda b, h, i: (b * nq + i, h)),
        out_shape=jax.ShapeDtypeStruct((n, heads * dv), jnp.bfloat16),
        scratch_shapes=[pltpu.VMEM((1, tq), jnp.float32),
                        pltpu.VMEM((dvp, tq), jnp.float32),
                        pltpu.VMEM((2, ck, tq), jnp.float32),
                        pltpu.VMEM((2, ck, tq), jnp.bfloat16),
                        pltpu.VMEM((2, 1, tq), jnp.float32)],
        compiler_params=_params(("parallel", "parallel", "arbitrary"), 48),
        name=f"flash_h{heads}",
    )(q, k.reshape(heads * batch, seq, dq), vt)


def _prep_c_kernel(q_ref, k_ref, v_ref, gq_ref, gk_ref, alibi_ref, qo_ref, ko_ref, vo_ref,
                   kmean_ref, *, blocks_per_seq, scale):
    i = pl.program_id(0)
    bq = i % blocks_per_seq

    @pl.when(bq == 0)
    def _():
        kmean_ref[...] = jnp.zeros_like(kmean_ref)

    rows = q_ref.shape[0]
    nblk = blocks_per_seq
    blk = lax.broadcasted_iota(jnp.int32, (nblk, rows), 0)
    lane = lax.broadcasted_iota(jnp.int32, (rows, LANES), 1)
    kpos = bq * rows + lax.broadcasted_iota(jnp.int32, (rows, LANES), 0)
    pos_part = jnp.where((lane - nblk) % 2 == 0, jnp.right_shift(kpos, 7), kpos & (LANES - 1))
    k_extra = jnp.where(lane < nblk, (lane == bq).astype(jnp.float32),
                        jnp.where(lane < nblk + ALIBI_COLS, pos_part.astype(jnp.float32), 0.0))
    k_extra = k_extra.astype(jnp.bfloat16)
    ones = jnp.ones((ONES_ROWS, rows), jnp.bfloat16)
    gates = []
    for h in range(C_HEADS):
        sl = slice(h * HEAD_DIM, (h + 1) * HEAD_DIM)
        q = q_ref[:, sl]
        qn = q * _rms_scale(q, HEAD_DIM) * gq_ref[...]
        k = k_ref[:, sl]
        kn = k * _rms_scale(k, HEAD_DIM) * gk_ref[...]
        qo_ref[h, :, :HEAD_DIM] = (qn * scale).astype(jnp.bfloat16)
        ko_ref[h, :, :HEAD_DIM] = kn.astype(jnp.bfloat16)
        ko_ref[h, :, HEAD_DIM:] = k_extra
        vo_ref[h, :HEAD_DIM, :] = v_ref[:, sl].T.astype(jnp.bfloat16)
        vo_ref[h, HEAD_DIM:, :] = ones
        gates.append(lax.dot_general(kmean_ref[h], qn, _NT, precision=lax.Precision.HIGHEST,
                                     preferred_element_type=jnp.float32))
        kmean_ref[h, pl.ds(bq, 1), :] = jnp.mean(kn, axis=0, keepdims=True)
    for h, gate in enumerate(gates):
        gate = jnp.where(blk < bq, gate, NEG)
        allowed = blk == bq
        for r in range(MOBA_TOPK):
            best = jnp.max(gate, axis=0, keepdims=True)
            idx = jnp.min(jnp.where(gate == best, blk, nblk), axis=0, keepdims=True)
            allowed = allowed | ((blk == idx) & (r < bq))
            gate = jnp.where(blk == idx, 2.0 * NEG, gate)
        block_bias = jnp.where(allowed, 0.0, NEG)
        block_bias = jnp.concatenate(
            [block_bias, jnp.zeros((LANES - nblk, rows), jnp.float32)], axis=0)
        qo_ref[h, :, HEAD_DIM:] = (block_bias.T + alibi_ref[h]).astype(jnp.bfloat16)


def prep_c(qkv, gq, gk, alibi_q, *, seq):
    n = qkv.shape[0]
    tm = MOBA_BLOCK
    w = C_HEADS * HEAD_DIM
    blocks_per_seq = seq // tm
    assert blocks_per_seq + ALIBI_COLS <= LANES
    blk = lambda c: pl.BlockSpec((tm, w), functools.partial(lambda i, c: (i, c), c=c))
    gspec = pl.BlockSpec((1, HEAD_DIM), lambda i: (0, 0))
    hm = pl.BlockSpec((C_HEADS, tm, 2 * HEAD_DIM), lambda i: (0, i, 0))
    hm_shape = jax.ShapeDtypeStruct((C_HEADS, n, 2 * HEAD_DIM), jnp.bfloat16)
    dvp = HEAD_DIM + ONES_ROWS
    return pl.pallas_call(
        functools.partial(_prep_c_kernel, blocks_per_seq=blocks_per_seq,
                          scale=HEAD_DIM ** -0.5 * LOG2E),
        grid=(n // tm,),
        in_specs=[blk(0), blk(1), blk(2), gspec, gspec,
                  pl.BlockSpec((C_HEADS, 1, LANES), lambda i: (0, 0, 0))],
        out_specs=[hm, hm, pl.BlockSpec((C_HEADS, dvp, tm), lambda i: (0, 0, i))],
        out_shape=[hm_shape, hm_shape, jax.ShapeDtypeStruct((C_HEADS, dvp, n), jnp.bfloat16)],
        scratch_shapes=[pltpu.VMEM((C_HEADS, blocks_per_seq, HEAD_DIM), jnp.float32)],
        compiler_params=_params(("arbitrary",), 40),
        name="prep_c",
    )(qkv, qkv, qkv, gq.reshape(1, -1), gk.reshape(1, -1), alibi_q)


def alibi_query_columns(blocks_per_seq):
    sigma = 2.0 ** (-8.0 * jnp.arange(1, C_HEADS + 1, dtype=jnp.float32) / C_HEADS) * LOG2E
    pieces = []
    rest = sigma
    for _ in range(ALIBI_COLS // 2):
        piece = rest.astype(jnp.bfloat16).astype(jnp.float32)
        pieces += [piece * float(LANES), piece]
        rest = rest - piece
    cols = jnp.stack(pieces, axis=1)
    cols = jnp.pad(cols, ((0, 0), (blocks_per_seq, LANES - blocks_per_seq - ALIBI_COLS)))
    return cols.reshape(C_HEADS, 1, LANES)


def _router_kernel(h_ref, g_ref, wr_ref, b_ref, gate_ref, grp_ref):
    x = h_ref[...]
    t = x * _rms_scale(x, x.shape[1]) * g_ref[...]
    logits = lax.dot_general(wr_ref[...], t, _NT, precision=lax.Precision.HIGHEST,
                             preferred_element_type=jnp.float32)
    scores = 1.0 / (1.0 + jnp.exp(-logits))
    biased = scores + b_ref[...]
    s = [scores[e:e + 1, :] for e in range(N_EXPERTS)]
    b = [biased[e:e + 1, :] for e in range(N_EXPERTS)]
    best_val, best_grp = None, None
    for g in range(N_GROUPS):
        mem = b[g * GROUP:(g + 1) * GROUP]
        top2 = None
        for x1 in range(GROUP):
            for x2 in range(x1 + 1, GROUP):
                pair = mem[x1] + mem[x2]
                top2 = pair if top2 is None else jnp.maximum(top2, pair)
        if g == 0:
            best_val, best_grp = top2, jnp.zeros_like(top2, dtype=jnp.int32)
        else:
            better = top2 > best_val
            best_grp = jnp.where(better, g, best_grp)
            best_val = jnp.where(better, top2, best_val)
    picked = []
    for e in range(N_EXPERTS):
        g = e // GROUP
        ahead = jnp.zeros_like(best_grp)
        for o in range(g * GROUP, (g + 1) * GROUP):
            if o == e:
                continue
            wins = (b[o] > b[e]) | ((b[o] == b[e]) & (o < e))
            ahead = ahead + wins.astype(jnp.int32)
        picked.append(jnp.where((best_grp == g) & (ahead < TOP_K), s[e], 0.0))
    total = picked[0]
    for e in range(1, N_EXPERTS):
        total = total + picked[e]
    for e in range(N_EXPERTS):
        gate_ref[e:e + 1, :] = picked[e] / total
    grp_ref[...] = best_grp


def router(h, g, w_router_t, bias, *, tm=512):
    n, d = h.shape
    return pl.pallas_call(
        _router_kernel,
        grid=(n // tm,),
        in_specs=[pl.BlockSpec((tm, d), lambda i: (i, 0)),
                  pl.BlockSpec((1, d), lambda i: (0, 0)),
                  pl.BlockSpec((N_EXPERTS, d), lambda i: (0, 0)),
                  pl.BlockSpec((N_EXPERTS, 1), lambda i: (0, 0))],
        out_specs=[pl.BlockSpec((N_EXPERTS, tm), lambda i: (0, i)),
                   pl.BlockSpec((1, tm), lambda i: (0, i))],
        out_shape=[jax.ShapeDtypeStruct((N_EXPERTS, n), jnp.float32),
                   jax.ShapeDtypeStruct((1, n), jnp.int32)],
        compiler_params=_params(("parallel",), 40),
        name="router",
    )(h, g.reshape(1, d), w_router_t, bias.reshape(N_EXPERTS, 1))


def _row_copies(tok_ref, base, rows, hbm_ref, buf_ref, sem, to_hbm):
    def copy(r):
        hbm_row = hbm_ref.at[pl.ds(tok_ref[base + r], 1), :]
        buf_row = buf_ref.at[pl.ds(r, 1), :]
        src, dst = (buf_row, hbm_row) if to_hbm else (hbm_row, buf_row)
        return pltpu.make_async_copy(src, dst, sem)

    def start(r, carry):
        copy(r).start()
        return carry

    def wait(r, carry):
        copy(r).wait()
        return carry

    lax.fori_loop(0, rows, start, 0, unroll=8)
    lax.fori_loop(0, rows, wait, 0, unroll=8)


def _moe_kernel(tile_ref, grp_ref, flag_ref, tok_ref, h_hbm, g_ref, gate_ref, wg_ref, wu_ref, wd_ref,
                out_hbm, hbuf, tbuf, sem, *, tm, halves):
    j = pl.program_id(0)
    sub = pl.program_id(1)
    flag = flag_ref[j]
    base = tile_ref[j] * tm

    @pl.when((sub == 0) & ((flag & 2) != 0))
    def _():
        _row_copies(tok_ref, base, tm, h_hbm, hbuf, sem, to_hbm=False)
        x = hbuf[...]
        tbuf[...] = (x * _rms_scale(x, x.shape[1]) * g_ref[...]).astype(jnp.bfloat16)

    @pl.when((flag & 1) != 0)
    def _():
        e = grp_ref[j] * GROUP + sub // halves
        t = tbuf[...]
        a = jnp.dot(t, wg_ref[0, 0].astype(jnp.bfloat16), preferred_element_type=jnp.float32)
        u = jnp.dot(t, wu_ref[0, 0].astype(jnp.bfloat16), preferred_element_type=jnp.float32)
        hid = (a / (1.0 + jnp.exp(-a))) * u
        y = jnp.dot(hid.astype(jnp.bfloat16), wd_ref[0, 0].astype(jnp.bfloat16),
                    preferred_element_type=jnp.float32)
        gates = gate_ref[...]
        lane = lax.broadcasted_iota(jnp.int32, gates.shape, 1)
        g_col = jnp.sum(jnp.where(lane == e, gates, 0.0), axis=1, keepdims=True)
        hbuf[...] += g_col * y

    @pl.when((sub == pl.num_programs(1) - 1) & ((flag & 4) != 0))
    def _():
        _row_copies(tok_ref, base, tm, out_hbm, hbuf, sem, to_hbm=True)


def _moe_worklist(grp, tm):
    n = grp.shape[0]
    n_tiles = n // tm
    order = jnp.argsort(grp, stable=True).astype(jnp.int32)
    counts = jnp.sum(grp[None, :] == jnp.arange(N_GROUPS, dtype=jnp.int32)[:, None], axis=1)
    ends = jnp.cumsum(counts)
    first_slot = jnp.arange(n_tiles, dtype=jnp.int32) * tm
    g_lo = jnp.sum(ends[None, :] <= first_slot[:, None], axis=1)
    g_hi = jnp.sum(ends[None, :] <= (first_slot + tm - 1)[:, None], axis=1)
    cnt = g_hi - g_lo + 1
    first_item = jnp.cumsum(cnt) - cnt
    total = first_item[-1] + cnt[-1]
    n_items = n_tiles + N_GROUPS - 1
    item = jnp.arange(n_items, dtype=jnp.int32)
    tile = jnp.minimum(jnp.sum((first_item + cnt)[None, :] <= item[:, None], axis=1), n_tiles - 1)
    valid = item < total
    grp_of = jnp.where(valid, g_lo[tile] + item - first_item[tile], g_hi[n_tiles - 1])
    is_first = valid & (item == first_item[tile])
    is_last = valid & (item == first_item[tile] + cnt[tile] - 1)
    flag = valid.astype(jnp.int32) + 2 * is_first.astype(jnp.int32) + 4 * is_last.astype(jnp.int32)
    return order, tile.astype(jnp.int32), grp_of.astype(jnp.int32), flag


def moe_grouped(h, ffn_g, gate_t, grp, wg, wu, wd, *, layer, tm=MOE_TM, halves=MOE_FF_SPLIT):
    n, d = h.shape
    ff = wg.shape[3] // halves
    order, tile, grp_of, flag = _moe_worklist(grp.reshape(n), tm)
    gates_sorted = jnp.take(gate_t.T, order, axis=0)
    expert = lambda j, sub, tile_ref, grp_ref, flag_ref, tok_ref: grp_ref[j] * GROUP + sub // halves
    grid_spec = pltpu.PrefetchScalarGridSpec(
        num_scalar_prefetch=4,
        grid=(tile.shape[0], GROUP * halves),
        in_specs=[pl.BlockSpec(memory_space=pl.ANY),
                  pl.BlockSpec((1, d), lambda j, sub, *_: (0, 0)),
                  pl.BlockSpec((tm, N_EXPERTS), lambda j, sub, tile_ref, *_: (tile_ref[j], 0)),
                  pl.BlockSpec((1, 1, d, ff),
                               lambda j, sub, *p: (layer, expert(j, sub, *p), 0, sub % halves)),
                  pl.BlockSpec((1, 1, d, ff),
                               lambda j, sub, *p: (layer, expert(j, sub, *p), 0, sub % halves)),
                  pl.BlockSpec((1, 1, ff, d),
                               lambda j, sub, *p: (layer, expert(j, sub, *p), sub % halves, 0))],
        out_specs=pl.BlockSpec(memory_space=pl.ANY),
        scratch_shapes=[pltpu.VMEM((tm, d), jnp.float32),
                        pltpu.VMEM((tm, d), jnp.bfloat16),
                        pltpu.SemaphoreType.DMA(())])
    return pl.pallas_call(
        functools.partial(_moe_kernel, tm=tm, halves=halves),
        grid_spec=grid_spec,
        out_shape=jax.ShapeDtypeStruct((n, d), jnp.float32),
        compiler_params=_params(("arbitrary", "arbitrary"), 48),
        name="moe_grouped",
    )(tile, grp_of, flag, order, h, ffn_g.reshape(1, d), gates_sorted, wg, wu, wd)


def moe_layer(h, ffn_g, w_router_t, router_bias, wg, wu, wd, *, layer):
    gate_t, grp = router(h, ffn_g, w_router_t, router_bias)
    return moe_grouped(h, ffn_g, gate_t, grp, wg, wu, wd, layer=layer)


def _pad_cols(w, width):
    return jnp.pad(w, ((0, 0), (0, width - w.shape[1])))


def _pad_heads(w, heads, real, padded):
    k = w.shape[0]
    w = w.reshape(k, heads, real)
    return jnp.pad(w, ((0, 0), (0, 0), (0, padded - real))).reshape(k, heads * padded)


def kernel(x, attn_norm, ev_w_in, ev_q_a_norm, ev_w_q_b, ev_kv_a_norm, ev_w_kv_b, ev_qn_a, ev_kn_a, ev_qn_b, ev_kn_b, ev_w_o, od_w_qkv, od_qn, od_kn, od_w_o, ffn_norm, w_router, router_bias, w_gate, w_up, w_down):
    batch, seq, d = x.shape
    n = batch * seq
    bf = jnp.bfloat16
    h = x.reshape(n, d)
    w_router_t = w_router.T
    depth = attn_norm.shape[0]
    for layer in range(depth):
        i = layer // 2
        if layer % 2 == 0:
            w_in = _pad_cols(ev_w_in[i], EVEN_IN_PAD).astype(bf)
            proj = norm_matmul(h, attn_norm[layer], w_in)
            qkv_a = prep_a(proj, ev_qn_a[i], ev_kn_a[i], batch=batch, seq=seq)
            out_a = dilated_attention(qkv_a, batch=batch, seq=seq)
            w_qb = _pad_heads(ev_w_q_b[i], B_HEADS, MLA_QK, MLA_QK_PAD).astype(bf)
            qb = norm_matmul(proj, ev_q_a_norm[i], w_qb, col_block=A_QKV // Q_LORA)
            ckv = norm_matmul(proj, ev_kv_a_norm[i], ev_w_kv_b[i].astype(bf),
                              col_block=(A_QKV + Q_LORA) // KV_LORA)
            gq_pad = jnp.pad(ev_qn_b[i], (0, MLA_QK_PAD - MLA_QK)).reshape(1, -1)
            gk_nope = ev_kn_b[i][:HEAD_DIM].reshape(1, -1)
            gk_pe = jnp.pad(ev_kn_b[i][HEAD_DIM:], (0, LANES - ROPE_DIM)).reshape(1, -1)
            q_b, k_b, v_b = prep_b(qb, ckv, proj, gq_pad, gk_nope, gk_pe, rope_tables(seq), seq=seq)
            out_b = flash_attention(q_b, k_b, v_b, batch=batch, seq=seq,
                                    tq=min(FLASH_TQ, seq), ck=FLASH_CK)
            h = matmul_residual([out_a, out_b], ev_w_o[i].astype(bf), h)
        else:
            qkv = norm_matmul(h, attn_norm[layer], od_w_qkv[i].astype(bf))
            q_c, k_c, v_c = prep_c(qkv, od_qn[i], od_kn[i],
                                   alibi_query_columns(seq // MOBA_BLOCK), seq=seq)
            out_c = flash_attention(q_c, k_c, v_c, batch=batch, seq=seq, tq=min(FLASH_TQ, seq),
                                    ck=FLASH_CK)
            h = matmul_residual([out_c], od_w_o[i].astype(bf), h)
        h = moe_layer(h, ffn_norm[layer], w_router_t, router_bias, w_gate, w_up, w_down, layer=layer)
    return h.reshape(batch, seq, d)
```

```python
import functools
import math

import jax
import jax.numpy as jnp
from jax import lax
from jax.experimental import pallas as pl
from jax.experimental.pallas import tpu as pltpu

EPS = 1e-6
NEG = -1e30
LANES = 128
HEAD_DIM = 128
ROPE_DIM = 64
ROPE_THETA = 10000.0
A_HEADS = 8
A_PATTERNS = ((128, 1), (512, 4), (2048, 16))
B_HEADS = 8
Q_LORA = 512
KV_LORA = 256
MLA_QK = 192
MLA_QK_PAD = 256
C_HEADS = 16
MOBA_BLOCK = 256
MOBA_TOPK = 3
N_EXPERTS = 16
N_GROUPS = 4
GROUP = N_EXPERTS // N_GROUPS
TOP_K = 2
EXPERT_FF = 512
A_QKV = 3 * A_HEADS * HEAD_DIM
EVEN_IN_PAD = 4096
KPE_COL = A_QKV + Q_LORA + KV_LORA
MIB = 1024 * 1024
FLASH_TQ = 1024
FLASH_CK = 512
ONES_ROWS = 16
ALIBI_COLS = 6
LOG2E = 1.4426950408889634
MOE_TM = 1024
MOE_FF_SPLIT = 2

_NT = (((1,), (1,)), ((), ()))


def _params(semantics, vmem_mib):
    return pltpu.CompilerParams(dimension_semantics=semantics,
                                vmem_limit_bytes=vmem_mib * MIB)


def _rms_scale(x, width):
    return lax.rsqrt(jnp.sum(x * x, axis=-1, keepdims=True) * (1.0 / width) + EPS)


def _norm_matmul_kernel(x_ref, g_ref, w_ref, o_ref, xn_ref, *, width):
    @pl.when(pl.program_id(1) == 0)
    def _():
        x = x_ref[...]
        xn_ref[...] = (x * _rms_scale(x, width) * g_ref[...]).astype(jnp.bfloat16)

    o_ref[...] = jnp.dot(xn_ref[...], w_ref[...],
                         preferred_element_type=jnp.float32).astype(o_ref.dtype)


def norm_matmul(x, g, w, *, col_block=0, tm=1024, tn=1024):
    n = x.shape[0]
    k, m = w.shape
    return pl.pallas_call(
        functools.partial(_norm_matmul_kernel, width=k),
        grid=(n // tm, m // tn),
        in_specs=[pl.BlockSpec((tm, k), lambda i, j: (i, col_block)),
                  pl.BlockSpec((1, k), lambda i, j: (0, 0)),
                  pl.BlockSpec((k, tn), lambda i, j: (0, j))],
        out_specs=pl.BlockSpec((tm, tn), lambda i, j: (i, j)),
        out_shape=jax.ShapeDtypeStruct((n, m), jnp.float32),
        scratch_shapes=[pltpu.VMEM((tm, k), jnp.bfloat16)],
        compiler_params=_params(("parallel", "arbitrary"), 48),
        name="norm_matmul",
    )(x, g.reshape(1, k), w)


def _matmul_res_kernel(*refs, n_a):
    a_refs, w_refs = refs[:n_a], refs[n_a:2 * n_a]
    r_ref, o_ref = refs[2 * n_a], refs[2 * n_a + 1]
    acc = r_ref[...]
    for a_ref, w_ref in zip(a_refs, w_refs):
        acc = acc + jnp.dot(a_ref[...], w_ref[...], preferred_element_type=jnp.float32)
    o_ref[...] = acc


def matmul_residual(a_list, w, res, *, tm=512, tn=1024):
    n, m = res.shape
    n_a = len(a_list)
    ka = a_list[0].shape[1]
    in_specs = [pl.BlockSpec((tm, ka), lambda i, j: (i, 0)) for _ in a_list]
    in_specs += [pl.BlockSpec((ka, tn), functools.partial(lambda i, j, p: (p, j), p=p))
                 for p in range(n_a)]
    in_specs += [pl.BlockSpec((tm, tn), lambda i, j: (i, j))]
    return pl.pallas_call(
        functools.partial(_matmul_res_kernel, n_a=n_a),
        grid=(n // tm, m // tn),
        in_specs=in_specs,
        out_specs=pl.BlockSpec((tm, tn), lambda i, j: (i, j)),
        out_shape=jax.ShapeDtypeStruct((n, m), jnp.float32),
        compiler_params=_params(("parallel", "parallel"), 40),
        name="matmul_residual",
    )(*a_list, *([w] * n_a), res)


def _prep_a_kernel(q_ref, k_ref, v_ref, gq_ref, gk_ref, *refs, scale):
    n_pat = len(A_PATTERNS)
    outs, (qs, ks, vs) = refs[:3 * n_pat], refs[3 * n_pat:]
    heads = [slice(h * HEAD_DIM, (h + 1) * HEAD_DIM) for h in range(A_HEADS)]
    for h, sl in enumerate(heads):
        q = q_ref[:, sl]
        qs[h] = q * _rms_scale(q, HEAD_DIM) * gq_ref[...] * scale
        k = k_ref[:, sl]
        ks[h] = k * _rms_scale(k, HEAD_DIM) * gk_ref[...]
        vs[h] = v_ref[:, sl]
    rows = qs.shape[1]
    for p, (_, dil) in enumerate(A_PATTERNS):
        for src, dst in zip((qs, ks, vs), outs[3 * p:3 * p + 3]):
            for r in range(dil):
                for h, sl in enumerate(heads):
                    dst[0, r, :, sl] = src[h, pl.ds(r, rows // dil, stride=dil), :].astype(
                        jnp.bfloat16)


def prep_a(proj, gq, gk, *, batch, seq, tm=512):
    n = proj.shape[0]
    w = A_HEADS * HEAD_DIM
    per_seq = seq // tm
    blk = lambda c: pl.BlockSpec((tm, w), functools.partial(lambda i, c: (i, c), c=c))
    gspec = pl.BlockSpec((1, HEAD_DIM), lambda i: (0, 0))
    out_specs, out_shape = [], []
    for _, dil in A_PATTERNS:
        spec = pl.BlockSpec((1, dil, tm // dil, w), lambda i: (i // per_seq, 0, i % per_seq, 0))
        shape = jax.ShapeDtypeStruct((batch, dil, seq // dil, w), jnp.bfloat16)
        out_specs += [spec] * 3
        out_shape += [shape] * 3
    outs = pl.pallas_call(
        functools.partial(_prep_a_kernel, scale=HEAD_DIM ** -0.5),
        grid=(n // tm,),
        in_specs=[blk(0), blk(1), blk(2), gspec, gspec],
        out_specs=out_specs,
        out_shape=out_shape,
        scratch_shapes=[pltpu.VMEM((A_HEADS, tm, HEAD_DIM), jnp.float32)] * 3,
        compiler_params=_params(("parallel",), 48),
        name="prep_a",
    )(proj, proj, proj, gq.reshape(1, -1), gk.reshape(1, -1))
    return [outs[3 * p:3 * p + 3] for p in range(len(A_PATTERNS))]


def _dilated_kernel(q_ref, kc_ref, vc_ref, o_ref, lse_ref, kprev, vprev, *, dil):
    band = HEAD_DIM
    n = pl.program_id(2)

    @pl.when(n == 0)
    def _():
        kprev[...] = jnp.zeros_like(kprev)
        vprev[...] = jnp.zeros_like(vprev)

    qi = lax.broadcasted_iota(jnp.int32, (band, band), 0)
    ki = lax.broadcasted_iota(jnp.int32, (band, band), 1)
    dist_c = (qi - ki).astype(jnp.float32)
    dist_p = dist_c + float(band)
    valid_c = qi >= ki
    valid_p = jnp.logical_and(ki >= qi, n > 0)
    heads = [slice(h * HEAD_DIM, (h + 1) * HEAD_DIM) for h in range(A_HEADS)]
    logits = []
    for sl in heads:
        q = q_ref[0, 0, :, sl]
        logits.append((lax.dot_general(q, kc_ref[0, 0, :, sl], _NT, preferred_element_type=jnp.float32),
                       lax.dot_general(q, kprev[:, sl], _NT, preferred_element_type=jnp.float32)))
    probs = []
    lse_tile = jnp.zeros((band, LANES), jnp.float32)
    for h, (sc, sp) in enumerate(logits):
        slope = 2.0 ** (-8.0 * (h + 1) / A_HEADS) * dil
        sc = jnp.where(valid_c, sc - slope * dist_c, NEG)
        sp = jnp.where(valid_p, sp - slope * dist_p, NEG)
        m = jnp.maximum(jnp.max(sc, axis=1, keepdims=True), jnp.max(sp, axis=1, keepdims=True))
        pc = jnp.exp(sc - m)
        pp = jnp.exp(sp - m)
        l = jnp.sum(pc, axis=1, keepdims=True) + jnp.sum(pp, axis=1, keepdims=True)
        probs.append((pc.astype(jnp.bfloat16), pp.astype(jnp.bfloat16), l))
        lse_tile = jnp.where(ki == h, m + jnp.log(l), lse_tile)
    lse_ref[0, 0] = lse_tile
    for sl, (pc, pp, l) in zip(heads, probs):
        acc = jnp.dot(pc, vc_ref[0, 0, :, sl], preferred_element_type=jnp.float32)
        acc = acc + jnp.dot(pp, vprev[:, sl], preferred_element_type=jnp.float32)
        o_ref[0, 0, :, sl] = acc / l
    kprev[...] = kc_ref[0, 0]
    vprev[...] = vc_ref[0, 0]


def dilated_pattern(q, k, v, *, dil):
    batch, _, sub, w = q.shape
    band = HEAD_DIM
    cur = pl.BlockSpec((1, 1, band, w), lambda b, r, n: (b, r, n, 0))
    return pl.pallas_call(
        functools.partial(_dilated_kernel, dil=dil),
        grid=(batch, dil, sub // band),
        in_specs=[cur, cur, cur],
        out_specs=[cur, pl.BlockSpec((1, 1, band, LANES), lambda b, r, n: (b, r, n, 0))],
        out_shape=[jax.ShapeDtypeStruct((batch, dil, sub, w), jnp.float32),
                   jax.ShapeDtypeStruct((batch, dil, sub, LANES), jnp.float32)],
        scratch_shapes=[pltpu.VMEM((band, w), jnp.bfloat16)] * 2,
        compiler_params=_params(("parallel", "parallel", "arbitrary"), 40),
        name=f"dilated_d{dil}",
    )(q, k, v)


def _dilated_merge_kernel(*refs):
    n_pat = len(A_PATTERNS)
    ins, out_ref, scratch = refs[:2 * n_pat], refs[2 * n_pat], refs[2 * n_pat + 1:]
    rows = out_ref.shape[0]
    heads = [slice(h * HEAD_DIM, (h + 1) * HEAD_DIM) for h in range(A_HEADS)]
    outs, lses = [], []
    for p, (_, dil) in enumerate(A_PATTERNS):
        o_ref, l_ref = ins[2 * p], ins[2 * p + 1]
        if dil == 1:
            outs.append(lambda h, o_ref=o_ref: o_ref[0, 0, :, heads[h]])
            lses.append(l_ref[0, 0])
            continue
        o_nat, l_nat = scratch[2 * p], scratch[2 * p + 1]
        for r in range(dil):
            for h, sl in enumerate(heads):
                o_nat[h, pl.ds(r, rows // dil, stride=dil), :] = o_ref[0, r, :, sl]
            l_nat[pl.ds(r, rows // dil, stride=dil), :] = l_ref[0, r]
        outs.append(lambda h, o_nat=o_nat: o_nat[h])
        lses.append(l_nat[...])
    for h, sl in enumerate(heads):
        lse_h = [l[:, h:h + 1] for l in lses]
        top = functools.reduce(jnp.maximum, lse_h)
        wts = [jnp.exp(l - top) for l in lse_h]
        num = sum(wt * o(h) for wt, o in zip(wts, outs))
        out_ref[:, sl] = (num / sum(wts)).astype(out_ref.dtype)


def dilated_attention(qkv_by_pattern, *, batch, seq, tm=512):
    w = A_HEADS * HEAD_DIM
    per_seq = seq // tm
    ins, in_specs, scratch = [], [], []
    for (q, k, v), (window, dil) in zip(qkv_by_pattern, A_PATTERNS):
        assert window // dil == HEAD_DIM
        o, lse = dilated_pattern(q, k, v, dil=dil)
        ins += [o, lse]
        in_specs += [pl.BlockSpec((1, dil, tm // dil, w), lambda i: (i // per_seq, 0, i % per_seq, 0)),
                     pl.BlockSpec((1, dil, tm // dil, LANES),
                                  lambda i: (i // per_seq, 0, i % per_seq, 0))]
        scratch += [pltpu.VMEM((A_HEADS, tm, HEAD_DIM), jnp.float32),
                    pltpu.VMEM((tm, LANES), jnp.float32)]
    n = batch * seq
    return pl.pallas_call(
        _dilated_merge_kernel,
        grid=(n // tm,),
        in_specs=in_specs,
        out_specs=pl.BlockSpec((tm, w), lambda i: (i, 0)),
        out_shape=jax.ShapeDtypeStruct((n, w), jnp.bfloat16),
        scratch_shapes=scratch,
        compiler_params=_params(("parallel",), 48),
        name="dilated_merge",
    )(*ins)


def _rope(t, cos, sin_lo, sin_hi):
    return (t * cos + pltpu.roll(t, LANES - ROPE_DIM // 2, 1) * sin_lo
            + pltpu.roll(t, ROPE_DIM // 2, 1) * sin_hi)


def _prep_b_kernel(qb_ref, ckv_ref, kpe_ref, gq_ref, gkn_ref, gkp_ref, cos_ref, slo_ref, shi_ref,
                   qo_ref, ko_ref, vo_ref, *, scale):
    cos, slo, shi = cos_ref[...], slo_ref[...], shi_ref[...]
    kpe = kpe_ref[...]
    kpe_ss = jnp.sum(kpe * kpe, axis=-1, keepdims=True)
    kpe_rot = _rope(kpe * gkp_ref[...], cos, slo, shi)
    for h in range(B_HEADS):
        lo = slice(h * MLA_QK_PAD, h * MLA_QK_PAD + HEAD_DIM)
        hi = slice(h * MLA_QK_PAD + HEAD_DIM, (h + 1) * MLA_QK_PAD)
        q_n, q_p = qb_ref[:, lo], qb_ref[:, hi]
        ss = jnp.sum(q_n * q_n, axis=-1, keepdims=True) + jnp.sum(q_p * q_p, axis=-1, keepdims=True)
        rq = lax.rsqrt(ss * (1.0 / MLA_QK) + EPS) * scale
        qo_ref[h, :, :HEAD_DIM] = (q_n * rq * gq_ref[:, :HEAD_DIM]).astype(jnp.bfloat16)
        qo_ref[h, :, HEAD_DIM:] = _rope(q_p * rq * gq_ref[:, HEAD_DIM:], cos, slo, shi
                                        ).astype(jnp.bfloat16)
        k_n = ckv_ref[:, lo]
        ss = jnp.sum(k_n * k_n, axis=-1, keepdims=True) + kpe_ss
        rk = lax.rsqrt(ss * (1.0 / MLA_QK) + EPS)
        ko_ref[h, :, :HEAD_DIM] = (k_n * rk * gkn_ref[...]).astype(jnp.bfloat16)
        ko_ref[h, :, HEAD_DIM:] = (kpe_rot * rk).astype(jnp.bfloat16)
        vo_ref[h, :HEAD_DIM, :] = ckv_ref[:, hi].T.astype(jnp.bfloat16)
        vo_ref[h, HEAD_DIM:, :] = jnp.ones((ONES_ROWS, ckv_ref.shape[0]), jnp.bfloat16)


def prep_b(qb, ckv, proj, gq_pad, gk_nope, gk_pe_pad, rope_tabs, *, seq, tm=256):
    n = qb.shape[0]
    wide = B_HEADS * MLA_QK_PAD
    nblk = seq // tm
    row = lambda c: pl.BlockSpec((tm, wide), lambda i: (i, 0))
    tab = pl.BlockSpec((tm, LANES), lambda i: (i % nblk, 0))
    return pl.pallas_call(
        functools.partial(_prep_b_kernel, scale=MLA_QK ** -0.5 * LOG2E),
        grid=(n // tm,),
        in_specs=[row(0), row(0),
                  pl.BlockSpec((tm, LANES), lambda i: (i, KPE_COL // LANES)),
                  pl.BlockSpec((1, MLA_QK_PAD), lambda i: (0, 0)),
                  pl.BlockSpec((1, LANES), lambda i: (0, 0)),
                  pl.BlockSpec((1, LANES), lambda i: (0, 0)),
                  tab, tab, tab],
        out_specs=[pl.BlockSpec((B_HEADS, tm, MLA_QK_PAD), lambda i: (0, i, 0)),
                   pl.BlockSpec((B_HEADS, tm, MLA_QK_PAD), lambda i: (0, i, 0)),
                   pl.BlockSpec((B_HEADS, HEAD_DIM + ONES_ROWS, tm), lambda i: (0, 0, i))],
        out_shape=[jax.ShapeDtypeStruct((B_HEADS, n, MLA_QK_PAD), jnp.bfloat16),
                   jax.ShapeDtypeStruct((B_HEADS, n, MLA_QK_PAD), jnp.bfloat16),
                   jax.ShapeDtypeStruct((B_HEADS, HEAD_DIM + ONES_ROWS, n), jnp.bfloat16)],
        compiler_params=_params(("parallel",), 40),
        name="prep_b",
    )(qb, ckv, proj, gq_pad, gk_nope, gk_pe_pad, *rope_tabs)


def rope_tables(seq):
    half = ROPE_DIM // 2
    inv = ROPE_THETA ** (-jnp.arange(0, ROPE_DIM, 2, dtype=jnp.float32) / ROPE_DIM)
    ang = jnp.arange(seq, dtype=jnp.float32)[:, None] * inv[None, :]
    cos, sin = jnp.cos(ang), jnp.sin(ang)
    z = jnp.zeros((seq, half), jnp.float32)
    pad = jnp.zeros((seq, LANES - ROPE_DIM), jnp.float32)
    return (jnp.concatenate([cos, cos, pad], axis=1),
            jnp.concatenate([-sin, z, pad], axis=1),
            jnp.concatenate([z, sin, pad], axis=1))


def _flash_kernel(q_ref, k_ref, vt_ref, o_ref, m_ref, acc_ref, s_buf, p_buf, alpha_buf, *, tq, ck, dv):
    i = pl.program_id(2)
    nb = tq // ck
    n_past = i * nb
    m_ref[...] = jnp.full_like(m_ref, NEG)
    acc_ref[...] = jnp.zeros_like(acc_ref)

    def scores(j, r0):
        start = pl.multiple_of(j * ck, ck)
        return lax.dot_general(k_ref[0, pl.ds(start, ck), :], q_ref[0, r0:, :], _NT,
                               preferred_element_type=jnp.float32)

    def fold(j, r0, s):
        start = pl.multiple_of(j * ck, ck)
        probs, alphas = [], []
        for c0 in range(0, tq - r0, LANES):
            cols = slice(r0 + c0, r0 + c0 + LANES)
            s_c = s[:, c0:c0 + LANES]
            m_prev = m_ref[:, cols]
            m_new = jnp.maximum(m_prev, jnp.max(s_c, axis=0, keepdims=True))
            m_ref[:, cols] = m_new
            alphas.append(jnp.exp2(m_prev - m_new))
            probs.append(jnp.exp2(s_c - m_new).astype(jnp.bfloat16))
        p = jnp.concatenate(probs, axis=1)
        alpha = jnp.concatenate(alphas, axis=1)
        acc_ref[:, r0:] = alpha * acc_ref[:, r0:] + jnp.dot(
            vt_ref[0, :, pl.ds(start, ck)], p, preferred_element_type=jnp.float32)

    def stage(c, slot):
        other = 1 - slot
        s_buf[other] = scores(c + 1, 0)
        start = pl.multiple_of(jnp.maximum(c - 1, 0) * ck, ck)
        acc_ref[...] = alpha_buf[other] * acc_ref[...] + jnp.dot(
            vt_ref[0, :, pl.ds(start, ck)], p_buf[other], preferred_element_type=jnp.float32)
        for c0 in range(0, tq, LANES):
            cols = slice(c0, c0 + LANES)
            s_c = s_buf[slot, :, cols]
            m_prev = m_ref[:, cols]
            m_new = jnp.maximum(m_prev, jnp.max(s_c, axis=0, keepdims=True))
            m_ref[:, cols] = m_new
            alpha_buf[slot, :, cols] = jnp.exp2(m_prev - m_new)
            p_buf[slot, :, cols] = jnp.exp2(s_c - m_new).astype(jnp.bfloat16)

    def pair(jj, carry):
        stage(2 * jj, 0)
        stage(2 * jj + 1, 1)
        return carry

    s_buf[0] = scores(0, 0)
    p_buf[1] = jnp.zeros((ck, tq), jnp.bfloat16)
    alpha_buf[1] = jnp.ones((1, tq), jnp.float32)
    lax.fori_loop(0, n_past // 2, pair, 0)
    start = pl.multiple_of(jnp.maximum(n_past - 1, 0) * ck, ck)
    acc_ref[...] = alpha_buf[1] * acc_ref[...] + jnp.dot(
        vt_ref[0, :, pl.ds(start, ck)], p_buf[1], preferred_element_type=jnp.float32)
    s = s_buf[0]
    band_logits = [s] + [scores(n_past + jb, jb * ck) for jb in range(1, nb)]
    for jb, s in enumerate(band_logits):
        j = n_past + jb
        r0 = jb * ck
        ki = lax.broadcasted_iota(jnp.int32, (ck, ck), 0)
        qi = lax.broadcasted_iota(jnp.int32, (ck, ck), 1)
        own = jnp.where(qi >= ki, s[:, :ck], NEG)
        fold(j, r0, own if tq - r0 == ck else jnp.concatenate([own, s[:, ck:]], axis=1))
    acc = acc_ref[...]
    o_ref[...] = (acc[:dv] / acc[dv:dv + 1]).T.astype(o_ref.dtype)


def flash_attention(q, k, vt, *, batch, seq, tq, ck):
    heads, n, dq = q.shape
    dvp = vt.shape[1]
    dv = dvp - ONES_ROWS
    nq = seq // tq
    return pl.pallas_call(
        functools.partial(_flash_kernel, tq=tq, ck=ck, dv=dv),
        grid=(batch, heads, nq),
        in_specs=[pl.BlockSpec((1, tq, dq), lambda b, h, i: (h, b * nq + i, 0)),
                  pl.BlockSpec((1, seq, dq), lambda b, h, i: (h * batch + b, 0, 0)),
                  pl.BlockSpec((1, dvp, seq), lambda b, h, i: (h, 0, b))],
        out_specs=pl.BlockSpec((tq, dv), lambda b, h, i: (b * nq + i, h)),
        out_shape=jax.ShapeDtypeStruct((n, heads * dv), jnp.bfloat16),
        scratch_shapes=[pltpu.VMEM((1, tq), jnp.float32),
                        pltpu.VMEM((dvp, tq), jnp.float32),
                        pltpu.VMEM((2, ck, tq), jnp.float32),
                        pltpu.VMEM((2, ck, tq), jnp.bfloat16),
                        pltpu.VMEM((2, 1, tq), jnp.float32)],
        compiler_params=_params(("parallel", "parallel", "arbitrary"), 48),
        name=f"flash_h{heads}",
    )(q, k.reshape(heads * batch, seq, dq), vt)


def _prep_c_kernel(q_ref, k_ref, v_ref, gq_ref, gk_ref, alibi_ref, qo_ref, ko_ref, vo_ref,
                   kmean_ref, *, blocks_per_seq, scale):
    i = pl.program_id(0)
    bq = i % blocks_per_seq

    @pl.when(bq == 0)
    def _():
        kmean_ref[...] = jnp.zeros_like(kmean_ref)

    rows = q_ref.shape[0]
    nblk = blocks_per_seq
    blk = lax.broadcasted_iota(jnp.int32, (nblk, rows), 0)
    lane = lax.broadcasted_iota(jnp.int32, (rows, LANES), 1)
    kpos = bq * rows + lax.broadcasted_iota(jnp.int32, (rows, LANES), 0)
    pos_part = jnp.where((lane - nblk) % 2 == 0, jnp.right_shift(kpos, 7), kpos & (LANES - 1))
    k_extra = jnp.where(lane < nblk, (lane == bq).astype(jnp.float32),
                        jnp.where(lane < nblk + ALIBI_COLS, pos_part.astype(jnp.float32), 0.0))
    k_extra = k_extra.astype(jnp.bfloat16)
    ones = jnp.ones((ONES_ROWS, rows), jnp.bfloat16)
    gates = []
    for h in range(C_HEADS):
        sl = slice(h * HEAD_DIM, (h + 1) * HEAD_DIM)
        q = q_ref[:, sl]
        qn = q * _rms_scale(q, HEAD_DIM) * gq_ref[...]
        k = k_ref[:, sl]
        kn = k * _rms_scale(k, HEAD_DIM) * gk_ref[...]
        qo_ref[h, :, :HEAD_DIM] = (qn * scale).astype(jnp.bfloat16)
        ko_ref[h, :, :HEAD_DIM] = kn.astype(jnp.bfloat16)
        ko_ref[h, :, HEAD_DIM:] = k_extra
        vo_ref[h, :HEAD_DIM, :] = v_ref[:, sl].T.astype(jnp.bfloat16)
        vo_ref[h, HEAD_DIM:, :] = ones
        gates.append(lax.dot_general(kmean_ref[h], qn, _NT, precision=lax.Precision.HIGHEST,
                                     preferred_element_type=jnp.float32))
        kmean_ref[h, pl.ds(bq, 1), :] = jnp.mean(kn, axis=0, keepdims=True)
    for h, gate in enumerate(gates):
        gate = jnp.where(blk < bq, gate, NEG)
        allowed = blk == bq
        for r in range(MOBA_TOPK):
            best = jnp.max(gate, axis=0, keepdims=True)
            idx = jnp.min(jnp.where(gate == best, blk, nblk), axis=0, keepdims=True)
            allowed = allowed | ((blk == idx) & (r < bq))
            gate = jnp.where(blk == idx, 2.0 * NEG, gate)
        block_bias = jnp.where(allowed, 0.0, NEG)
        block_bias = jnp.concatenate(
            [block_bias, jnp.zeros((LANES - nblk, rows), jnp.float32)], axis=0)
        qo_ref[h, :, HEAD_DIM:] = (block_bias.T + alibi_ref[h]).astype(jnp.bfloat16)


def prep_c(qkv, gq, gk, alibi_q, *, seq):
    n = qkv.shape[0]
    tm = MOBA_BLOCK
    w = C_HEADS * HEAD_DIM
    blocks_per_seq = seq // tm
    assert blocks_per_seq + ALIBI_COLS <= LANES
    blk = lambda c: pl.BlockSpec((tm, w), functools.partial(lambda i, c: (i, c), c=c))
    gspec = pl.BlockSpec((1, HEAD_DIM), lambda i: (0, 0))
    hm = pl.BlockSpec((C_HEADS, tm, 2 * HEAD_DIM), lambda i: (0, i, 0))
    hm_shape = jax.ShapeDtypeStruct((C_HEADS, n, 2 * HEAD_DIM), jnp.bfloat16)
    dvp = HEAD_DIM + ONES_ROWS
    return pl.pallas_call(
        functools.partial(_prep_c_kernel, blocks_per_seq=blocks_per_seq,
                          scale=HEAD_DIM ** -0.5 * LOG2E),
        grid=(n // tm,),
        in_specs=[blk(0), blk(1), blk(2), gspec, gspec,
                  pl.BlockSpec((C_HEADS, 1, LANES), lambda i: (0, 0, 0))],
        out_specs=[hm, hm, pl.BlockSpec((C_HEADS, dvp, tm), lambda i: (0, 0, i))],
        out_shape=[hm_shape, hm_shape, jax.ShapeDtypeStruct((C_HEADS, dvp, n), jnp.bfloat16)],
        scratch_shapes=[pltpu.VMEM((C_HEADS, blocks_per_seq, HEAD_DIM), jnp.float32)],
        compiler_params=_params(("arbitrary",), 40),
        name="prep_c",
    )(qkv, qkv, qkv, gq.reshape(1, -1), gk.reshape(1, -1), alibi_q)


def alibi_query_columns(blocks_per_seq):
    sigma = 2.0 ** (-8.0 * jnp.arange(1, C_HEADS + 1, dtype=jnp.float32) / C_HEADS) * LOG2E
    pieces = []
    rest = sigma
    for _ in range(ALIBI_COLS // 2):
        piece = rest.astype(jnp.bfloat16).astype(jnp.float32)
        pieces += [piece * float(LANES), piece]
        rest = rest - piece
    cols = jnp.stack(pieces, axis=1)
    cols = jnp.pad(cols, ((0, 0), (blocks_per_seq, LANES - blocks_per_seq - ALIBI_COLS)))
    return cols.reshape(C_HEADS, 1, LANES)


def _router_kernel(h_ref, g_ref, wr_ref, b_ref, gate_ref, grp_ref):
    x = h_ref[...]
    t = x * _rms_scale(x, x.shape[1]) * g_ref[...]
    logits = lax.dot_general(wr_ref[...], t, _NT, precision=lax.Precision.HIGHEST,
                             preferred_element_type=jnp.float32)
    scores = 1.0 / (1.0 + jnp.exp(-logits))
    biased = scores + b_ref[...]
    s = [scores[e:e + 1, :] for e in range(N_EXPERTS)]
    b = [biased[e:e + 1, :] for e in range(N_EXPERTS)]
    best_val, best_grp = None, None
    for g in range(N_GROUPS):
        mem = b[g * GROUP:(g + 1) * GROUP]
        top2 = None
        for x1 in range(GROUP):
            for x2 in range(x1 + 1, GROUP):
                pair = mem[x1] + mem[x2]
                top2 = pair if top2 is None else jnp.maximum(top2, pair)
        if g == 0:
            best_val, best_grp = top2, jnp.zeros_like(top2, dtype=jnp.int32)
        else:
            better = top2 > best_val
            best_grp = jnp.where(better, g, best_grp)
            best_val = jnp.where(better, top2, best_val)
    picked = []
    for e in range(N_EXPERTS):
        g = e // GROUP
        ahead = jnp.zeros_like(best_grp)
        for o in range(g * GROUP, (g + 1) * GROUP):
            if o == e:
                continue
            wins = (b[o] > b[e]) | ((b[o] == b[e]) & (o < e))
            ahead = ahead + wins.astype(jnp.int32)
        picked.append(jnp.where((best_grp == g) & (ahead < TOP_K), s[e], 0.0))
    total = picked[0]
    for e in range(1, N_EXPERTS):
        total = total + picked[e]
    for e in range(N_EXPERTS):
        gate_ref[e:e + 1, :] = picked[e] / total
    grp_ref[...] = best_grp


def router(h, g, w_router_t, bias, *, tm=512):
    n, d = h.shape
    return pl.pallas_call(
        _router_kernel,
        grid=(n // tm,),
        in_specs=[pl.BlockSpec((tm, d), lambda i: (i, 0)),
                  pl.BlockSpec((1, d), lambda i: (0, 0)),
                  pl.BlockSpec((N_EXPERTS, d), lambda i: (0, 0)),
                  pl.BlockSpec((N_EXPERTS, 1), lambda i: (0, 0))],
        out_specs=[pl.BlockSpec((N_EXPERTS, tm), lambda i: (0, i)),
                   pl.BlockSpec((1, tm), lambda i: (0, i))],
        out_shape=[jax.ShapeDtypeStruct((N_EXPERTS, n), jnp.float32),
                   jax.ShapeDtypeStruct((1, n), jnp.int32)],
        compiler_params=_params(("parallel",), 40),
        name="router",
    )(h, g.reshape(1, d), w_router_t, bias.reshape(N_EXPERTS, 1))


def _row_copies(tok_ref, base, rows, hbm_ref, buf_ref, sem, to_hbm):
    def copy(r):
        hbm_row = hbm_ref.at[pl.ds(tok_ref[base + r], 1), :]
        buf_row = buf_ref.at[pl.ds(r, 1), :]
        src, dst = (buf_row, hbm_row) if to_hbm else (hbm_row, buf_row)
        return pltpu.make_async_copy(src, dst, sem)

    def start(r, carry):
        copy(r).start()
        return carry

    def wait(r, carry):
        copy(r).wait()
        return carry

    lax.fori_loop(0, rows, start, 0, unroll=8)
    lax.fori_loop(0, rows, wait, 0, unroll=8)


def _moe_kernel(tile_ref, grp_ref, flag_ref, tok_ref, h_hbm, g_ref, gate_ref, wg_ref, wu_ref, wd_ref,
                out_hbm, hbuf, tbuf, sem, *, tm, halves):
    j = pl.program_id(0)
    sub = pl.program_id(1)
    flag = flag_ref[j]
    base = tile_ref[j] * tm

    @pl.when((sub == 0) & ((flag & 2) != 0))
    def _():
        _row_copies(tok_ref, base, tm, h_hbm, hbuf, sem, to_hbm=False)
        x = hbuf[...]
        tbuf[...] = (x * _rms_scale(x, x.shape[1]) * g_ref[...]).astype(jnp.bfloat16)

    @pl.when((flag & 1) != 0)
    def _():
        e = grp_ref[j] * GROUP + sub // halves
        t = tbuf[...]
        a = jnp.dot(t, wg_ref[0, 0].astype(jnp.bfloat16), preferred_element_type=jnp.float32)
        u = jnp.dot(t, wu_ref[0, 0].astype(jnp.bfloat16), preferred_element_type=jnp.float32)
        hid = (a / (1.0 + jnp.exp(-a))) * u
        y = jnp.dot(hid.astype(jnp.bfloat16), wd_ref[0, 0].astype(jnp.bfloat16),
                    preferred_element_type=jnp.float32)
        gates = gate_ref[...]
        lane = lax.broadcasted_iota(jnp.int32, gates.shape, 1)
        g_col = jnp.sum(jnp.where(lane == e, gates, 0.0), axis=1, keepdims=True)
        hbuf[...] += g_col * y

    @pl.when((sub == pl.num_programs(1) - 1) & ((flag & 4) != 0))
    def _():
        _row_copies(tok_ref, base, tm, out_hbm, hbuf, sem, to_hbm=True)


def _moe_worklist(grp, tm):
    n = grp.shape[0]
    n_tiles = n // tm
    order = jnp.argsort(grp, stable=True).astype(jnp.int32)
    counts = jnp.sum(grp[None, :] == jnp.arange(N_GROUPS, dtype=jnp.int32)[:, None], axis=1)
    ends = jnp.cumsum(counts)
    first_slot = jnp.arange(n_tiles, dtype=jnp.int32) * tm
    g_lo = jnp.sum(ends[None, :] <= first_slot[:, None], axis=1)
    g_hi = jnp.sum(ends[None, :] <= (first_slot + tm - 1)[:, None], axis=1)
    cnt = g_hi - g_lo + 1
    first_item = jnp.cumsum(cnt) - cnt
    total = first_item[-1] + cnt[-1]
    n_items = n_tiles + N_GROUPS - 1
    item = jnp.arange(n_items, dtype=jnp.int32)
    tile = jnp.minimum(jnp.sum((first_item + cnt)[None, :] <= item[:, None], axis=1), n_tiles - 1)
    valid = item < total
    grp_of = jnp.where(valid, g_lo[tile] + item - first_item[tile], g_hi[n_tiles - 1])
    is_first = valid & (item == first_item[tile])
    is_last = valid & (item == first_item[tile] + cnt[tile] - 1)
    flag = valid.astype(jnp.int32) + 2 * is_first.astype(jnp.int32) + 4 * is_last.astype(jnp.int32)
    return order, tile.astype(jnp.int32), grp_of.astype(jnp.int32), flag


def moe_grouped(h, ffn_g, gate_t, grp, wg, wu, wd, *, layer, tm=MOE_TM, halves=MOE_FF_SPLIT):
    n, d = h.shape
    ff = wg.shape[3] // halves
    order, tile, grp_of, flag = _moe_worklist(grp.reshape(n), tm)
    gates_sorted = jnp.take(gate_t.T, order, axis=0)
    expert = lambda j, sub, tile_ref, grp_ref, flag_ref, tok_ref: grp_ref[j] * GROUP + sub // halves
    grid_spec = pltpu.PrefetchScalarGridSpec(
        num_scalar_prefetch=4,
        grid=(tile.shape[0], GROUP * halves),
        in_specs=[pl.BlockSpec(memory_space=pl.ANY),
                  pl.BlockSpec((1, d), lambda j, sub, *_: (0, 0)),
                  pl.BlockSpec((tm, N_EXPERTS), lambda j, sub, tile_ref, *_: (tile_ref[j], 0)),
                  pl.BlockSpec((1, 1, d, ff),
                               lambda j, sub, *p: (layer, expert(j, sub, *p), 0, sub % halves)),
                  pl.BlockSpec((1, 1, d, ff),
                               lambda j, sub, *p: (layer, expert(j, sub, *p), 0, sub % halves)),
                  pl.BlockSpec((1, 1, ff, d),
                               lambda j, sub, *p: (layer, expert(j, sub, *p), sub % halves, 0))],
        out_specs=pl.BlockSpec(memory_space=pl.ANY),
        scratch_shapes=[pltpu.VMEM((tm, d), jnp.float32),
                        pltpu.VMEM((tm, d), jnp.bfloat16),
                        pltpu.SemaphoreType.DMA(())])
    return pl.pallas_call(
        functools.partial(_moe_kernel, tm=tm, halves=halves),
        grid_spec=grid_spec,
        out_shape=jax.ShapeDtypeStruct((n, d), jnp.float32),
        compiler_params=_params(("arbitrary", "arbitrary"), 48),
        name="moe_grouped",
    )(tile, grp_of, flag, order, h, ffn_g.reshape(1, d), gates_sorted, wg, wu, wd)


def moe_layer(h, ffn_g, w_router_t, router_bias, wg, wu, wd, *, layer):
    gate_t, grp = router(h, ffn_g, w_router_t, router_bias)
    return moe_grouped(h, ffn_g, gate_t, grp, wg, wu, wd, layer=layer)


def _pad_cols(w, width):
    return jnp.pad(w, ((0, 0), (0, width - w.shape[1])))


def _pad_heads(w, heads, real, padded):
    k = w.shape[0]
    w = w.reshape(k, heads, real)
    return jnp.pad(w, ((0, 0), (0, 0), (0, padded - real))).reshape(k, heads * padded)


def kernel(x, attn_norm, ev_w_in, ev_q_a_norm, ev_w_q_b, ev_kv_a_norm, ev_w_kv_b, ev_qn_a, ev_kn_a, ev_qn_b, ev_kn_b, ev_w_o, od_w_qkv, od_qn, od_kn, od_w_o, ffn_norm, w_router, router_bias, w_gate, w_up, w_down):
    batch, seq, d = x.shape
    n = batch * seq
    bf = jnp.bfloat16
    h = x.reshape(n, d)
    w_router_t = w_router.T
    depth = attn_norm.shape[0]
    for layer in range(depth):
        i = layer // 2
        if layer % 2 == 0:
            w_in = _pad_cols(ev_w_in[i], EVEN_IN_PAD).astype(bf)
            proj = norm_matmul(h, attn_norm[layer], w_in)
            qkv_a = prep_a(proj, ev_qn_a[i], ev_kn_a[i], batch=batch, seq=seq)
            out_a = dilated_attention(qkv_a, batch=batch, seq=seq)
            w_qb = _pad_heads(ev_w_q_b[i], B_HEADS, MLA_QK, MLA_QK_PAD).astype(bf)
            qb = norm_matmul(proj, ev_q_a_norm[i], w_qb, col_block=A_QKV // Q_LORA)
            ckv = norm_matmul(proj, ev_kv_a_norm[i], ev_w_kv_b[i].astype(bf),
                              col_block=(A_QKV + Q_LORA) // KV_LORA)
            gq_pad = jnp.pad(ev_qn_b[i], (0, MLA_QK_PAD - MLA_QK)).reshape(1, -1)
            gk_nope = ev_kn_b[i][:HEAD_DIM].reshape(1, -1)
            gk_pe = jnp.pad(ev_kn_b[i][HEAD_DIM:], (0, LANES - ROPE_DIM)).reshape(1, -1)
            q_b, k_b, v_b = prep_b(qb, ckv, proj, gq_pad, gk_nope, gk_pe, rope_tables(seq), seq=seq)
            out_b = flash_attention(q_b, k_b, v_b, batch=batch, seq=seq,
                                    tq=min(FLASH_TQ, seq), ck=FLASH_CK)
            h = matmul_residual([out_a, out_b], ev_w_o[i].astype(bf), h)
        else:
            qkv = norm_matmul(h, attn_norm[layer], od_w_qkv[i].astype(bf))
            q_c, k_c, v_c = prep_c(qkv, od_qn[i], od_kn[i],
                                   alibi_query_columns(seq // MOBA_BLOCK), seq=seq)
            out_c = flash_attention(q_c, k_c, v_c, batch=batch, seq=seq, tq=min(FLASH_TQ, seq),
                                    ck=FLASH_CK)
            h = matmul_residual([out_c], od_w_o[i].astype(bf), h)
        h = moe_layer(h, ffn_norm[layer], w_router_t, router_bias, w_gate, w_up, w_down, layer=layer)
    return h.reshape(batch, seq, d)
```

```python
import functools
import math

import jax
import jax.numpy as jnp
from jax import lax
from jax.experimental import pallas as pl
from jax.experimental.pallas import tpu as pltpu

EPS = 1e-6
NEG = -1e30
LANES = 128
HEAD_DIM = 128
ROPE_DIM = 64
ROPE_THETA = 10000.0
A_HEADS = 8
A_PATTERNS = ((128, 1), (512, 4), (2048, 16))
B_HEADS = 8
Q_LORA = 512
KV_LORA = 256
MLA_QK = 192
MLA_QK_PAD = 256
C_HEADS = 16
MOBA_BLOCK = 256
MOBA_TOPK = 3
N_EXPERTS = 16
N_GROUPS = 4
GROUP = N_EXPERTS // N_GROUPS
TOP_K = 2
EXPERT_FF = 512
A_QKV = 3 * A_HEADS * HEAD_DIM
EVEN_IN_PAD = 4096
KPE_COL = A_QKV + Q_LORA + KV_LORA
MIB = 1024 * 1024
FLASH_TQ = 1024
FLASH_CK = 512
ONES_ROWS = 16
ALIBI_COLS = 6
LOG2E = 1.4426950408889634
MOE_TM = 1024
MOE_FF_SPLIT = 2

_NT = (((1,), (1,)), ((), ()))


def _params(semantics, vmem_mib):
    return pltpu.CompilerParams(dimension_semantics=semantics,
                                vmem_limit_bytes=vmem_mib * MIB)


def _rms_scale(x, width):
    return lax.rsqrt(jnp.sum(x * x, axis=-1, keepdims=True) * (1.0 / width) + EPS)


def _norm_matmul_kernel(x_ref, g_ref, w_ref, o_ref, xn_ref, *, width):
    @pl.when(pl.program_id(1) == 0)
    def _():
        x = x_ref[...]
        xn_ref[...] = (x * _rms_scale(x, width) * g_ref[...]).astype(jnp.bfloat16)

    o_ref[...] = jnp.dot(xn_ref[...], w_ref[...],
                         preferred_element_type=jnp.float32).astype(o_ref.dtype)


def norm_matmul(x, g, w, *, col_block=0, tm=1024, tn=1024):
    n = x.shape[0]
    k, m = w.shape
    return pl.pallas_call(
        functools.partial(_norm_matmul_kernel, width=k),
        grid=(n // tm, m // tn),
        in_specs=[pl.BlockSpec((tm, k), lambda i, j: (i, col_block)),
                  pl.BlockSpec((1, k), lambda i, j: (0, 0)),
                  pl.BlockSpec((k, tn), lambda i, j: (0, j))],
        out_specs=pl.BlockSpec((tm, tn), lambda i, j: (i, j)),
        out_shape=jax.ShapeDtypeStruct((n, m), jnp.float32),
        scratch_shapes=[pltpu.VMEM((tm, k), jnp.bfloat16)],
        compiler_params=_params(("parallel", "arbitrary"), 48),
        name="norm_matmul",
    )(x, g.reshape(1, k), w)


def _matmul_res_kernel(*refs, n_a):
    a_refs, w_refs = refs[:n_a], refs[n_a:2 * n_a]
    r_ref, o_ref = refs[2 * n_a], refs[2 * n_a + 1]
    acc = r_ref[...]
    for a_ref, w_ref in zip(a_refs, w_refs):
        acc = acc + jnp.dot(a_ref[...], w_ref[...], preferred_element_type=jnp.float32)
    o_ref[...] = acc


def matmul_residual(a_list, w, res, *, tm=1024, tn=1024):
    n, m = res.shape
    n_a = len(a_list)
    ka = a_list[0].shape[1]
    in_specs = [pl.BlockSpec((tm, ka), lambda i, j: (i, 0)) for _ in a_list]
    in_specs += [pl.BlockSpec((ka, tn), functools.partial(lambda i, j, p: (p, j), p=p))
                 for p in range(n_a)]
    in_specs += [pl.BlockSpec((tm, tn), lambda i, j: (i, j))]
    return pl.pallas_call(
        functools.partial(_matmul_res_kernel, n_a=n_a),
        grid=(n // tm, m // tn),
        in_specs=in_specs,
        out_specs=pl.BlockSpec((tm, tn), lambda i, j: (i, j)),
        out_shape=jax.ShapeDtypeStruct((n, m), jnp.float32),
        compiler_params=_params(("parallel", "parallel"), 40),
        name="matmul_residual",
    )(*a_list, *([w] * n_a), res)


def _prep_a_kernel(q_ref, k_ref, v_ref, gq_ref, gk_ref, *refs, scale):
    n_pat = len(A_PATTERNS)
    outs, (qs, ks, vs) = refs[:3 * n_pat], refs[3 * n_pat:]
    heads = [slice(h * HEAD_DIM, (h + 1) * HEAD_DIM) for h in range(A_HEADS)]
    for h, sl in enumerate(heads):
        q = q_ref[:, sl]
        qs[h] = q * _rms_scale(q, HEAD_DIM) * gq_ref[...] * scale
        k = k_ref[:, sl]
        ks[h] = k * _rms_scale(k, HEAD_DIM) * gk_ref[...]
        vs[h] = v_ref[:, sl]
    rows = qs.shape[1]
    for p, (_, dil) in enumerate(A_PATTERNS):
        for src, dst in zip((qs, ks, vs), outs[3 * p:3 * p + 3]):
            for r in range(dil):
                for h, sl in enumerate(heads):
                    dst[0, r, :, sl] = src[h, pl.ds(r, rows // dil, stride=dil), :].astype(
                        jnp.bfloat16)


def prep_a(proj, gq, gk, *, batch, seq, tm=512):
    n = proj.shape[0]
    w = A_HEADS * HEAD_DIM
    per_seq = seq // tm
    blk = lambda c: pl.BlockSpec((tm, w), functools.partial(lambda i, c: (i, c), c=c))
    gspec = pl.BlockSpec((1, HEAD_DIM), lambda i: (0, 0))
    out_specs, out_shape = [], []
    for _, dil in A_PATTERNS:
        spec = pl.BlockSpec((1, dil, tm // dil, w), lambda i: (i // per_seq, 0, i % per_seq, 0))
        shape = jax.ShapeDtypeStruct((batch, dil, seq // dil, w), jnp.bfloat16)
        out_specs += [spec] * 3
        out_shape += [shape] * 3
    outs = pl.pallas_call(
        functools.partial(_prep_a_kernel, scale=HEAD_DIM ** -0.5),
        grid=(n // tm,),
        in_specs=[blk(0), blk(1), blk(2), gspec, gspec],
        out_specs=out_specs,
        out_shape=out_shape,
        scratch_shapes=[pltpu.VMEM((A_HEADS, tm, HEAD_DIM), jnp.float32)] * 3,
        compiler_params=_params(("parallel",), 48),
        name="prep_a",
    )(proj, proj, proj, gq.reshape(1, -1), gk.reshape(1, -1))
    return [outs[3 * p:3 * p + 3] for p in range(len(A_PATTERNS))]


def _dilated_kernel(q_ref, kc_ref, vc_ref, o_ref, lse_ref, kprev, vprev, *, dil):
    band = HEAD_DIM
    n = pl.program_id(2)

    @pl.when(n == 0)
    def _():
        kprev[...] = jnp.zeros_like(kprev)
        vprev[...] = jnp.zeros_like(vprev)

    qi = lax.broadcasted_iota(jnp.int32, (band, band), 0)
    ki = lax.broadcasted_iota(jnp.int32, (band, band), 1)
    dist_c = (qi - ki).astype(jnp.float32)
    dist_p = dist_c + float(band)
    valid_c = qi >= ki
    valid_p = jnp.logical_and(ki >= qi, n > 0)
    heads = [slice(h * HEAD_DIM, (h + 1) * HEAD_DIM) for h in range(A_HEADS)]
    logits = []
    for sl in heads:
        q = q_ref[0, 0, :, sl]
        logits.append((lax.dot_general(q, kc_ref[0, 0, :, sl], _NT, preferred_element_type=jnp.float32),
                       lax.dot_general(q, kprev[:, sl], _NT, preferred_element_type=jnp.float32)))
    probs = []
    lse_tile = jnp.zeros((band, LANES), jnp.float32)
    for h, (sc, sp) in enumerate(logits):
        slope = 2.0 ** (-8.0 * (h + 1) / A_HEADS) * dil
        sc = jnp.where(valid_c, sc - slope * dist_c, NEG)
        sp = jnp.where(valid_p, sp - slope * dist_p, NEG)
        m = jnp.maximum(jnp.max(sc, axis=1, keepdims=True), jnp.max(sp, axis=1, keepdims=True))
        pc = jnp.exp(sc - m)
        pp = jnp.exp(sp - m)
        l = jnp.sum(pc, axis=1, keepdims=True) + jnp.sum(pp, axis=1, keepdims=True)
        probs.append((pc.astype(jnp.bfloat16), pp.astype(jnp.bfloat16), l))
        lse_tile = jnp.where(ki == h, m + jnp.log(l), lse_tile)
    lse_ref[0, 0] = lse_tile
    for sl, (pc, pp, l) in zip(heads, probs):
        acc = jnp.dot(pc, vc_ref[0, 0, :, sl], preferred_element_type=jnp.float32)
        acc = acc + jnp.dot(pp, vprev[:, sl], preferred_element_type=jnp.float32)
        o_ref[0, 0, :, sl] = acc / l
    kprev[...] = kc_ref[0, 0]
    vprev[...] = vc_ref[0, 0]


def dilated_pattern(q, k, v, *, dil):
    batch, _, sub, w = q.shape
    band = HEAD_DIM
    cur = pl.BlockSpec((1, 1, band, w), lambda b, r, n: (b, r, n, 0))
    return pl.pallas_call(
        functools.partial(_dilated_kernel, dil=dil),
        grid=(batch, dil, sub // band),
        in_specs=[cur, cur, cur],
        out_specs=[cur, pl.BlockSpec((1, 1, band, LANES), lambda b, r, n: (b, r, n, 0))],
        out_shape=[jax.ShapeDtypeStruct((batch, dil, sub, w), jnp.float32),
                   jax.ShapeDtypeStruct((batch, dil, sub, LANES), jnp.float32)],
        scratch_shapes=[pltpu.VMEM((band, w), jnp.bfloat16)] * 2,
        compiler_params=_params(("parallel", "parallel", "arbitrary"), 40),
        name=f"dilated_d{dil}",
    )(q, k, v)


def _dilated_merge_kernel(*refs):
    n_pat = len(A_PATTERNS)
    ins, out_ref, scratch = refs[:2 * n_pat], refs[2 * n_pat], refs[2 * n_pat + 1:]
    rows = out_ref.shape[0]
    heads = [slice(h * HEAD_DIM, (h + 1) * HEAD_DIM) for h in range(A_HEADS)]
    outs, lses = [], []
    for p, (_, dil) in enumerate(A_PATTERNS):
        o_ref, l_ref = ins[2 * p], ins[2 * p + 1]
        if dil == 1:
            outs.append(lambda h, o_ref=o_ref: o_ref[0, 0, :, heads[h]])
            lses.append(l_ref[0, 0])
            continue
        o_nat, l_nat = scratch[2 * p], scratch[2 * p + 1]
        for r in range(dil):
            for h, sl in enumerate(heads):
                o_nat[h, pl.ds(r, rows // dil, stride=dil), :] = o_ref[0, r, :, sl]
            l_nat[pl.ds(r, rows // dil, stride=dil), :] = l_ref[0, r]
        outs.append(lambda h, o_nat=o_nat: o_nat[h])
        lses.append(l_nat[...])
    for h, sl in enumerate(heads):
        lse_h = [l[:, h:h + 1] for l in lses]
        top = functools.reduce(jnp.maximum, lse_h)
        wts = [jnp.exp(l - top) for l in lse_h]
        num = sum(wt * o(h) for wt, o in zip(wts, outs))
        out_ref[:, sl] = (num / sum(wts)).astype(out_ref.dtype)


def dilated_attention(qkv_by_pattern, *, batch, seq, tm=512):
    w = A_HEADS * HEAD_DIM
    per_seq = seq // tm
    ins, in_specs, scratch = [], [], []
    for (q, k, v), (window, dil) in zip(qkv_by_pattern, A_PATTERNS):
        assert window // dil == HEAD_DIM
        o, lse = dilated_pattern(q, k, v, dil=dil)
        ins += [o, lse]
        in_specs += [pl.BlockSpec((1, dil, tm // dil, w), lambda i: (i // per_seq, 0, i % per_seq, 0)),
                     pl.BlockSpec((1, dil, tm // dil, LANES),
                                  lambda i: (i // per_seq, 0, i % per_seq, 0))]
        scratch += [pltpu.VMEM((A_HEADS, tm, HEAD_DIM), jnp.float32),
                    pltpu.VMEM((tm, LANES), jnp.float32)]
    n = batch * seq
    return pl.pallas_call(
        _dilated_merge_kernel,
        grid=(n // tm,),
        in_specs=in_specs,
        out_specs=pl.BlockSpec((tm, w), lambda i: (i, 0)),
        out_shape=jax.ShapeDtypeStruct((n, w), jnp.bfloat16),
        scratch_shapes=scratch,
        compiler_params=_params(("parallel",), 48),
        name="dilated_merge",
    )(*ins)


def _rope(t, cos, sin_lo, sin_hi):
    return (t * cos + pltpu.roll(t, LANES - ROPE_DIM // 2, 1) * sin_lo
            + pltpu.roll(t, ROPE_DIM // 2, 1) * sin_hi)


def _prep_b_kernel(qb_ref, ckv_ref, kpe_ref, gq_ref, gkn_ref, gkp_ref, cos_ref, slo_ref, shi_ref,
                   qo_ref, ko_ref, vo_ref, *, scale):
    cos, slo, shi = cos_ref[...], slo_ref[...], shi_ref[...]
    kpe = kpe_ref[...]
    kpe_ss = jnp.sum(kpe * kpe, axis=-1, keepdims=True)
    kpe_rot = _rope(kpe * gkp_ref[...], cos, slo, shi)
    for h in range(B_HEADS):
        lo = slice(h * MLA_QK_PAD, h * MLA_QK_PAD + HEAD_DIM)
        hi = slice(h * MLA_QK_PAD + HEAD_DIM, (h + 1) * MLA_QK_PAD)
        q_n, q_p = qb_ref[:, lo], qb_ref[:, hi]
        ss = jnp.sum(q_n * q_n, axis=-1, keepdims=True) + jnp.sum(q_p * q_p, axis=-1, keepdims=True)
        rq = lax.rsqrt(ss * (1.0 / MLA_QK) + EPS) * scale
        qo_ref[h, :, :HEAD_DIM] = (q_n * rq * gq_ref[:, :HEAD_DIM]).astype(jnp.bfloat16)
        qo_ref[h, :, HEAD_DIM:] = _rope(q_p * rq * gq_ref[:, HEAD_DIM:], cos, slo, shi
                                        ).astype(jnp.bfloat16)
        k_n = ckv_ref[:, lo]
        ss = jnp.sum(k_n * k_n, axis=-1, keepdims=True) + kpe_ss
        rk = lax.rsqrt(ss * (1.0 / MLA_QK) + EPS)
        ko_ref[h, :, :HEAD_DIM] = (k_n * rk * gkn_ref[...]).astype(jnp.bfloat16)
        ko_ref[h, :, HEAD_DIM:] = (kpe_rot * rk).astype(jnp.bfloat16)
        vo_ref[h, :HEAD_DIM, :] = ckv_ref[:, hi].T.astype(jnp.bfloat16)
        vo_ref[h, HEAD_DIM:, :] = jnp.ones((ONES_ROWS, ckv_ref.shape[0]), jnp.bfloat16)


def prep_b(qb, ckv, proj, gq_pad, gk_nope, gk_pe_pad, rope_tabs, *, seq, tm=256):
    n = qb.shape[0]
    wide = B_HEADS * MLA_QK_PAD
    nblk = seq // tm
    row = lambda c: pl.BlockSpec((tm, wide), lambda i: (i, 0))
    tab = pl.BlockSpec((tm, LANES), lambda i: (i % nblk, 0))
    return pl.pallas_call(
        functools.partial(_prep_b_kernel, scale=MLA_QK ** -0.5 * LOG2E),
        grid=(n // tm,),
        in_specs=[row(0), row(0),
                  pl.BlockSpec((tm, LANES), lambda i: (i, KPE_COL // LANES)),
                  pl.BlockSpec((1, MLA_QK_PAD), lambda i: (0, 0)),
                  pl.BlockSpec((1, LANES), lambda i: (0, 0)),
                  pl.BlockSpec((1, LANES), lambda i: (0, 0)),
                  tab, tab, tab],
        out_specs=[pl.BlockSpec((B_HEADS, tm, MLA_QK_PAD), lambda i: (0, i, 0)),
                   pl.BlockSpec((B_HEADS, tm, MLA_QK_PAD), lambda i: (0, i, 0)),
                   pl.BlockSpec((B_HEADS, HEAD_DIM + ONES_ROWS, tm), lambda i: (0, 0, i))],
        out_shape=[jax.ShapeDtypeStruct((B_HEADS, n, MLA_QK_PAD), jnp.bfloat16),
                   jax.ShapeDtypeStruct((B_HEADS, n, MLA_QK_PAD), jnp.bfloat16),
                   jax.ShapeDtypeStruct((B_HEADS, HEAD_DIM + ONES_ROWS, n), jnp.bfloat16)],
        compiler_params=_params(("parallel",), 40),
        name="prep_b",
    )(qb, ckv, proj, gq_pad, gk_nope, gk_pe_pad, *rope_tabs)


def rope_tables(seq):
    half = ROPE_DIM // 2
    inv = ROPE_THETA ** (-jnp.arange(0, ROPE_DIM, 2, dtype=jnp.float32) / ROPE_DIM)
    ang = jnp.arange(seq, dtype=jnp.float32)[:, None] * inv[None, :]
    cos, sin = jnp.cos(ang), jnp.sin(ang)
    z = jnp.zeros((seq, half), jnp.float32)
    pad = jnp.zeros((seq, LANES - ROPE_DIM), jnp.float32)
    return (jnp.concatenate([cos, cos, pad], axis=1),
            jnp.concatenate([-sin, z, pad], axis=1),
            jnp.concatenate([z, sin, pad], axis=1))


def _flash_kernel(q_ref, k_ref, vt_ref, o_ref, m_ref, acc_ref, s_buf, p_buf, alpha_buf, *, tq, ck, dv):
    i = pl.program_id(2)
    nb = tq // ck
    n_past = i * nb
    m_ref[...] = jnp.full_like(m_ref, NEG)
    acc_ref[...] = jnp.zeros_like(acc_ref)

    def scores(j, r0):
        start = pl.multiple_of(j * ck, ck)
        return lax.dot_general(k_ref[0, pl.ds(start, ck), :], q_ref[0, r0:, :], _NT,
                               preferred_element_type=jnp.float32)

    def fold(j, r0, s):
        start = pl.multiple_of(j * ck, ck)
        probs, alphas = [], []
        for c0 in range(0, tq - r0, LANES):
            cols = slice(r0 + c0, r0 + c0 + LANES)
            s_c = s[:, c0:c0 + LANES]
            m_prev = m_ref[:, cols]
            m_new = jnp.maximum(m_prev, jnp.max(s_c, axis=0, keepdims=True))
            m_ref[:, cols] = m_new
            alphas.append(jnp.exp2(m_prev - m_new))
            probs.append(jnp.exp2(s_c - m_new).astype(jnp.bfloat16))
        p = jnp.concatenate(probs, axis=1)
        alpha = jnp.concatenate(alphas, axis=1)
        acc_ref[:, r0:] = alpha * acc_ref[:, r0:] + jnp.dot(
            vt_ref[0, :, pl.ds(start, ck)], p, preferred_element_type=jnp.float32)

    def stage(c, slot):
        other = 1 - slot
        s_buf[other] = scores(c + 1, 0)
        start = pl.multiple_of(jnp.maximum(c - 1, 0) * ck, ck)
        acc_ref[...] = alpha_buf[other] * acc_ref[...] + jnp.dot(
            vt_ref[0, :, pl.ds(start, ck)], p_buf[other], preferred_element_type=jnp.float32)
        for c0 in range(0, tq, LANES):
            cols = slice(c0, c0 + LANES)
            s_c = s_buf[slot, :, cols]
            m_prev = m_ref[:, cols]
            m_new = jnp.maximum(m_prev, jnp.max(s_c, axis=0, keepdims=True))
            m_ref[:, cols] = m_new
            alpha_buf[slot, :, cols] = jnp.exp2(m_prev - m_new)
            p_buf[slot, :, cols] = jnp.exp2(s_c - m_new).astype(jnp.bfloat16)

    def pair(jj, carry):
        stage(2 * jj, 0)
        stage(2 * jj + 1, 1)
        return carry

    s_buf[0] = scores(0, 0)
    p_buf[1] = jnp.zeros((ck, tq), jnp.bfloat16)
    alpha_buf[1] = jnp.ones((1, tq), jnp.float32)
    lax.fori_loop(0, n_past // 2, pair, 0)
    start = pl.multiple_of(jnp.maximum(n_past - 1, 0) * ck, ck)
    acc_ref[...] = alpha_buf[1] * acc_ref[...] + jnp.dot(
        vt_ref[0, :, pl.ds(start, ck)], p_buf[1], preferred_element_type=jnp.float32)
    s = s_buf[0]
    band_logits = [s] + [scores(n_past + jb, jb * ck) for jb in range(1, nb)]
    for jb, s in enumerate(band_logits):
        j = n_past + jb
        r0 = jb * ck
        ki = lax.broadcasted_iota(jnp.int32, (ck, ck), 0)
        qi = lax.broadcasted_iota(jnp.int32, (ck, ck), 1)
        own = jnp.where(qi >= ki, s[:, :ck], NEG)
        fold(j, r0, own if tq - r0 == ck else jnp.concatenate([own, s[:, ck:]], axis=1))
    acc = acc_ref[...]
    o_ref[...] = (acc[:dv] / acc[dv:dv + 1]).T.astype(o_ref.dtype)


def flash_attention(q, k, vt, *, batch, seq, tq, ck):
    heads, n, dq = q.shape
    dvp = vt.shape[1]
    dv = dvp - ONES_ROWS
    nq = seq // tq
    assert seq % tq == 0 and tq % (2 * ck) == 0
    return pl.pallas_call(
        functools.partial(_flash_kernel, tq=tq, ck=ck, dv=dv),
        grid=(batch, heads, nq),
        in_specs=[pl.BlockSpec((1, tq, dq), lambda b, h, i: (h, b * nq + i, 0)),
                  pl.BlockSpec((1, seq, dq), lambda b, h, i: (h * batch + b, 0, 0)),
                  pl.BlockSpec((1, dvp, seq), lambda b, h, i: (h, 0, b))],
        out_specs=pl.BlockSpec((tq, dv), lambda b, h, i: (b * nq + i, h)),
        out_shape=jax.ShapeDtypeStruct((n, heads * dv), jnp.bfloat16),
        scratch_shapes=[pltpu.VMEM((1, tq), jnp.float32),
                        pltpu.VMEM((dvp, tq), jnp.float32),
                        pltpu.VMEM((2, ck, tq), jnp.float32),
                        pltpu.VMEM((2, ck, tq), jnp.bfloat16),
                        pltpu.VMEM((2, 1, tq), jnp.float32)],
        compiler_params=_params(("parallel", "parallel", "arbitrary"), 48),
        name=f"flash_h{heads}",
    )(q, k.reshape(heads * batch, seq, dq), vt)


def _prep_c_kernel(q_ref, k_ref, v_ref, gq_ref, gk_ref, alibi_ref, qo_ref, ko_ref, vo_ref,
                   kmean_ref, *, blocks_per_seq, scale):
    i = pl.program_id(0)
    bq = i % blocks_per_seq

    @pl.when(bq == 0)
    def _():
        kmean_ref[...] = jnp.zeros_like(kmean_ref)

    rows = q_ref.shape[0]
    nblk = blocks_per_seq
    blk = lax.broadcasted_iota(jnp.int32, (nblk, rows), 0)
    lane = lax.broadcasted_iota(jnp.int32, (rows, LANES), 1)
    kpos = bq * rows + lax.broadcasted_iota(jnp.int32, (rows, LANES), 0)
    pos_part = jnp.where((lane - nblk) % 2 == 0, jnp.right_shift(kpos, 7), kpos & (LANES - 1))
    k_extra = jnp.where(lane < nblk, (lane == bq).astype(jnp.float32),
                        jnp.where(lane < nblk + ALIBI_COLS, pos_part.astype(jnp.float32), 0.0))
    k_extra = k_extra.astype(jnp.bfloat16)
    ones = jnp.ones((ONES_ROWS, rows), jnp.bfloat16)
    gates = []
    for h in range(C_HEADS):
        sl = slice(h * HEAD_DIM, (h + 1) * HEAD_DIM)
        q = q_ref[:, sl]
        qn = q * _rms_scale(q, HEAD_DIM) * gq_ref[...]
        k = k_ref[:, sl]
        kn = k * _rms_scale(k, HEAD_DIM) * gk_ref[...]
        qo_ref[h, :, :HEAD_DIM] = (qn * scale).astype(jnp.bfloat16)
        ko_ref[h, :, :HEAD_DIM] = kn.astype(jnp.bfloat16)
        ko_ref[h, :, HEAD_DIM:] = k_extra
        vo_ref[h, :HEAD_DIM, :] = v_ref[:, sl].T.astype(jnp.bfloat16)
        vo_ref[h, HEAD_DIM:, :] = ones
        gates.append(lax.dot_general(kmean_ref[h], qn, _NT, precision=lax.Precision.HIGHEST,
                                     preferred_element_type=jnp.float32))
        kmean_ref[h, pl.ds(bq, 1), :] = jnp.mean(kn, axis=0, keepdims=True)
    for h, gate in enumerate(gates):
        gate = jnp.where(blk < bq, gate, NEG)
        allowed = blk == bq
        for r in range(MOBA_TOPK):
            best = jnp.max(gate, axis=0, keepdims=True)
            idx = jnp.min(jnp.where(gate == best, blk, nblk), axis=0, keepdims=True)
            allowed = allowed | ((blk == idx) & (r < bq))
            gate = jnp.where(blk == idx, 2.0 * NEG, gate)
        block_bias = jnp.where(allowed, 0.0, NEG)
        block_bias = jnp.concatenate(
            [block_bias, jnp.zeros((LANES - nblk, rows), jnp.float32)], axis=0)
        qo_ref[h, :, HEAD_DIM:] = (block_bias.T + alibi_ref[h]).astype(jnp.bfloat16)


def prep_c(qkv, gq, gk, alibi_q, *, seq):
    n = qkv.shape[0]
    tm = MOBA_BLOCK
    w = C_HEADS * HEAD_DIM
    blocks_per_seq = seq // tm
    assert blocks_per_seq + ALIBI_COLS <= LANES
    blk = lambda c: pl.BlockSpec((tm, w), functools.partial(lambda i, c: (i, c), c=c))
    gspec = pl.BlockSpec((1, HEAD_DIM), lambda i: (0, 0))
    hm = pl.BlockSpec((C_HEADS, tm, 2 * HEAD_DIM), lambda i: (0, i, 0))
    hm_shape = jax.ShapeDtypeStruct((C_HEADS, n, 2 * HEAD_DIM), jnp.bfloat16)
    dvp = HEAD_DIM + ONES_ROWS
    return pl.pallas_call(
        functools.partial(_prep_c_kernel, blocks_per_seq=blocks_per_seq,
                          scale=HEAD_DIM ** -0.5 * LOG2E),
        grid=(n // tm,),
        in_specs=[blk(0), blk(1), blk(2), gspec, gspec,
                  pl.BlockSpec((C_HEADS, 1, LANES), lambda i: (0, 0, 0))],
        out_specs=[hm, hm, pl.BlockSpec((C_HEADS, dvp, tm), lambda i: (0, 0, i))],
        out_shape=[hm_shape, hm_shape, jax.ShapeDtypeStruct((C_HEADS, dvp, n), jnp.bfloat16)],
        scratch_shapes=[pltpu.VMEM((C_HEADS, blocks_per_seq, HEAD_DIM), jnp.float32)],
        compiler_params=_params(("arbitrary",), 40),
        name="prep_c",
    )(qkv, qkv, qkv, gq.reshape(1, -1), gk.reshape(1, -1), alibi_q)


def alibi_query_columns(blocks_per_seq):
    sigma = 2.0 ** (-8.0 * jnp.arange(1, C_HEADS + 1, dtype=jnp.float32) / C_HEADS) * LOG2E
    pieces = []
    rest = sigma
    for _ in range(ALIBI_COLS // 2):
        piece = rest.astype(jnp.bfloat16).astype(jnp.float32)
        pieces += [piece * float(LANES), piece]
        rest = rest - piece
    cols = jnp.stack(pieces, axis=1)
    cols = jnp.pad(cols, ((0, 0), (blocks_per_seq, LANES - blocks_per_seq - ALIBI_COLS)))
    return cols.reshape(C_HEADS, 1, LANES)


def _router_kernel(h_ref, g_ref, wr_ref, b_ref, gate_ref, grp_ref):
    x = h_ref[...]
    t = x * _rms_scale(x, x.shape[1]) * g_ref[...]
    logits = lax.dot_general(wr_ref[...], t, _NT, precision=lax.Precision.HIGHEST,
                             preferred_element_type=jnp.float32)
    scores = 1.0 / (1.0 + jnp.exp(-logits))
    biased = scores + b_ref[...]
    s = [scores[e:e + 1, :] for e in range(N_EXPERTS)]
    b = [biased[e:e + 1, :] for e in range(N_EXPERTS)]
    best_val, best_grp = None, None
    for g in range(N_GROUPS):
        mem = b[g * GROUP:(g + 1) * GROUP]
        top2 = None
        for x1 in range(GROUP):
            for x2 in range(x1 + 1, GROUP):
                pair = mem[x1] + mem[x2]
                top2 = pair if top2 is None else jnp.maximum(top2, pair)
        if g == 0:
            best_val, best_grp = top2, jnp.zeros_like(top2, dtype=jnp.int32)
        else:
            better = top2 > best_val
            best_grp = jnp.where(better, g, best_grp)
            best_val = jnp.where(better, top2, best_val)
    picked = []
    for e in range(N_EXPERTS):
        g = e // GROUP
        ahead = jnp.zeros_like(best_grp)
        for o in range(g * GROUP, (g + 1) * GROUP):
            if o == e:
                continue
            wins = (b[o] > b[e]) | ((b[o] == b[e]) & (o < e))
            ahead = ahead + wins.astype(jnp.int32)
        picked.append(jnp.where((best_grp == g) & (ahead < TOP_K), s[e], 0.0))
    total = picked[0]
    for e in range(1, N_EXPERTS):
        total = total + picked[e]
    for e in range(N_EXPERTS):
        gate_ref[e:e + 1, :] = picked[e] / total
    grp_ref[...] = best_grp


def router(h, g, w_router_t, bias, *, tm=512):
    n, d = h.shape
    return pl.pallas_call(
        _router_kernel,
        grid=(n // tm,),
        in_specs=[pl.BlockSpec((tm, d), lambda i: (i, 0)),
                  pl.BlockSpec((1, d), lambda i: (0, 0)),
                  pl.BlockSpec((N_EXPERTS, d), lambda i: (0, 0)),
                  pl.BlockSpec((N_EXPERTS, 1), lambda i: (0, 0))],
        out_specs=[pl.BlockSpec((N_EXPERTS, tm), lambda i: (0, i)),
                   pl.BlockSpec((1, tm), lambda i: (0, i))],
        out_shape=[jax.ShapeDtypeStruct((N_EXPERTS, n), jnp.float32),
                   jax.ShapeDtypeStruct((1, n), jnp.int32)],
        compiler_params=_params(("parallel",), 40),
        name="router",
    )(h, g.reshape(1, d), w_router_t, bias.reshape(N_EXPERTS, 1))


def _row_copies(tok_ref, base, rows, hbm_ref, buf_ref, sem, to_hbm):
    def copy(r):
        hbm_row = hbm_ref.at[pl.ds(tok_ref[base + r], 1), :]
        buf_row = buf_ref.at[pl.ds(r, 1), :]
        src, dst = (buf_row, hbm_row) if to_hbm else (hbm_row, buf_row)
        return pltpu.make_async_copy(src, dst, sem)

    def start(r, carry):
        copy(r).start()
        return carry

    def wait(r, carry):
        copy(r).wait()
        return carry

    lax.fori_loop(0, rows, start, 0, unroll=8)
    lax.fori_loop(0, rows, wait, 0, unroll=8)


def _moe_kernel(tile_ref, grp_ref, flag_ref, tok_ref, h_hbm, g_ref, gate_ref, wg_ref, wu_ref, wd_ref,
                out_hbm, hbuf, tbuf, sem, *, tm, halves):
    j = pl.program_id(0)
    sub = pl.program_id(1)
    flag = flag_ref[j]
    base = tile_ref[j] * tm

    @pl.when((sub == 0) & ((flag & 2) != 0))
    def _():
        _row_copies(tok_ref, base, tm, h_hbm, hbuf, sem, to_hbm=False)
        x = hbuf[...]
        tbuf[...] = (x * _rms_scale(x, x.shape[1]) * g_ref[...]).astype(jnp.bfloat16)

    @pl.when((flag & 1) != 0)
    def _():
        e = grp_ref[j] * GROUP + sub // halves
        t = tbuf[...]
        a = jnp.dot(t, wg_ref[0, 0].astype(jnp.bfloat16), preferred_element_type=jnp.float32)
        u = jnp.dot(t, wu_ref[0, 0].astype(jnp.bfloat16), preferred_element_type=jnp.float32)
        hid = (a / (1.0 + jnp.exp(-a))) * u
        y = jnp.dot(hid.astype(jnp.bfloat16), wd_ref[0, 0].astype(jnp.bfloat16),
                    preferred_element_type=jnp.float32)
        gates = gate_ref[...]
        lane = lax.broadcasted_iota(jnp.int32, gates.shape, 1)
        g_col = jnp.sum(jnp.where(lane == e, gates, 0.0), axis=1, keepdims=True)
        hbuf[...] += g_col * y

    @pl.when((sub == pl.num_programs(1) - 1) & ((flag & 4) != 0))
    def _():
        _row_copies(tok_ref, base, tm, out_hbm, hbuf, sem, to_hbm=True)


def _moe_worklist(grp, tm):
    n = grp.shape[0]
    n_tiles = n // tm
    order = jnp.argsort(grp, stable=True).astype(jnp.int32)
    counts = jnp.sum(grp[None, :] == jnp.arange(N_GROUPS, dtype=jnp.int32)[:, None], axis=1)
    ends = jnp.cumsum(counts)
    first_slot = jnp.arange(n_tiles, dtype=jnp.int32) * tm
    g_lo = jnp.sum(ends[None, :] <= first_slot[:, None], axis=1)
    g_hi = jnp.sum(ends[None, :] <= (first_slot + tm - 1)[:, None], axis=1)
    cnt = g_hi - g_lo + 1
    first_item = jnp.cumsum(cnt) - cnt
    total = first_item[-1] + cnt[-1]
    n_items = n_tiles + N_GROUPS - 1
    item = jnp.arange(n_items, dtype=jnp.int32)
    tile = jnp.minimum(jnp.sum((first_item + cnt)[None, :] <= item[:, None], axis=1), n_tiles - 1)
    valid = item < total
    grp_of = jnp.where(valid, g_lo[tile] + item - first_item[tile], g_hi[n_tiles - 1])
    is_first = valid & (item == first_item[tile])
    is_last = valid & (item == first_item[tile] + cnt[tile] - 1)
    flag = valid.astype(jnp.int32) + 2 * is_first.astype(jnp.int32) + 4 * is_last.astype(jnp.int32)
    return order, tile.astype(jnp.int32), grp_of.astype(jnp.int32), flag


def moe_grouped(h, ffn_g, gate_t, grp, wg, wu, wd, *, layer, tm=MOE_TM, halves=MOE_FF_SPLIT):
    n, d = h.shape
    ff = wg.shape[3] // halves
    order, tile, grp_of, flag = _moe_worklist(grp.reshape(n), tm)
    gates_sorted = jnp.take(gate_t.T, order, axis=0)
    expert = lambda j, sub, tile_ref, grp_ref, flag_ref, tok_ref: grp_ref[j] * GROUP + sub // halves
    grid_spec = pltpu.PrefetchScalarGridSpec(
        num_scalar_prefetch=4,
        grid=(tile.shape[0], GROUP * halves),
        in_specs=[pl.BlockSpec(memory_space=pl.ANY),
                  pl.BlockSpec((1, d), lambda j, sub, *_: (0, 0)),
                  pl.BlockSpec((tm, N_EXPERTS), lambda j, sub, tile_ref, *_: (tile_ref[j], 0)),
                  pl.BlockSpec((1, 1, d, ff),
                               lambda j, sub, *p: (layer, expert(j, sub, *p), 0, sub % halves)),
                  pl.BlockSpec((1, 1, d, ff),
                               lambda j, sub, *p: (layer, expert(j, sub, *p), 0, sub % halves)),
                  pl.BlockSpec((1, 1, ff, d),
                               lambda j, sub, *p: (layer, expert(j, sub, *p), sub % halves, 0))],
        out_specs=pl.BlockSpec(memory_space=pl.ANY),
        scratch_shapes=[pltpu.VMEM((tm, d), jnp.float32),
                        pltpu.VMEM((tm, d), jnp.bfloat16),
                        pltpu.SemaphoreType.DMA(())])
    return pl.pallas_call(
        functools.partial(_moe_kernel, tm=tm, halves=halves),
        grid_spec=grid_spec,
        out_shape=jax.ShapeDtypeStruct((n, d), jnp.float32),
        compiler_params=_params(("arbitrary", "arbitrary"), 48),
        name="moe_grouped",
    )(tile, grp_of, flag, order, h, ffn_g.reshape(1, d), gates_sorted, wg, wu, wd)


def moe_layer(h, ffn_g, w_router_t, router_bias, wg, wu, wd, *, layer):
    gate_t, grp = router(h, ffn_g, w_router_t, router_bias)
    return moe_grouped(h, ffn_g, gate_t, grp, wg, wu, wd, layer=layer)


def _pad_cols(w, width):
    return jnp.pad(w, ((0, 0), (0, width - w.shape[1])))


def _pad_heads(w, heads, real, padded):
    k = w.shape[0]
    w = w.reshape(k, heads, real)
    return jnp.pad(w, ((0, 0), (0, 0), (0, padded - real))).reshape(k, heads * padded)


def kernel(x, attn_norm, ev_w_in, ev_q_a_norm, ev_w_q_b, ev_kv_a_norm, ev_w_kv_b, ev_qn_a, ev_kn_a, ev_qn_b, ev_kn_b, ev_w_o, od_w_qkv, od_qn, od_kn, od_w_o, ffn_norm, w_router, router_bias, w_gate, w_up, w_down):
    batch, seq, d = x.shape
    n = batch * seq
    bf = jnp.bfloat16
    h = x.reshape(n, d)
    w_router_t = w_router.T
    depth = attn_norm.shape[0]
    for layer in range(depth):
        i = layer // 2
        if layer % 2 == 0:
            w_in = _pad_cols(ev_w_in[i], EVEN_IN_PAD).astype(bf)
            proj = norm_matmul(h, attn_norm[layer], w_in)
            qkv_a = prep_a(proj, ev_qn_a[i], ev_kn_a[i], batch=batch, seq=seq)
            out_a = dilated_attention(qkv_a, batch=batch, seq=seq)
            w_qb = _pad_heads(ev_w_q_b[i], B_HEADS, MLA_QK, MLA_QK_PAD).astype(bf)
            qb = norm_matmul(proj, ev_q_a_norm[i], w_qb, col_block=A_QKV // Q_LORA)
            ckv = norm_matmul(proj, ev_kv_a_norm[i], ev_w_kv_b[i].astype(bf),
                              col_block=(A_QKV + Q_LORA) // KV_LORA)
            gq_pad = jnp.pad(ev_qn_b[i], (0, MLA_QK_PAD - MLA_QK)).reshape(1, -1)
            gk_nope = ev_kn_b[i][:HEAD_DIM].reshape(1, -1)
            gk_pe = jnp.pad(ev_kn_b[i][HEAD_DIM:], (0, LANES - ROPE_DIM)).reshape(1, -1)
            q_b, k_b, v_b = prep_b(qb, ckv, proj, gq_pad, gk_nope, gk_pe, rope_tables(seq), seq=seq)
            out_b = flash_attention(q_b, k_b, v_b, batch=batch, seq=seq,
                                    tq=min(FLASH_TQ, seq), ck=FLASH_CK)
            h = matmul_residual([out_a, out_b], ev_w_o[i].astype(bf), h)
        else:
            qkv = norm_matmul(h, attn_norm[layer], od_w_qkv[i].astype(bf))
            q_c, k_c, v_c = prep_c(qkv, od_qn[i], od_kn[i],
                                   alibi_query_columns(seq // MOBA_BLOCK), seq=seq)
            out_c = flash_attention(q_c, k_c, v_c, batch=batch, seq=seq, tq=min(FLASH_TQ, seq),
                                    ck=FLASH_CK)
            h = matmul_residual([out_c], od_w_o[i].astype(bf), h)
        h = moe_layer(h, ffn_norm[layer], w_router_t, router_bias, w_gate, w_up, w_down, layer=layer)
    return h.reshape(batch, seq, d)
```
